```python
import jax
import jax.numpy as jnp
from jax import lax
import numpy as np

D_MODEL = 4096
BATCH = 4
SEQ = 2048
DEPTH = 1
DEC_BATCH = 32
DEC_SEQ = 1
PAST_LEN = 8192
PAGE_SIZE = 128

N_HEADS = 16
N_KV_HEADS = 4
GROUP = N_HEADS // N_KV_HEADS
HEAD_DIM = 128
ROT_DIM = HEAD_DIM // 4
ROPE_THETA = 500000.0
L_CMP = 32
STRIDE = 16
L_SEL = 64
N_SEL = 16
WINDOW = 512
WIN_Q_BLOCK = 128
SEL_Q_CHUNK = 32
FORCE_BONUS = 1000.0
SCALE = HEAD_DIM ** -0.5
NEG_INF = -1e30
SG_WIDTH = D_MODEL // 2
SG_GROUPS = 16
SG_GROUP_DIM = SG_WIDTH // SG_GROUPS
SG_CHUNK = 128
D_FF = 11008
CONV_W = 3
EPS = 1e-6
Q_W = N_HEADS * HEAD_DIM
KV_W = N_KV_HEADS * HEAD_DIM
NSA_GATE_W = 3 * N_HEADS
S_Q = Q_W
S_KC = S_Q + KV_W
S_VC = S_KC + KV_W
S_KS = S_VC + KV_W
S_VS = S_KS + KV_W
S_KW = S_VS + KV_W
S_VW = S_KW + KV_W
S_G = S_VW + NSA_GATE_W
S_UV = S_G + 2 * SG_WIDTH
IN_W = S_UV + 2 * D_MODEL
IN_SPLITS = (S_Q, S_KC, S_VC, S_KS, S_VS, S_KW, S_VW, S_G, S_UV)

kernel_name = 'nsa_sgu_convffn_hybrid_step'


def rms_norm(x, g):
    x32 = x.astype(jnp.float32)
    y = x32 * lax.rsqrt(jnp.mean(x32 * x32, axis=-1, keepdims=True) + EPS)
    return (y * g.astype(jnp.float32)).astype(x.dtype)


def rope(x, pos):
    half = ROT_DIM // 2
    inv_freq = jnp.power(jnp.float32(ROPE_THETA), -jnp.arange(half, dtype=jnp.float32) / half)
    ang = pos.astype(jnp.float32)[:, None] * inv_freq[None, :]
    cos = jnp.cos(ang)[None, :, None, :]
    sin = jnp.sin(ang)[None, :, None, :]
    xr = x[..., :ROT_DIM].astype(jnp.float32)
    x1, x2 = xr[..., :half], xr[..., half:]
    rot = jnp.concatenate([x1 * cos - x2 * sin, x2 * cos + x1 * sin], axis=-1)
    return jnp.concatenate([rot.astype(x.dtype), x[..., ROT_DIM:]], axis=-1)


def to_groups(q):
    B, T = q.shape[:2]
    return q.reshape(B, T, N_KV_HEADS, GROUP, HEAD_DIM)


def masked_attend(q, k, v, mask):
    s = jnp.einsum('...qgrd,...kgd->...qgrk', q, k, preferred_element_type=jnp.float32) * SCALE
    m = mask[..., :, None, None, :]
    p = jax.nn.softmax(jnp.where(m, s, NEG_INF), axis=-1) * m.astype(jnp.float32)
    out = jnp.einsum('...qgrk,...kgd->...qgrd', p.astype(v.dtype), v)
    return out, p


def compress(k, pool_w, bias, w):
    B, T = k.shape[:2]
    nsub = T // STRIDE
    ks = k[:, :nsub * STRIDE].reshape(B, nsub, STRIDE, N_KV_HEADS, HEAD_DIM)
    first = jnp.einsum('bnjgd,jd->bngd', ks, pool_w[:STRIDE])
    second = jnp.einsum('bnjgd,jd->bngd', ks, pool_w[STRIDE:])
    pooled = first[:, :-1] + second[:, 1:] + bias
    return jnp.einsum('bngd,de->bnge', jax.nn.silu(pooled), w)


def cmp_attend(q, ck, cv, q_pos):
    n = ck.shape[1]
    end = jnp.arange(n) * STRIDE + (L_CMP - 1)
    mask = end[None, :] <= q_pos[:, None]
    return masked_attend(to_groups(q), ck, cv, mask[None])


def select_blocks(p, q_pos, n_keys):
    pg = p.sum(axis=3)
    n_cmp = pg.shape[-1]
    n_blk = max(-(-n_keys // L_SEL), N_SEL)
    c_start = jnp.arange(n_cmp)[:, None] * STRIDE
    b_start = jnp.arange(n_blk)[None, :] * L_SEL
    overlap = ((c_start < b_start + L_SEL) & (c_start + L_CMP > b_start)).astype(jnp.float32)
    imp = jnp.einsum('btgn,nj->btgj', pg, overlap)
    qb = (q_pos // L_SEL)[:, None]
    j = jnp.arange(n_blk)[None, :]
    causal = j <= qb
    forced = ((j == 0) | (j == qb) | (j == qb - 1)).astype(jnp.float32)
    score = jnp.where(causal[None, :, None, :], imp + FORCE_BONUS * forced[None, :, None, :], -jnp.inf)
    vals, idx = lax.top_k(score, N_SEL)
    return idx, jnp.isfinite(vals)


def sel_attend(qg, k, v, idx, valid, q_pos):
    B, Tq, G, S, L, D = k.shape
    kpos = idx[..., None] * L_SEL + jnp.arange(L_SEL)
    mask = (valid[..., None] & (kpos <= q_pos[None, :, None, None, None])).reshape(B, Tq, G, 1, S * L)
    s = jnp.einsum('btgrd,btgkd->btgrk', qg, k.reshape(B, Tq, G, S * L, D),
                   preferred_element_type=jnp.float32) * SCALE
    p = jax.nn.softmax(jnp.where(mask, s, NEG_INF), axis=-1)
    return jnp.einsum('btgrk,btgkd->btgrd', p.astype(v.dtype), v.reshape(B, Tq, G, S * L, D))


def gather_blocks(k, idx):
    B, T = k.shape[:2]
    kpos = jnp.clip(idx[..., None] * L_SEL + jnp.arange(L_SEL), 0, T - 1)
    b_i = jnp.arange(B)[:, None, None, None, None]
    g_i = jnp.arange(N_KV_HEADS)[None, None, :, None, None]
    return k[b_i, kpos, g_i]


def gather_paged_blocks(pool, new, page_table, idx):
    Bd, n_pages = page_table.shape
    page = pool.shape[1]
    per_page = page // L_SEL
    past_blocks = n_pages * per_page
    in_past = idx < past_blocks
    jp = jnp.minimum(idx, past_blocks - 1)
    b4 = jnp.arange(Bd)[:, None, None, None]
    phys = page_table[b4, jp // per_page]
    off = (jp % per_page)[..., None] * L_SEL + jnp.arange(L_SEL)
    g_i = jnp.arange(N_KV_HEADS)[None, None, :, None, None]
    k_past = pool[phys[..., None], off, g_i]
    tpos = jnp.clip((idx - past_blocks)[..., None] * L_SEL + jnp.arange(L_SEL), 0, new.shape[1] - 1)
    k_new = new[b4[..., None], tpos, g_i]
    return jnp.where(in_past[..., None, None], k_past, k_new)


def window_prompt(qg, k, v, pos):
    B, T = qg.shape[:2]
    nb = T // WIN_Q_BLOCK
    span = WINDOW + WIN_Q_BLOCK
    kidx = jnp.arange(nb)[:, None] * WIN_Q_BLOCK + jnp.arange(span)[None, :]
    pad = ((0, 0), (WINDOW, 0), (0, 0), (0, 0))
    kb = jnp.pad(k, pad)[:, kidx]
    vb = jnp.pad(v, pad)[:, kidx]
    kpos = (kidx - WINDOW)[:, None, :]
    qpos = pos.reshape(nb, WIN_Q_BLOCK)[:, :, None]
    mask = (kpos >= 0) & (kpos <= qpos) & (kpos > qpos - WINDOW)
    out, _ = masked_attend(qg.reshape((B, nb, WIN_Q_BLOCK) + qg.shape[2:]), kb, vb, mask[None])
    return out.reshape(qg.shape)


def project(x, pos, norm_g, w_in):
    B, T, _ = x.shape
    z = jnp.einsum('btd,de->bte', rms_norm(x, norm_g), w_in)
    q, kc, vc, ks, vs, kw, vw, g_nsa, uv, g_merge = jnp.split(z, IN_SPLITS, axis=-1)
    q = q.reshape(B, T, N_HEADS, HEAD_DIM)
    kv = lambda t: t.reshape(B, T, N_KV_HEADS, HEAD_DIM)
    g_nsa = jax.nn.sigmoid(g_nsa).reshape(B, T, N_KV_HEADS, GROUP, 3)
    return (q, rope(q, pos), kv(kc), kv(vc), rope(kv(ks), pos), kv(vs), rope(kv(kw), pos), kv(vw),
            g_nsa, uv, g_merge)


def nsa_prompt(q, qr, kc, vc, ks, vs, kw, vw, pos, pk, bk, wk, pv, bv, wv):
    B, T = q.shape[:2]
    ck = compress(kc, pk, bk, wk)
    cv = compress(vc, pv, bv, wv)
    o_cmp, p_cmp = cmp_attend(q, ck, cv, pos)
    idx, valid = select_blocks(p_cmp, pos, T)
    qg = to_groups(qr)
    n_c = T // SEL_Q_CHUNK
    chunks = lambda a: jnp.moveaxis(a.reshape((B, n_c, SEL_Q_CHUNK) + a.shape[2:]), 1, 0)

    def one_chunk(args):
        qc, ic, vlc, pc = args
        return sel_attend(qc, gather_blocks(ks, ic), gather_blocks(vs, ic), ic, vlc, pc)

    o_slc = lax.map(one_chunk, (chunks(qg), chunks(idx), chunks(valid), pos.reshape(n_c, SEL_Q_CHUNK)))
    o_slc = jnp.moveaxis(o_slc, 0, 1).reshape(qg.shape)
    o_win = window_prompt(qg, kw, vw, pos)
    return o_cmp, o_slc, o_win


def nsa_sample(q, qr, kc, vc, ks, vs, kw, vw, pos, pk, bk, wk, pv, bv, wv,
               pool_k_cmp, pool_v_cmp, pool_k_slc, pool_v_slc, buf_k, buf_v, page_table):
    Bd, Td = q.shape[:2]
    past = page_table.shape[1] * pool_k_cmp.shape[1]
    past_rows = lambda pool: pool[page_table].reshape(Bd, past, N_KV_HEADS, HEAD_DIM)
    ck = compress(jnp.concatenate([past_rows(pool_k_cmp), kc], axis=1), pk, bk, wk)
    cv = compress(jnp.concatenate([past_rows(pool_v_cmp), vc], axis=1), pv, bv, wv)
    o_cmp, p_cmp = cmp_attend(q, ck, cv, pos)
    idx, valid = select_blocks(p_cmp, pos, past + Td)
    qg = to_groups(qr)
    o_slc = sel_attend(qg, gather_paged_blocks(pool_k_slc, ks, page_table, idx),
                       gather_paged_blocks(pool_v_slc, vs, page_table, idx), idx, valid, pos)
    wb = buf_k.shape[1]
    k_all = jnp.concatenate([buf_k, kw], axis=1)
    v_all = jnp.concatenate([buf_v, vw], axis=1)
    kpos = past - wb + jnp.arange(wb + Td)
    mask = (kpos[None, :] <= pos[:, None]) & (kpos[None, :] > pos[:, None] - WINDOW)
    o_win, _ = masked_attend(qg, k_all, v_all, mask[None])
    keep = min(WINDOW, past + Td)
    return o_cmp, o_slc, o_win, k_all[:, wb + Td - keep:], v_all[:, wb + Td - keep:]


def nsa_combine(o_cmp, o_slc, o_win, g):
    o = g[..., 0:1] * o_cmp + g[..., 1:2] * o_slc + g[..., 2:3] * o_win
    B, T = o.shape[:2]
    return o.reshape(B, T, Q_W)


def chunk_mix(v, w_s, b_s):
    B, T = v.shape[:2]
    n_c = -(-T // SG_CHUNK)
    vp = jnp.pad(v, ((0, 0), (0, n_c * SG_CHUNK - T), (0, 0), (0, 0)))
    vp = vp.reshape(B, n_c, SG_CHUNK, SG_GROUPS, SG_GROUP_DIM)
    w = w_s * jnp.tril(jnp.ones((SG_CHUNK, SG_CHUNK), w_s.dtype))
    mixed = jnp.einsum('gij,bcjgd->bcigd', w, vp) + b_s.T[None, None, :, :, None]
    return mixed.reshape(B, n_c * SG_CHUNK, SG_GROUPS, SG_GROUP_DIM)[:, :T]


def spatial_gating(uv, norm_g, w_s, b_s):
    u, v = jnp.split(jax.nn.gelu(uv), 2, axis=-1)
    v = rms_norm(v, norm_g)
    B, T, _ = v.shape
    mixed = chunk_mix(v.reshape(B, T, SG_GROUPS, SG_GROUP_DIM), w_s, b_s).reshape(B, T, SG_WIDTH)
    return u * mixed, v


def merge_and_ffn(x, o_nsa, o_sg, g_merge, w_pa, w_pb, w_out, norm2_g, w_up, conv_w, conv_b, w_down, conv_hist):
    g_a, g_b = jnp.split(jax.nn.sigmoid(g_merge), 2, axis=-1)
    m = g_a * (o_nsa @ w_pa) + g_b * (o_sg @ w_pb)
    h = x + m @ w_out
    up = rms_norm(h, norm2_g) @ w_up
    T = up.shape[1]
    full = jnp.concatenate([conv_hist.astype(up.dtype), up], axis=1)
    conv = conv_b + full[:, 0:T] * conv_w[0]
    for tap in range(1, CONV_W):
        conv = conv + full[:, tap:tap + T] * conv_w[tap]
    a, b = jnp.split(conv, 2, axis=-1)
    y = h + (jax.nn.silu(a) * b) @ w_down
    return y, full[:, T:]


def setup_inputs(seed: int = 0) -> dict:
    key = jax.random.key(seed)
    ks = jax.random.split(key, 40)
    f32 = jnp.float32
    nrm = lambda k, shape, scale: jax.random.normal(k, shape, f32) * scale
    n_pages = PAST_LEN // PAGE_SIZE
    used = DEC_BATCH * n_pages
    n_phys = used + max(1, used // 4)
    page_table = jax.random.permutation(ks[0], n_phys)[:used].reshape(DEC_BATCH, n_pages).astype(jnp.int32)
    pool = (DEPTH, n_phys, PAGE_SIZE, N_KV_HEADS, HEAD_DIM)
    wbuf = min(WINDOW, PAST_LEN)
    return {
        'x_prompt': nrm(ks[1], (BATCH, SEQ, D_MODEL), 1.0),
        'x_sample': nrm(ks[2], (DEC_BATCH, DEC_SEQ, D_MODEL), 1.0),
        'cache_k_cmp': nrm(ks[3], pool, 1.0),
        'cache_v_cmp': nrm(ks[4], pool, 1.0),
        'cache_k_slc': nrm(ks[5], pool, 1.0),
        'cache_v_slc': nrm(ks[6], pool, 1.0),
        'cache_k_win': nrm(ks[7], (DEPTH, DEC_BATCH, wbuf, N_KV_HEADS, HEAD_DIM), 1.0),
        'cache_v_win': nrm(ks[8], (DEPTH, DEC_BATCH, wbuf, N_KV_HEADS, HEAD_DIM), 1.0),
        'state_ffn_conv': nrm(ks[9], (DEPTH, DEC_BATCH, CONV_W - 1, 2 * D_FF), 1.0),
        'page_table': page_table,
        'norm1_g': 1.0 + nrm(ks[10], (DEPTH, D_MODEL), 0.02),
        'w_in': nrm(ks[11], (DEPTH, D_MODEL, IN_W), D_MODEL ** -0.5),
        'cmp_pool_k': nrm(ks[12], (DEPTH, L_CMP, HEAD_DIM), L_CMP ** -0.5),
        'cmp_bias_k': nrm(ks[13], (DEPTH, HEAD_DIM), 0.02),
        'cmp_w_k': nrm(ks[14], (DEPTH, HEAD_DIM, HEAD_DIM), HEAD_DIM ** -0.5),
        'cmp_pool_v': nrm(ks[15], (DEPTH, L_CMP, HEAD_DIM), L_CMP ** -0.5),
        'cmp_bias_v': nrm(ks[16], (DEPTH, HEAD_DIM), 0.02),
        'cmp_w_v': nrm(ks[17], (DEPTH, HEAD_DIM, HEAD_DIM), HEAD_DIM ** -0.5),
        'sg_norm_g': 1.0 + nrm(ks[18], (DEPTH, SG_WIDTH), 0.02),
        'sg_w': nrm(ks[19], (DEPTH, SG_GROUPS, SG_CHUNK, SG_CHUNK), SG_CHUNK ** -0.5),
        'sg_b': 1.0 + nrm(ks[20], (DEPTH, SG_GROUPS, SG_CHUNK), 0.02),
        'w_proj_a': nrm(ks[21], (DEPTH, Q_W, D_MODEL), Q_W ** -0.5),
        'w_proj_b': nrm(ks[22], (DEPTH, SG_WIDTH, D_MODEL), SG_WIDTH ** -0.5),
        'w_out': nrm(ks[23], (DEPTH, D_MODEL, D_MODEL), D_MODEL ** -0.5),
        'norm2_g': 1.0 + nrm(ks[24], (DEPTH, D_MODEL), 0.02),
        'w_up': nrm(ks[25], (DEPTH, D_MODEL, 2 * D_FF), D_MODEL ** -0.5),
        'conv_w': nrm(ks[26], (DEPTH, CONV_W, 2 * D_FF), CONV_W ** -0.5),
        'conv_b': nrm(ks[27], (DEPTH, 2 * D_FF), 0.02),
        'w_down': nrm(ks[28], (DEPTH, D_FF, D_MODEL), D_FF ** -0.5),
        'norm_f_g': 1.0 + nrm(ks[29], (D_MODEL,), 0.02),
    }


def reference(x_prompt, x_sample, cache_k_cmp, cache_v_cmp, cache_k_slc, cache_v_slc, cache_k_win, cache_v_win,
              state_ffn_conv, page_table, norm1_g, w_in, cmp_pool_k, cmp_bias_k, cmp_w_k, cmp_pool_v, cmp_bias_v,
              cmp_w_v, sg_norm_g, sg_w, sg_b, w_proj_a, w_proj_b, w_out, norm2_g, w_up, conv_w, conv_b, w_down,
              norm_f_g):
    B, T, _ = x_prompt.shape
    Bd, Td, _ = x_sample.shape
    past = page_table.shape[1] * cache_k_cmp.shape[2]
    pos_p = jnp.arange(T)
    pos_s = past + jnp.arange(Td)
    hp, hs = x_prompt, x_sample
    new_p = [[] for _ in range(7)]
    new_s = [[] for _ in range(8)]
    for l in range(DEPTH):
        cmp_args = (cmp_pool_k[l], cmp_bias_k[l], cmp_w_k[l], cmp_pool_v[l], cmp_bias_v[l], cmp_w_v[l])
        tail_args = (w_proj_a[l], w_proj_b[l], w_out[l], norm2_g[l], w_up[l], conv_w[l], conv_b[l], w_down[l])
        q, qr, kc, vc, ks, vs, kw, vw, g_nsa, uv, g_m = project(hp, pos_p, norm1_g[l], w_in[l])
        o_cmp, o_slc, o_win = nsa_prompt(q, qr, kc, vc, ks, vs, kw, vw, pos_p, *cmp_args)
        o_sg, _ = spatial_gating(uv, sg_norm_g[l], sg_w[l], sg_b[l])
        hist0 = jnp.zeros((B, CONV_W - 1, 2 * D_FF), hp.dtype)
        hp, conv_p = merge_and_ffn(hp, nsa_combine(o_cmp, o_slc, o_win, g_nsa), o_sg, g_m, *tail_args, hist0)
        keep_p = min(WINDOW, T)
        for lst, val in zip(new_p, (kc, vc, ks, vs, kw[:, T - keep_p:], vw[:, T - keep_p:], conv_p)):
            lst.append(val)
        q, qr, kc, vc, ks, vs, kw, vw, g_nsa, uv, g_m = project(hs, pos_s, norm1_g[l], w_in[l])
        o_cmp, o_slc, o_win, win_k, win_v = nsa_sample(
            q, qr, kc, vc, ks, vs, kw, vw, pos_s, *cmp_args,
            cache_k_cmp[l], cache_v_cmp[l], cache_k_slc[l], cache_v_slc[l], cache_k_win[l], cache_v_win[l],
            page_table)
        o_sg, sg_v = spatial_gating(uv, sg_norm_g[l], sg_w[l], sg_b[l])
        hs, conv_s = merge_and_ffn(hs, nsa_combine(o_cmp, o_slc, o_win, g_nsa), o_sg, g_m, *tail_args,
                                   state_ffn_conv[l])
        for lst, val in zip(new_s, (kc, vc, ks, vs, win_k, win_v, sg_v, conv_s)):
            lst.append(val)
    y_prompt = rms_norm(hp, norm_f_g)
    y_sample = rms_norm(hs, norm_f_g)
    return (y_prompt, y_sample,
            jnp.stack(new_p[0]), jnp.stack(new_p[1]), jnp.stack(new_p[2]), jnp.stack(new_p[3]),
            jnp.stack(new_p[4]), jnp.stack(new_p[5]), jnp.stack(new_p[6]),
            jnp.stack(new_s[0]), jnp.stack(new_s[1]), jnp.stack(new_s[2]), jnp.stack(new_s[3]),
            jnp.stack(new_s[4]), jnp.stack(new_s[5]), jnp.stack(new_s[6]), jnp.stack(new_s[7]))
```

```python
import functools

import jax
import jax.numpy as jnp
from jax import lax
from jax.experimental import pallas as pl
from jax.experimental.pallas import tpu as pltpu

f32 = jnp.float32
bf16 = jnp.bfloat16
HIGHEST = lax.Precision.HIGHEST

N_HEADS = 16
N_KV_HEADS = 4
GROUP = N_HEADS // N_KV_HEADS
HEAD_DIM = 128
ROT_DIM = HEAD_DIM // 4
ROT_HALF = ROT_DIM // 2
ROPE_THETA = 500000.0
L_CMP = 32
STRIDE = 16
L_SEL = 64
N_SEL = 16
WINDOW = 512
FORCE_BONUS = 1000.0
SCALE = HEAD_DIM ** -0.5
NEG_INF = -1e30
SG_GROUPS = 16
SG_CHUNK = 128
CONV_W = 3
EPS = 1e-6

GROUP_W = GROUP * HEAD_DIM
Q_W = N_HEADS * HEAD_DIM
KV_W = N_KV_HEADS * HEAD_DIM
NSA_GATE_W = 3 * N_HEADS
QKV_W = Q_W + 6 * KV_W
Z1_W = QKV_W + GROUP_W
COL_KC, COL_VC, COL_KS, COL_VS, COL_KW, COL_VW, COL_GATE = (Q_W // GROUP_W + i for i in range(7))

VMEM_LIMIT_BYTES = 52 * 1024 * 1024


def _params(*semantics):
    return pltpu.CompilerParams(dimension_semantics=semantics, vmem_limit_bytes=VMEM_LIMIT_BYTES)


def _tile(n, candidates):
    for c in candidates:
        if n % c == 0:
            return c
    return n


def _sigmoid(x):
    return 1.0 / (1.0 + jnp.exp(-x))


def _silu(x):
    return x * _sigmoid(x)


def _gelu_tanh(x):
    return x * (0.5 * (1.0 + jnp.tanh(0.7978845608028654 * (x + 0.044715 * (x * x * x)))))


def _dot(a, b, precision=None):
    return jnp.dot(a, b, preferred_element_type=f32, precision=precision)


def _dot_nt(a, b, precision=None):
    return lax.dot_general(a, b, (((1,), (1,)), ((), ())), precision=precision, preferred_element_type=f32)


def _rope(x, c, sa, sb):
    lanes = x.shape[-1]
    return x * c + pltpu.roll(x, lanes - ROT_HALF, 1) * sa + pltpu.roll(x, ROT_HALF, 1) * sb


def _rope_tables(pos, reps):
    n = pos.shape[0]
    inv_freq = jnp.power(jnp.float32(ROPE_THETA), -jnp.arange(ROT_HALF, dtype=f32) / ROT_HALF)
    ang = pos.astype(f32)[:, None] * inv_freq[None, :]
    cos, sin = jnp.cos(ang), jnp.sin(ang)
    zeros = lambda w: jnp.zeros((n, w), f32)
    c = jnp.concatenate([cos, cos, jnp.ones((n, HEAD_DIM - ROT_DIM), f32)], axis=1)
    sa = jnp.concatenate([-sin, zeros(HEAD_DIM - ROT_HALF)], axis=1)
    sb = jnp.concatenate([zeros(ROT_HALF), sin, zeros(HEAD_DIM - ROT_DIM)], axis=1)
    return tuple(jnp.tile(t, (1, reps)) for t in (c, sa, sb))


def _rmsnorm_kernel(x_ref, g_ref, o_ref):
    x = x_ref[...].astype(f32)
    y = x * lax.rsqrt(jnp.mean(x * x, axis=-1, keepdims=True) + EPS)
    o_ref[...] = (y * g_ref[...]).astype(o_ref.dtype)


def _rmsnorm(x, g, out_dtype, name):
    m, d = x.shape
    tm = _tile(m, (256, 128, 64, 32, 16, 8))
    return pl.pallas_call(
        _rmsnorm_kernel,
        grid=(m // tm,),
        in_specs=[pl.BlockSpec((tm, d), lambda i: (i, 0)), pl.BlockSpec((1, d), lambda i: (0, 0))],
        out_specs=pl.BlockSpec((tm, d), lambda i: (i, 0)),
        out_shape=jax.ShapeDtypeStruct((m, d), out_dtype),
        compiler_params=_params("parallel"),
        name=name,
    )(x, g.reshape(1, d).astype(f32))


def _mm_kernel(a_ref, w_ref, *rest, epilogue):
    o_ref = rest[-1]
    acc = _dot(a_ref[...], w_ref[...])
    o_ref[...] = epilogue(acc, *[r[...] for r in rest[:-1]]).astype(o_ref.dtype)


def _matmul(a, w, out_dtype, name, epilogue=lambda acc: acc, extras=(), tn_candidates=(512, 256, 128)):
    m, k = a.shape
    n = w.shape[1]
    tm = _tile(m, (1024, 512, 256, 128, 64, 32, 16))
    tn = _tile(n, tn_candidates)
    in_specs = [pl.BlockSpec((tm, k), lambda i, j: (i, 0)), pl.BlockSpec((k, tn), lambda i, j: (0, j))]
    in_specs += [pl.BlockSpec((tm, tn), lambda i, j: (i, j)) for _ in extras]
    return pl.pallas_call(
        functools.partial(_mm_kernel, epilogue=epilogue),
        grid=(m // tm, n // tn),
        in_specs=in_specs,
        out_specs=pl.BlockSpec((tm, tn), lambda i, j: (i, j)),
        out_shape=jax.ShapeDtypeStruct((m, n), out_dtype),
        compiler_params=_params("parallel", "arbitrary"),
        name=name,
    )(a, w, *extras)


def _merge_kernel(a_ref, wa_ref, b_ref, wb_ref, ga_ref, gb_ref, o_ref):
    pa = _dot(a_ref[...], wa_ref[...])
    pb = _dot(b_ref[...], wb_ref[...])
    o_ref[...] = (ga_ref[...].astype(f32) * pa + gb_ref[...].astype(f32) * pb).astype(o_ref.dtype)


def _merge(o_nsa, w_pa, o_sg, w_pb, gates, name):
    m, ka = o_nsa.shape
    kb = o_sg.shape[1]
    d = w_pa.shape[1]
    tm = _tile(m, (1024, 512, 256, 128, 64, 32, 16))
    tn = _tile(d, (512, 256, 128))
    nb = d // tn
    return pl.pallas_call(
        _merge_kernel,
        grid=(m // tm, nb),
        in_specs=[
            pl.BlockSpec((tm, ka), lambda i, j: (i, 0)),
            pl.BlockSpec((ka, tn), lambda i, j: (0, j)),
            pl.BlockSpec((tm, kb), lambda i, j: (i, 0)),
            pl.BlockSpec((kb, tn), lambda i, j: (0, j)),
            pl.BlockSpec((tm, tn), lambda i, j: (i, j)),
            pl.BlockSpec((tm, tn), lambda i, j: (i, j + nb)),
        ],
        out_specs=pl.BlockSpec((tm, tn), lambda i, j: (i, j)),
        out_shape=jax.ShapeDtypeStruct((m, d), bf16),
        compiler_params=_params("parallel", "arbitrary"),
        name=name,
    )(o_nsa, w_pa, o_sg, w_pb, gates, gates)


def _mm_ksplit_kernel(a_ref, w_ref, r_ref, o_ref, acc_ref, *, nk):
    kk = pl.program_id(2)

    @pl.when(kk == 0)
    def _():
        acc_ref[...] = jnp.zeros_like(acc_ref)

    acc_ref[...] += _dot(a_ref[...], w_ref[...])

    @pl.when(kk == nk - 1)
    def _():
        o_ref[...] = r_ref[...] + acc_ref[...]


def _matmul_residual_ksplit(a, w, resid, name):
    m, k = a.shape
    n = w.shape[1]
    tm = _tile(m, (512, 256, 128, 64, 32, 16))
    tn = _tile(n, (512, 256, 128))
    tk = k
    for parts in (1, 2, 4, 8):
        if k % parts == 0 and (k // parts) % 128 == 0 and k // parts <= 6144:
            tk = k // parts
            break
    nk = k // tk
    return pl.pallas_call(
        functools.partial(_mm_ksplit_kernel, nk=nk),
        grid=(m // tm, n // tn, nk),
        in_specs=[
            pl.BlockSpec((tm, tk), lambda i, j, kk: (i, kk)),
            pl.BlockSpec((tk, tn), lambda i, j, kk: (kk, j)),
            pl.BlockSpec((tm, tn), lambda i, j, kk: (i, j)),
        ],
        out_specs=pl.BlockSpec((tm, tn), lambda i, j, kk: (i, j)),
        out_shape=jax.ShapeDtypeStruct((m, n), f32),
        scratch_shapes=[pltpu.VMEM((tm, tn), f32)],
        compiler_params=_params("parallel", "arbitrary", "arbitrary"),
        name=name,
    )(a, w, resid)


def _prep_kernel(kc, vc, ks, vs, kw, vw, c_ref, sa_ref, sb_ref, okc, ovc, oks, ovs, okw, ovw):
    okc[...] = kc[...]
    ovc[...] = vc[...]
    ovs[...] = vs[...]
    ovw[...] = vw[...]
    c, sa, sb = c_ref[...], sa_ref[...], sb_ref[...]
    oks[...] = _rope(ks[...], c, sa, sb)
    okw[...] = _rope(kw[...], c, sa, sb)


def _prep_kv(z1, tables, rows_per_seq, name):
    m = z1.shape[0]
    tr = _tile(rows_per_seq, (256, 128, 64, 32, 16, 8))
    n_tab = tables[0].shape[0] // tr
    col = lambda cb: pl.BlockSpec((tr, KV_W), lambda i, cb=cb: (i, cb))
    tab = pl.BlockSpec((tr, KV_W), lambda i: (i % n_tab, 0))
    out = pl.BlockSpec((tr, KV_W), lambda i: (i, 0))
    return pl.pallas_call(
        _prep_kernel,
        grid=(m // tr,),
        in_specs=[col(COL_KC), col(COL_VC), col(COL_KS), col(COL_VS), col(COL_KW), col(COL_VW), tab, tab, tab],
        out_specs=[out] * 6,
        out_shape=[jax.ShapeDtypeStruct((m, KV_W), f32)] * 6,
        compiler_params=_params("parallel"),
        name=name,
    )(z1, z1, z1, z1, z1, z1, *tables)


def _pool_halves(load_rows, pw_ref, nsub):
    first = None
    second = None
    for j in range(STRIDE):
        rows = load_rows(j, nsub)
        fa = rows * pw_ref[j:j + 1, :]
        sa = rows * pw_ref[STRIDE + j:STRIDE + j + 1, :]
        first = fa if first is None else first + fa
        second = sa if second is None else second + sa
    return first, second


def _compress_prompt_kernel(k_ref, v_ref, pwk_ref, bk_ref, pwv_ref, bv_ref, ak_ref, av_ref, *, nsub):
    for x_ref, pw_ref, b_ref, o_ref in ((k_ref, pwk_ref, bk_ref, ak_ref), (v_ref, pwv_ref, bv_ref, av_ref)):
        first, second = _pool_halves(lambda j, n, x_ref=x_ref: x_ref[pl.ds(j, n, stride=STRIDE), :], pw_ref, nsub)
        pooled = pltpu.roll(first, 1, 0) + second + b_ref[...]
        o_ref[0] = _silu(pooled)


def _compress_prompt(kc, vc, pwk, bk, pwv, bv, batch, seq, name):
    nsub = seq // STRIDE
    xspec = pl.BlockSpec((seq, HEAD_DIM), lambda b, g: (b, g))
    pwspec = pl.BlockSpec((L_CMP, HEAD_DIM), lambda b, g: (0, 0))
    bspec = pl.BlockSpec((1, HEAD_DIM), lambda b, g: (0, 0))
    ospec = pl.BlockSpec((1, nsub, HEAD_DIM), lambda b, g: (b, 0, g))
    return pl.pallas_call(
        functools.partial(_compress_prompt_kernel, nsub=nsub),
        grid=(batch, N_KV_HEADS),
        in_specs=[xspec, xspec, pwspec, bspec, pwspec, bspec],
        out_specs=[ospec, ospec],
        out_shape=[jax.ShapeDtypeStruct((batch, nsub, KV_W), f32)] * 2,
        compiler_params=_params("parallel", "parallel"),
        name=name,
    )(kc, vc, pwk, bk.reshape(1, HEAD_DIM), pwv, bv.reshape(1, HEAD_DIM))


def _compress_paged_kernel(pt_ref, k_ref, v_ref, pwk_ref, bk_ref, pwv_ref, bv_ref, ak_ref, av_ref,
                           carry_ref, *, nsub):
    del pt_ref

    @pl.when(pl.program_id(1) == 0)
    def _():
        carry_ref[...] = jnp.zeros_like(carry_ref)

    for idx, (x_ref, pw_ref, b_ref, o_ref) in enumerate(
            ((k_ref, pwk_ref, bk_ref, ak_ref), (v_ref, pwv_ref, bv_ref, av_ref))):
        first = None
        second = None
        for j in range(STRIDE):
            rows = x_ref[pl.ds(j, nsub, stride=STRIDE)]
            fa = rows * pw_ref[j]
            sa = rows * pw_ref[STRIDE + j]
            first = fa if first is None else first + fa
            second = sa if second is None else second + sa
        shifted = jnp.concatenate([carry_ref[idx][None], first[:nsub - 1]], axis=0)
        carry_ref[idx] = first[nsub - 1]
        o_ref[0] = _silu(shifted + second + b_ref[...])


def _compress_paged(pool_k, pool_v, page_table, pwk, bk, pwv, bv, name):
    n_batch, n_pages = page_table.shape
    page = pool_k.shape[1]
    nsub = page // STRIDE
    pool_spec = pl.BlockSpec((None, page, N_KV_HEADS, HEAD_DIM), lambda b, p, pt: (pt[b * n_pages + p], 0, 0, 0))
    pw_spec = pl.BlockSpec((L_CMP, 1, HEAD_DIM), lambda b, p, pt: (0, 0, 0))
    b_spec = pl.BlockSpec((1, 1, HEAD_DIM), lambda b, p, pt: (0, 0, 0))
    o_spec = pl.BlockSpec((1, nsub, N_KV_HEADS, HEAD_DIM), lambda b, p, pt: (b, p, 0, 0))
    out_shape = jax.ShapeDtypeStruct((n_batch, n_pages * nsub, N_KV_HEADS, HEAD_DIM), f32)
    return pl.pallas_call(
        functools.partial(_compress_paged_kernel, nsub=nsub),
        grid_spec=pltpu.PrefetchScalarGridSpec(
            num_scalar_prefetch=1,
            grid=(n_batch, n_pages),
            in_specs=[pool_spec, pool_spec, pw_spec, b_spec, pw_spec, b_spec],
            out_specs=[o_spec, o_spec],
            scratch_shapes=[pltpu.VMEM((2, N_KV_HEADS, HEAD_DIM), f32)],
        ),
        out_shape=[out_shape, out_shape],
        compiler_params=_params("parallel", "arbitrary"),
        name=name,
    )(page_table.reshape(-1), pool_k, pool_v,
      pwk.reshape(L_CMP, 1, HEAD_DIM), bk.reshape(1, 1, HEAD_DIM),
      pwv.reshape(L_CMP, 1, HEAD_DIM), bv.reshape(1, 1, HEAD_DIM))


def _cmp_select_kernel(q_ref, ak_ref, av_ref, wk_ref, wv_ref, ocmp_ref, sel_ref, score_ref, *,
                       tq, nrows, nblk, pos_base, pos_stride, emit_indices):
    qi = pl.program_id(2)
    pos0 = pos_base + qi * tq * pos_stride
    pos_col = pos0 + lax.broadcasted_iota(jnp.int32, (tq, 1), 0) * pos_stride
    pos_row = pos0 + lax.broadcasted_iota(jnp.int32, (1, tq), 1) * pos_stride

    ck = _dot(ak_ref[0], wk_ref[...], precision=HIGHEST)
    cv = _dot(av_ref[0].astype(bf16), wv_ref[...].astype(bf16)).astype(bf16)

    rown = lax.broadcasted_iota(jnp.int32, (1, nrows), 1)
    visible = (rown >= 1) & (rown * STRIDE + (L_CMP - STRIDE - 1) <= pos_col)

    pg = jnp.zeros((tq, nrows), f32)
    for r in range(GROUP):
        sl = slice(r * HEAD_DIM, (r + 1) * HEAD_DIM)
        s = _dot_nt(q_ref[:, sl], ck, precision=HIGHEST) * SCALE
        s = jnp.where(visible, s, NEG_INF)
        e = jnp.where(visible, jnp.exp(s - jnp.max(s, axis=-1, keepdims=True)), 0.0)
        den = jnp.sum(e, axis=-1, keepdims=True)
        p = jnp.where(den > 0.0, e / jnp.where(den > 0.0, den, 1.0), 0.0)
        ocmp_ref[:, sl] = _dot(p.astype(bf16), cv)
        pg = pg + p

    blk = lax.broadcasted_iota(jnp.int32, (nblk, 1), 0)
    c_start = (rown - 1) * STRIDE
    overlap = (rown >= 1) & (c_start < blk * L_SEL + L_SEL) & (c_start + L_CMP > blk * L_SEL)
    imp_t = _dot_nt(overlap.astype(f32), pg, precision=HIGHEST)
    qb = pos_row // L_SEL
    causal = blk <= qb
    forced = (blk == 0) | (blk == qb) | (blk == qb - 1)
    score = jnp.where(causal, imp_t + FORCE_BONUS * forced.astype(f32), -jnp.inf)
    score_ref[...] = score

    def count_better(i, cnt):
        other = score_ref[pl.ds(i, 1), :]
        better = (other > score) | ((other == score) & (i < blk))
        return cnt + better.astype(f32)

    rank = lax.fori_loop(0, nblk, count_better, jnp.zeros((nblk, tq), f32))
    if emit_indices:
        for slot in range(N_SEL):
            hit = causal & (rank == float(slot))
            sel_ref[0, 0, slot:slot + 1, :] = jnp.sum(jnp.where(hit, blk, 0), axis=0, keepdims=True)
            sel_ref[0, 0, N_SEL + slot:N_SEL + slot + 1, :] = jnp.max(hit.astype(jnp.int32), axis=0, keepdims=True)
    else:
        chosen = jnp.where(causal & (rank < float(N_SEL)), 1.0, 0.0).astype(bf16)
        eye = (lax.broadcasted_iota(jnp.int32, (tq, tq), 0)
               == lax.broadcasted_iota(jnp.int32, (tq, tq), 1)).astype(bf16)
        sel_ref[0, 0] = _dot_nt(eye, chosen)


def _cmp_select(q, ak, av, wk, wv, *, n_batch, n_q, tq, nblk, pos_base, pos_stride, emit_indices, name):
    nrows = ak.shape[1]
    rows = n_batch * n_q * tq
    q_spec = pl.BlockSpec((tq, GROUP_W), lambda b, g, i: (b * n_q + i, g))
    a_spec = pl.BlockSpec((1, nrows, HEAD_DIM), lambda b, g, i: (b, 0, g))
    w_spec = pl.BlockSpec((HEAD_DIM, HEAD_DIM), lambda b, g, i: (0, 0))
    if emit_indices:
        sel_shape = jax.ShapeDtypeStruct((n_batch, N_KV_HEADS, 2 * N_SEL, tq), jnp.int32)
        sel_spec = pl.BlockSpec((1, 1, 2 * N_SEL, tq), lambda b, g, i: (b, g, 0, 0))
    else:
        sel_shape = jax.ShapeDtypeStruct((n_batch, N_KV_HEADS, n_q * tq, nblk), f32)
        sel_spec = pl.BlockSpec((1, 1, tq, nblk), lambda b, g, i: (b, g, i, 0))
    return pl.pallas_call(
        functools.partial(_cmp_select_kernel, tq=tq, nrows=nrows, nblk=nblk, pos_base=pos_base,
                          pos_stride=pos_stride, emit_indices=emit_indices),
        grid=(n_batch, N_KV_HEADS, n_q),
        in_specs=[q_spec, a_spec, a_spec, w_spec, w_spec],
        out_specs=[pl.BlockSpec((tq, GROUP_W), lambda b, g, i: (b * n_q + i, g)), sel_spec],
        out_shape=[jax.ShapeDtypeStruct((rows, Q_W), f32), sel_shape],
        scratch_shapes=[pltpu.VMEM((nblk, tq), f32)],
        compiler_params=_params("parallel", "parallel", "arbitrary"),
        name=name,
    )(q, ak, av, wk, wv)


def _softmax_pv(s, v):
    e = jnp.exp(s - jnp.max(s, axis=-1, keepdims=True))
    den = jnp.sum(e, axis=-1, keepdims=True)
    return _dot(e.astype(bf16), v) / den


def _nsa_prompt_kernel(q_ref, c_ref, sa_ref, sb_ref, ks_ref, vs_ref, kw_ref, vw_ref, sel_ref, ocmp_ref,
                       gate_ref, o_ref, *, tq, seq, nblk, span):
    qi = pl.program_id(2)
    t0 = qi * tq
    qr = _rope(q_ref[...], c_ref[...], sa_ref[...], sb_ref[...]).astype(bf16)
    qa = jnp.concatenate([qr[:, r * HEAD_DIM:(r + 1) * HEAD_DIM] for r in range(GROUP)], axis=0)
    tpos = t0 + lax.broadcasted_iota(jnp.int32, (tq, 1), 0)

    kpos = lax.broadcasted_iota(jnp.int32, (1, seq), 1)
    expand = (kpos // L_SEL == lax.broadcasted_iota(jnp.int32, (nblk, 1), 0)).astype(bf16)
    chosen = _dot(sel_ref[0, 0].astype(bf16), expand)
    mask = (chosen > 0.5) & (kpos <= tpos)
    s = _dot_nt(qa, ks_ref[...].astype(bf16)) * SCALE
    s = jnp.where(mask[None], s.reshape(GROUP, tq, seq), NEG_INF).reshape(GROUP * tq, seq)
    o_slc = _softmax_pv(s, vs_ref[...].astype(bf16))

    start = pl.multiple_of(jnp.maximum(t0 - WINDOW, 0), tq)
    wpos = start + lax.broadcasted_iota(jnp.int32, (1, span), 1)
    wmask = (wpos <= tpos) & (wpos > tpos - WINDOW)
    sw = _dot_nt(qa, kw_ref[pl.ds(start, span), :].astype(bf16)) * SCALE
    sw = jnp.where(wmask[None], sw.reshape(GROUP, tq, span), NEG_INF).reshape(GROUP * tq, span)
    o_win = _softmax_pv(sw, vw_ref[pl.ds(start, span), :].astype(bf16))

    gates = _sigmoid(gate_ref[...])
    for r in range(GROUP):
        sl = slice(r * HEAD_DIM, (r + 1) * HEAD_DIM)
        rows = slice(r * tq, (r + 1) * tq)
        o = (gates[:, 3 * r:3 * r + 1] * ocmp_ref[:, sl] + gates[:, 3 * r + 1:3 * r + 2] * o_slc[rows]
             + gates[:, 3 * r + 2:3 * r + 3] * o_win[rows])
        o_ref[:, sl] = o.astype(o_ref.dtype)


def _nsa_prompt(z1, tables, ks, vs, kw, vw, sel, ocmp, gates, *, batch, seq, nblk, name):
    tq = _tile(seq, (256, 128))
    n_q = seq // tq
    span = min(WINDOW + tq, seq)
    row_blk = lambda w: pl.BlockSpec((tq, w), lambda b, g, i: (b * n_q + i, g))
    tab = pl.BlockSpec((tq, GROUP_W), lambda b, g, i: (i, 0))
    kv = pl.BlockSpec((seq, HEAD_DIM), lambda b, g, i: (b, g))
    return pl.pallas_call(
        functools.partial(_nsa_prompt_kernel, tq=tq, seq=seq, nblk=nblk, span=span),
        grid=(batch, N_KV_HEADS, n_q),
        in_specs=[row_blk(GROUP_W), tab, tab, tab, kv, kv, kv, kv,
                  pl.BlockSpec((1, 1, tq, nblk), lambda b, g, i: (b, g, i, 0)),
                  row_blk(GROUP_W),
                  pl.BlockSpec((None, tq, 3 * GROUP), lambda b, g, i: (g, b * n_q + i, 0))],
        out_specs=row_blk(GROUP_W),
        out_shape=jax.ShapeDtypeStruct((batch * seq, Q_W), bf16),
        compiler_params=_params("parallel", "parallel", "arbitrary"),
        name=name,
    )(z1, *tables, ks, vs, kw, vw, sel, ocmp, gates)


def _nsa_sample_kernel(idx_ref, pt_ref, q_ref, c_ref, sa_ref, sb_ref, ksn_ref, vsn_ref, kwin_ref, vwin_ref,
                       ocmp_ref, gate_ref, poolk_ref, poolv_ref, o_ref, kbuf, vbuf, sems, *,
                       rows, n_pages, page, pos):
    b = pl.program_id(0)
    per_page = page // L_SEL
    past_blocks = n_pages * per_page
    n_keys = N_SEL * L_SEL

    def block_copies(g, slot):
        blk = idx_ref[(b * N_KV_HEADS + g) * 2 * N_SEL + slot]
        jp = jnp.minimum(blk, past_blocks - 1)
        phys = pt_ref[b * n_pages + jp // per_page]
        off = pl.multiple_of((jp % per_page) * L_SEL, L_SEL)
        dst = pl.ds(slot * L_SEL, L_SEL)
        return (pltpu.make_async_copy(poolk_ref.at[phys, pl.ds(off, L_SEL), g], kbuf.at[g, dst], sems.at[0]),
                pltpu.make_async_copy(poolv_ref.at[phys, pl.ds(off, L_SEL), g], vbuf.at[g, dst], sems.at[1]))

    for g in range(N_KV_HEADS):
        for slot in range(N_SEL):
            for cp in block_copies(g, slot):
                cp.start()
    for g in range(N_KV_HEADS):
        for slot in range(N_SEL):
            for cp in block_copies(g, slot):
                cp.wait()

    q = q_ref[...]
    qr = _rope(q, c_ref[...], sa_ref[...], sb_ref[...]).astype(bf16)
    gates = _sigmoid(gate_ref[0])
    lane = lax.broadcasted_iota(jnp.int32, (1, n_keys), 1)
    lane_slot = lane // L_SEL
    for g in range(N_KV_HEADS):
        blk_vec = jnp.zeros((1, n_keys), jnp.int32)
        ok_vec = jnp.zeros((1, n_keys), jnp.int32)
        for slot in range(N_SEL):
            base = (b * N_KV_HEADS + g) * 2 * N_SEL
            blk_vec = jnp.where(lane_slot == slot, idx_ref[base + slot], blk_vec)
            ok_vec = jnp.where(lane_slot == slot, idx_ref[base + N_SEL + slot], ok_vec)
        in_past = blk_vec < past_blocks
        kpos = blk_vec * L_SEL + lane % L_SEL
        mask = (ok_vec > 0) & (kpos <= pos)
        kg = kbuf[g].astype(bf16)
        vg = vbuf[g].astype(bf16)
        gsl = slice(g * HEAD_DIM, (g + 1) * HEAD_DIM)
        k_new = ksn_ref[0][:, gsl].astype(bf16).astype(f32)
        v_new = vsn_ref[0][:, gsl].astype(bf16).astype(f32)
        kw_g = kwin_ref[0][:, gsl].astype(bf16)
        vw_g = vwin_ref[0][:, gsl].astype(bf16)
        for r in range(GROUP):
            h = g * GROUP + r
            sl = slice(h * HEAD_DIM, (h + 1) * HEAD_DIM)
            qh = qr[:, sl]
            s_new = jnp.sum(qh.astype(f32) * k_new, axis=-1, keepdims=True)
            s = jnp.where(in_past, _dot_nt(qh, kg), s_new) * SCALE
            s = jnp.where(mask, s, NEG_INF)
            e = jnp.exp(s - jnp.max(s, axis=-1, keepdims=True))
            den = jnp.sum(e, axis=-1, keepdims=True)
            e_new = jnp.sum(jnp.where(in_past, 0.0, e), axis=-1, keepdims=True)
            e_past = jnp.where(in_past, e, 0.0).astype(bf16)
            o_slc = (_dot(e_past, vg) + e_new.astype(bf16).astype(f32) * v_new) / den
            o_win = _softmax_pv(_dot_nt(qh, kw_g) * SCALE, vw_g)
            o_ref[:, sl] = (gates[:, 3 * h:3 * h + 1] * ocmp_ref[:, sl] + gates[:, 3 * h + 1:3 * h + 2] * o_slc
                            + gates[:, 3 * h + 2:3 * h + 3] * o_win)


def _nsa_sample(idx, page_table, q8, tables, ks_new, vs_new, kwin, vwin, ocmp, gates, pool_k, pool_v, *,
                rows, pos, name):
    n_batch, n_pages = page_table.shape
    page = pool_k.shape[1]
    window = kwin.shape[1]
    full = lambda shape: pl.BlockSpec(shape, lambda b, *_: (0,) * len(shape))
    per_b2 = lambda w: pl.BlockSpec((rows, w), lambda b, *_: (b, 0))
    per_b3 = lambda s, w: pl.BlockSpec((1, s, w), lambda b, *_: (b, 0, 0))
    return pl.pallas_call(
        functools.partial(_nsa_sample_kernel, rows=rows, n_pages=n_pages, page=page, pos=pos),
        grid_spec=pltpu.PrefetchScalarGridSpec(
            num_scalar_prefetch=2,
            grid=(n_batch,),
            in_specs=[per_b2(Q_W), full((rows, Q_W)), full((rows, Q_W)), full((rows, Q_W)),
                      per_b3(1, KV_W), per_b3(1, KV_W), per_b3(window, KV_W), per_b3(window, KV_W),
                      per_b2(Q_W), per_b3(1, 128),
                      pl.BlockSpec(memory_space=pl.ANY), pl.BlockSpec(memory_space=pl.ANY)],
            out_specs=per_b2(Q_W),
            scratch_shapes=[pltpu.VMEM((N_KV_HEADS, N_SEL * L_SEL, HEAD_DIM), f32),
                            pltpu.VMEM((N_KV_HEADS, N_SEL * L_SEL, HEAD_DIM), f32),
                            pltpu.SemaphoreType.DMA((2,))],
        ),
        out_shape=jax.ShapeDtypeStruct((n_batch * rows, Q_W), f32),
        compiler_params=_params("arbitrary"),
        name=name,
    )(idx, page_table.reshape(-1), q8, *tables, ks_new, vs_new, kwin, vwin, ocmp, gates, pool_k, pool_v)


def _sgu_prompt_kernel(u_ref, v_ref, g_ref, w_ref, bt_ref, o_ref):
    v = v_ref[...].astype(f32)
    vn = (v * lax.rsqrt(jnp.mean(v * v, axis=-1, keepdims=True) + EPS) * g_ref[...]).astype(bf16)
    tril = (lax.broadcasted_iota(jnp.int32, (SG_CHUNK, SG_CHUNK), 0)
            >= lax.broadcasted_iota(jnp.int32, (SG_CHUNK, SG_CHUNK), 1))
    bt = bt_ref[...]
    group_dim = v.shape[-1] // SG_GROUPS
    for gi in range(SG_GROUPS):
        sl = slice(gi * group_dim, (gi + 1) * group_dim)
        w = jnp.where(tril, w_ref[gi], 0.0).astype(bf16)
        mixed = _dot(w, vn[:, sl]) + bt[:, gi:gi + 1]
        o_ref[:, sl] = (u_ref[:, sl].astype(f32) * mixed).astype(o_ref.dtype)


def _sgu_prompt(uv, norm_g, w_s, b_s, name):
    m, two_w = uv.shape
    width = two_w // 2
    return pl.pallas_call(
        _sgu_prompt_kernel,
        grid=(m // SG_CHUNK,),
        in_specs=[pl.BlockSpec((SG_CHUNK, width), lambda i: (i, 0)),
                  pl.BlockSpec((SG_CHUNK, width), lambda i: (i, 1)),
                  pl.BlockSpec((1, width), lambda i: (0, 0)),
                  pl.BlockSpec((SG_GROUPS, SG_CHUNK, SG_CHUNK), lambda i: (0, 0, 0)),
                  pl.BlockSpec((SG_CHUNK, SG_GROUPS), lambda i: (0, 0))],
        out_specs=pl.BlockSpec((SG_CHUNK, width), lambda i: (i, 0)),
        out_shape=jax.ShapeDtypeStruct((m, width), bf16),
        compiler_params=_params("parallel"),
        name=name,
    )(uv, uv, norm_g.reshape(1, width), w_s, b_s.T)


def _sgu_sample_kernel(u_ref, v_ref, g_ref, w0_ref, b0_ref, o_ref, vn_ref):
    v = v_ref[...]
    vn = v * lax.rsqrt(jnp.mean(v * v, axis=-1, keepdims=True) + EPS) * g_ref[...]
    vn_ref[...] = vn
    o_ref[...] = (u_ref[...] * (vn * w0_ref[...] + b0_ref[...])).astype(o_ref.dtype)


def _sgu_sample(uv, norm_g, w_s, b_s, name):
    m, two_w = uv.shape
    width = two_w // 2
    group_dim = width // SG_GROUPS
    w0 = jnp.repeat(w_s[:, 0, 0], group_dim).reshape(1, width)
    b0 = jnp.repeat(b_s[:, 0], group_dim).reshape(1, width)
    row = pl.BlockSpec((1, width), lambda i: (0, 0))
    return pl.pallas_call(
        _sgu_sample_kernel,
        grid=(1,),
        in_specs=[pl.BlockSpec((m, width), lambda i: (0, 0)), pl.BlockSpec((m, width), lambda i: (0, 1)),
                  row, row, row],
        out_specs=[pl.BlockSpec((m, width), lambda i: (0, 0))] * 2,
        out_shape=[jax.ShapeDtypeStruct((m, width), bf16), jax.ShapeDtypeStruct((m, width), f32)],
        compiler_params=_params("arbitrary"),
        name=name,
    )(uv, uv, norm_g.reshape(1, width), w0, b0)


CONV_HALO = 8


def _ffn_up_prompt_kernel(x_ref, wa_ref, wb_ref, cwa_ref, cwb_ref, cba_ref, cbb_ref, act_ref, ta_ref, tb_ref,
                          bufa, bufb, *, tm, tiles_per_seq):
    mi = pl.program_id(1)

    @pl.when(mi % tiles_per_seq == 0)
    def _():
        bufa[0:CONV_HALO, :] = jnp.zeros((CONV_HALO, bufa.shape[1]), f32)
        bufb[0:CONV_HALO, :] = jnp.zeros((CONV_HALO, bufb.shape[1]), f32)

    x = x_ref[...]
    halves = []
    for w_ref, cw_ref, cb_ref, buf, tail_ref in ((wa_ref, cwa_ref, cba_ref, bufa, ta_ref),
                                                 (wb_ref, cwb_ref, cbb_ref, bufb, tb_ref)):
        up = _dot(x, w_ref[...])
        buf[CONV_HALO:CONV_HALO + tm, :] = up
        conv = cb_ref[...] + buf[CONV_HALO - 2:CONV_HALO - 2 + tm, :] * cw_ref[0:1, :]
        conv = conv + buf[CONV_HALO - 1:CONV_HALO - 1 + tm, :] * cw_ref[1:2, :]
        conv = conv + up * cw_ref[2:3, :]
        tail_ref[0] = up[tm - (CONV_W - 1):, :]
        buf[0:CONV_HALO, :] = up[tm - CONV_HALO:, :]
        halves.append(conv)
    act_ref[...] = (_silu(halves[0]) * halves[1]).astype(act_ref.dtype)


def _ffn_up_prompt(hn, w_up, conv_w, conv_b, batch, seq, name):
    m, d = hn.shape
    d_ff = w_up.shape[1] // 2
    tm = _tile(seq, (1024, 512, 256, 128))
    tn = _tile(d_ff, (256, 128))
    nb = d_ff // tn
    tiles_per_seq = seq // tm
    wspec = lambda off: pl.BlockSpec((d, tn), lambda j, i, off=off: (0, j + off))
    cwspec = lambda off: pl.BlockSpec((CONV_W, tn), lambda j, i, off=off: (0, j + off))
    cbspec = lambda off: pl.BlockSpec((1, tn), lambda j, i, off=off: (0, j + off))
    tail = pl.BlockSpec((1, CONV_W - 1, tn), lambda j, i: (i // tiles_per_seq, 0, j))
    tail_shape = jax.ShapeDtypeStruct((batch, CONV_W - 1, d_ff), f32)
    return pl.pallas_call(
        functools.partial(_ffn_up_prompt_kernel, tm=tm, tiles_per_seq=tiles_per_seq),
        grid=(nb, m // tm),
        in_specs=[pl.BlockSpec((tm, d), lambda j, i: (i, 0)), wspec(0), wspec(nb),
                  cwspec(0), cwspec(nb), cbspec(0), cbspec(nb)],
        out_specs=[pl.BlockSpec((tm, tn), lambda j, i: (i, j)), tail, tail],
        out_shape=[jax.ShapeDtypeStruct((m, d_ff), bf16), tail_shape, tail_shape],
        scratch_shapes=[pltpu.VMEM((CONV_HALO + tm, tn), f32), pltpu.VMEM((CONV_HALO + tm, tn), f32)],
        compiler_params=_params("parallel", "arbitrary"),
        name=name,
    )(hn, w_up, w_up, conv_w, conv_w, conv_b, conv_b)


def _ffn_up_sample_kernel(x_ref, wa_ref, wb_ref, cwa_ref, cwb_ref, cba_ref, cbb_ref, h0a_ref, h0b_ref,
                          h1a_ref, h1b_ref, act_ref, upa_ref, upb_ref):
    x = x_ref[...]
    halves = []
    for w_ref, cw_ref, cb_ref, h0_ref, h1_ref, up_ref in (
            (wa_ref, cwa_ref, cba_ref, h0a_ref, h1a_ref, upa_ref),
            (wb_ref, cwb_ref, cbb_ref, h0b_ref, h1b_ref, upb_ref)):
        up = _dot(x, w_ref[...])
        up_ref[...] = up
        conv = cb_ref[...] + h0_ref[...] * cw_ref[0:1, :]
        conv = conv + h1_ref[...] * cw_ref[1:2, :]
        halves.append(conv + up * cw_ref[2:3, :])
    act_ref[...] = (_silu(halves[0]) * halves[1]).astype(act_ref.dtype)


def _ffn_up_sample(hn, w_up, conv_w, conv_b, hist0, hist1, name):
    m, d = hn.shape
    d_ff = w_up.shape[1] // 2
    tn = _tile(d_ff, (256, 128))
    nb = d_ff // tn
    wspec = lambda off: pl.BlockSpec((d, tn), lambda j, off=off: (0, j + off))
    cwspec = lambda off: pl.BlockSpec((CONV_W, tn), lambda j, off=off: (0, j + off))
    cbspec = lambda off: pl.BlockSpec((1, tn), lambda j, off=off: (0, j + off))
    rowspec = lambda off: pl.BlockSpec((m, tn), lambda j, off=off: (0, j + off))
    up_shape = jax.ShapeDtypeStruct((m, d_ff), f32)
    return pl.pallas_call(
        _ffn_up_sample_kernel,
        grid=(nb,),
        in_specs=[pl.BlockSpec((m, d), lambda j: (0, 0)), wspec(0), wspec(nb), cwspec(0), cwspec(nb),
                  cbspec(0), cbspec(nb), rowspec(0), rowspec(nb), rowspec(0), rowspec(nb)],
        out_specs=[rowspec(0), rowspec(0), rowspec(0)],
        out_shape=[jax.ShapeDtypeStruct((m, d_ff), bf16), up_shape, up_shape],
        compiler_params=_params("parallel"),
        name=name,
    )(hn, w_up, w_up, conv_w, conv_w, conv_b, conv_b, hist0, hist0, hist1, hist1)


def _split_in_proj(w_in):
    d = w_in.shape[0]
    pad = jnp.zeros((d, GROUP_W - NSA_GATE_W), w_in.dtype)
    w_qkvg = jnp.concatenate([w_in[:, :QKV_W + NSA_GATE_W], pad], axis=1).astype(bf16)
    uv_end = QKV_W + NSA_GATE_W + d
    return w_qkvg, w_in[:, QKV_W + NSA_GATE_W:uv_end].astype(bf16), w_in[:, uv_end:].astype(bf16)


def _in_proj(x2d, norm_g, weights, uv_dtype, tag):
    w_qkvg, w_uv, w_gm = weights
    xn = _rmsnorm(x2d, norm_g, bf16, f"norm1_{tag}")
    z1 = _matmul(xn, w_qkvg, f32, f"in_proj_qkv_{tag}")
    uv = _matmul(xn, w_uv, uv_dtype, f"in_proj_uv_{tag}", epilogue=_gelu_tanh)
    gm = _matmul(xn, w_gm, bf16, f"in_proj_gate_{tag}", epilogue=_sigmoid)
    return z1, uv, gm


def _tail(x2d, o_nsa, o_sg, gm, w_pa, w_pb, w_out, norm2_g, tag):
    m = _merge(o_nsa, w_pa, o_sg, w_pb, gm, f"merge_{tag}")
    h = _matmul(m, w_out, f32, f"out_proj_{tag}", epilogue=lambda acc, x: x + acc, extras=(x2d,))
    hn = _rmsnorm(h, norm2_g, bf16, f"norm2_{tag}")
    return h, hn


def kernel(x_prompt, x_sample, cache_k_cmp, cache_v_cmp, cache_k_slc, cache_v_slc, cache_k_win, cache_v_win,
           state_ffn_conv, page_table, norm1_g, w_in, cmp_pool_k, cmp_bias_k, cmp_w_k, cmp_pool_v, cmp_bias_v,
           cmp_w_v, sg_norm_g, sg_w, sg_b, w_proj_a, w_proj_b, w_out, norm2_g, w_up, conv_w, conv_b, w_down,
           norm_f_g):
    batch, seq, d_model = x_prompt.shape
    n_dec, dec_seq, _ = x_sample.shape
    depth = w_in.shape[0]
    page = cache_k_cmp.shape[2]
    past = page_table.shape[1] * page
    assert dec_seq == 1 and cache_k_win.shape[2] == WINDOW and past >= WINDOW
    assert seq % SG_CHUNK == 0 and seq % L_SEL == 0 and past % L_SEL == 0
    nblk_p = max(seq // L_SEL, N_SEL)
    nblk_s = -(-max(-(-(past + 1) // L_SEL), N_SEL) // 8) * 8
    rows_s = 8

    tables_p = _rope_tables(jnp.arange(seq), GROUP)
    tables_s_kv = _rope_tables(jnp.full((n_dec,), past), GROUP)
    tables_s_q = _rope_tables(jnp.full((rows_s,), past), N_HEADS)

    hp = x_prompt.reshape(batch * seq, d_model)
    hs = x_sample.reshape(n_dec, d_model)
    new_p = [[] for _ in range(7)]
    new_s = [[] for _ in range(8)]
    for l in range(depth):
        weights_in = _split_in_proj(w_in[l])
        wpa, wpb, wout = w_proj_a[l].astype(bf16), w_proj_b[l].astype(bf16), w_out[l].astype(bf16)
        wup, wdown = w_up[l].astype(bf16), w_down[l].astype(bf16)
        pwk, bk, pwv, bv = cmp_pool_k[l], cmp_bias_k[l], cmp_pool_v[l], cmp_bias_v[l]
        cw = conv_w[l]
        cb = conv_b[l].reshape(1, -1)

        z1, uv, gm = _in_proj(hp, norm1_g[l], weights_in, bf16, "p")
        kc, vc, ks, vs, kw, vw = _prep_kv(z1, tables_p, seq, "prep_p")
        ak, av = _compress_prompt(kc, vc, pwk, bk, pwv, bv, batch, seq, "compress_p")
        tq = _tile(seq, (256, 128))
        ocmp, sel = _cmp_select(z1, ak, av, cmp_w_k[l], cmp_w_v[l], n_batch=batch, n_q=seq // tq, tq=tq,
                                nblk=nblk_p, pos_base=0, pos_stride=1, emit_indices=False, name="cmp_select_p")
        gates = z1[:, QKV_W:QKV_W + NSA_GATE_W].reshape(batch * seq, N_KV_HEADS, 3 * GROUP).transpose(1, 0, 2)
        o_nsa = _nsa_prompt(z1, tables_p, ks, vs, kw, vw, sel, ocmp, gates, batch=batch, seq=seq, nblk=nblk_p,
                            name="nsa_p")
        o_sg = _sgu_prompt(uv, sg_norm_g[l], sg_w[l], sg_b[l], "sgu_p")
        h, hn = _tail(hp, o_nsa, o_sg, gm, wpa, wpb, wout, norm2_g[l], "p")
        act, tail_a, tail_b = _ffn_up_prompt(hn, wup, cw, cb, batch, seq, "ffn_up_p")
        hp = _matmul_residual_ksplit(act, wdown, h, "ffn_down_p")
        kv5 = lambda t: t.reshape(batch, seq, N_KV_HEADS, HEAD_DIM)
        keep_p = min(WINDOW, seq)
        for lst, val in zip(new_p, (kv5(kc), kv5(vc), kv5(ks), kv5(vs), kv5(kw)[:, seq - keep_p:],
                                    kv5(vw)[:, seq - keep_p:], jnp.concatenate([tail_a, tail_b], axis=-1))):
            lst.append(val)

        z1, uv, gm = _in_proj(hs, norm1_g[l], weights_in, f32, "s")
        kc, vc, ks, vs, kw, vw = _prep_kv(z1, tables_s_kv, n_dec, "prep_s")
        ak, av = _compress_paged(cache_k_cmp[l], cache_v_cmp[l], page_table, pwk, bk, pwv, bv, "compress_s")
        ak = ak.reshape(n_dec, -1, KV_W)
        av = av.reshape(n_dec, -1, KV_W)
        q8 = jnp.broadcast_to(z1[:, None, :Q_W], (n_dec, rows_s, Q_W)).reshape(n_dec * rows_s, Q_W)
        ocmp, idx = _cmp_select(q8, ak, av, cmp_w_k[l], cmp_w_v[l], n_batch=n_dec, n_q=1, tq=rows_s,
                                nblk=nblk_s, pos_base=past, pos_stride=0, emit_indices=True, name="cmp_select_s")
        kv3 = lambda t: t.reshape(n_dec, 1, KV_W)
        win_k = jnp.concatenate([cache_k_win[l].reshape(n_dec, WINDOW, KV_W)[:, 1:], kv3(kw)], axis=1)
        win_v = jnp.concatenate([cache_v_win[l].reshape(n_dec, WINDOW, KV_W)[:, 1:], kv3(vw)], axis=1)
        gates = jnp.pad(z1[:, QKV_W:QKV_W + NSA_GATE_W], ((0, 0), (0, 128 - NSA_GATE_W))).reshape(n_dec, 1, 128)
        o_nsa = _nsa_sample(idx[:, :, :, 0].reshape(-1), page_table, q8, tables_s_q, kv3(ks), kv3(vs), win_k, win_v,
                            ocmp, gates, cache_k_slc[l], cache_v_slc[l], rows=rows_s, pos=past, name="nsa_s")
        o_nsa = o_nsa.reshape(n_dec, rows_s, Q_W)[:, 0].astype(bf16)
        o_sg, sg_v = _sgu_sample(uv, sg_norm_g[l], sg_w[l], sg_b[l], "sgu_s")
        h, hn = _tail(hs, o_nsa, o_sg, gm, wpa, wpb, wout, norm2_g[l], "s")
        hist = state_ffn_conv[l]
        act, up_a, up_b = _ffn_up_sample(hn, wup, cw, cb, hist[:, 0], hist[:, 1], "ffn_up_s")
        hs = _matmul_residual_ksplit(act, wdown, h, "ffn_down_s")
        kv5 = lambda t: t.reshape(n_dec, 1, N_KV_HEADS, HEAD_DIM)
        conv_s = jnp.stack([hist[:, 1], jnp.concatenate([up_a, up_b], axis=-1)], axis=1)
        for lst, val in zip(new_s, (kv5(kc), kv5(vc), kv5(ks), kv5(vs),
                                    win_k.reshape(n_dec, WINDOW, N_KV_HEADS, HEAD_DIM),
                                    win_v.reshape(n_dec, WINDOW, N_KV_HEADS, HEAD_DIM),
                                    sg_v.reshape(n_dec, 1, -1), conv_s)):
            lst.append(val)

    y_prompt = _rmsnorm(hp, norm_f_g, f32, "norm_f_p").reshape(batch, seq, d_model)
    y_sample = _rmsnorm(hs, norm_f_g, f32, "norm_f_s").reshape(n_dec, 1, d_model)
    return (y_prompt, y_sample, *(jnp.stack(v) for v in new_p), *(jnp.stack(v) for v in new_s))
```

```python
import functools

import jax
import jax.numpy as jnp
from jax import lax
from jax.experimental import pallas as pl
from jax.experimental.pallas import tpu as pltpu

f32 = jnp.float32
bf16 = jnp.bfloat16
HIGHEST = lax.Precision.HIGHEST

N_HEADS = 16
N_KV_HEADS = 4
GROUP = N_HEADS // N_KV_HEADS
HEAD_DIM = 128
ROT_DIM = HEAD_DIM // 4
ROT_HALF = ROT_DIM // 2
ROPE_THETA = 500000.0
L_CMP = 32
STRIDE = 16
L_SEL = 64
N_SEL = 16
WINDOW = 512
FORCE_BONUS = 1000.0
SCALE = HEAD_DIM ** -0.5
NEG_INF = -1e30
SG_GROUPS = 16
SG_CHUNK = 128
CONV_W = 3
EPS = 1e-6

SUBLANES = 8
LANES = 128
GROUP_W = GROUP * HEAD_DIM
Q_W = N_HEADS * HEAD_DIM
KV_W = N_KV_HEADS * HEAD_DIM
NSA_GATE_W = 3 * N_HEADS
QKV_W = Q_W + 6 * KV_W
Z1_W = QKV_W + GROUP_W
UV_SHIFT = NSA_GATE_W
COL_KC, COL_VC, COL_KS, COL_VS, COL_KW, COL_VW, COL_GATE = (Q_W // GROUP_W + i for i in range(7))

VMEM_LIMIT_BYTES = 56 * 1024 * 1024


def _params(*semantics):
    return pltpu.CompilerParams(dimension_semantics=semantics, vmem_limit_bytes=VMEM_LIMIT_BYTES)


def _tile(n, candidates):
    for c in candidates:
        if n % c == 0:
            return c
    return n


def _sigmoid(x):
    return 1.0 / (1.0 + jnp.exp(-x))


def _silu(x):
    return x * _sigmoid(x)


def _gelu_tanh(x):
    return x * (0.5 * (1.0 + jnp.tanh(0.7978845608028654 * (x + 0.044715 * (x * x * x)))))


def _dot(a, b, precision=None):
    return jnp.dot(a, b, preferred_element_type=f32, precision=precision)


def _dot_nt(a, b, precision=None):
    return lax.dot_general(a, b, (((1,), (1,)), ((), ())), precision=precision, preferred_element_type=f32)


def _rope(x, c, sa, sb):
    lanes = x.shape[-1]
    return x * c + pltpu.roll(x, lanes - ROT_HALF, 1) * sa + pltpu.roll(x, ROT_HALF, 1) * sb


def _rope_tables(pos, reps):
    n = pos.shape[0]
    inv_freq = jnp.power(jnp.float32(ROPE_THETA), -jnp.arange(ROT_HALF, dtype=f32) / ROT_HALF)
    ang = pos.astype(f32)[:, None] * inv_freq[None, :]
    cos, sin = jnp.cos(ang), jnp.sin(ang)
    zeros = lambda w: jnp.zeros((n, w), f32)
    c = jnp.concatenate([cos, cos, jnp.ones((n, HEAD_DIM - ROT_DIM), f32)], axis=1)
    sa = jnp.concatenate([-sin, zeros(HEAD_DIM - ROT_HALF)], axis=1)
    sb = jnp.concatenate([zeros(ROT_HALF), sin, zeros(HEAD_DIM - ROT_DIM)], axis=1)
    return tuple(jnp.tile(t, (1, reps)) for t in (c, sa, sb))


def _rmsnorm_kernel(x_ref, g_ref, o_ref):
    x = x_ref[...].astype(f32)
    y = x * lax.rsqrt(jnp.mean(x * x, axis=-1, keepdims=True) + EPS)
    o_ref[...] = (y * g_ref[...]).astype(o_ref.dtype)


def _rmsnorm(x, g, out_dtype, name):
    m, d = x.shape
    tm = _tile(m, (256, 128, 64, 32, 16, 8))
    return pl.pallas_call(
        _rmsnorm_kernel,
        grid=(m // tm,),
        in_specs=[pl.BlockSpec((tm, d), lambda i: (i, 0)), pl.BlockSpec((1, d), lambda i: (0, 0))],
        out_specs=pl.BlockSpec((tm, d), lambda i: (i, 0)),
        out_shape=jax.ShapeDtypeStruct((m, d), out_dtype),
        compiler_params=_params("parallel"),
        name=name,
    )(x, g.reshape(1, d).astype(f32))


CAST_ROWS = 512


def _cast_weight_tile(w_ref, whi_ref, wbf_ref, shift):
    k, tn = w_ref.shape
    step = _tile(k, (CAST_ROWS, 256, 128))
    for r0 in range(0, k, step):
        rows = slice(r0, r0 + step)
        if shift:
            wide = jnp.concatenate([w_ref[rows, :], whi_ref[rows, :]], axis=1)
            wbf_ref[rows, :] = pltpu.roll(wide, wide.shape[1] - shift, 1)[:, :tn].astype(bf16)
        else:
            wbf_ref[rows, :] = w_ref[rows, :].astype(bf16)


def _mm_wres_kernel(ap_ref, as_ref, w_ref, *rest, epilogue, shift, n_extra, n_m):
    rest = list(rest)
    whi_ref = rest.pop(0) if shift else None
    extras_p, extras_s = rest[:n_extra], rest[n_extra:2 * n_extra]
    op_ref, os_ref, wbf_ref = rest[2 * n_extra:]
    mi = pl.program_id(1)

    @pl.when(mi == 0)
    def _():
        _cast_weight_tile(w_ref, whi_ref, wbf_ref, shift)

    w = wbf_ref[...]
    op_ref[...] = epilogue(_dot(ap_ref[...], w), *[e[...] for e in extras_p]).astype(op_ref.dtype)

    @pl.when(mi == n_m - 1)
    def _():
        os_ref[...] = epilogue(_dot(as_ref[...], w), *[e[...] for e in extras_s]).astype(os_ref.dtype)


def _matmul_wres(a_p, a_s, w, n_cols, out_dtypes, name, *, col_block0=0, shift=0, epilogue=lambda acc: acc,
                 extras_p=(), extras_s=()):
    mp, k = a_p.shape
    ms = a_s.shape[0]
    tm = _tile(mp, (1024, 512, 256, 128))
    tn = _tile(n_cols, (512, 256, 128))
    n_m, n_n = mp // tm, n_cols // tn
    hi_per_tile = tn // LANES
    in_specs = [pl.BlockSpec((tm, k), lambda j, i: (i, 0)),
                pl.BlockSpec((ms, k), lambda j, i: (0, 0)),
                pl.BlockSpec((k, tn), lambda j, i: (0, col_block0 + j))]
    operands = [a_p, a_s, w]
    if shift:
        in_specs.append(pl.BlockSpec((k, LANES), lambda j, i: (0, (col_block0 + j + 1) * hi_per_tile)))
        operands.append(w)
    in_specs += [pl.BlockSpec((tm, tn), lambda j, i: (i, j)) for _ in extras_p]
    in_specs += [pl.BlockSpec((ms, tn), lambda j, i: (0, j)) for _ in extras_s]
    return pl.pallas_call(
        functools.partial(_mm_wres_kernel, epilogue=epilogue, shift=shift, n_extra=len(extras_p), n_m=n_m),
        grid=(n_n, n_m),
        in_specs=in_specs,
        out_specs=[pl.BlockSpec((tm, tn), lambda j, i: (i, j)), pl.BlockSpec((ms, tn), lambda j, i: (0, j))],
        out_shape=[jax.ShapeDtypeStruct((mp, n_cols), out_dtypes[0]),
                   jax.ShapeDtypeStruct((ms, n_cols), out_dtypes[1])],
        scratch_shapes=[pltpu.VMEM((k, tn), bf16)],
        compiler_params=_params("parallel", "arbitrary"),
        name=name,
    )(*operands, *extras_p, *extras_s)


def _merge_kernel(ap_ref, as_ref, bp_ref, bs_ref, wa_ref, wb_ref, gap_ref, gbp_ref, gas_ref, gbs_ref,
                  op_ref, os_ref, wabf_ref, wbbf_ref, *, n_m):
    mi = pl.program_id(1)

    @pl.when(mi == 0)
    def _():
        _cast_weight_tile(wa_ref, None, wabf_ref, 0)
        _cast_weight_tile(wb_ref, None, wbbf_ref, 0)

    def mix(a_ref, b_ref, ga_ref, gb_ref, o_ref):
        pa = _dot(a_ref[...], wabf_ref[...])
        pb = _dot(b_ref[...], wbbf_ref[...])
        o_ref[...] = (ga_ref[...].astype(f32) * pa + gb_ref[...].astype(f32) * pb).astype(o_ref.dtype)

    mix(ap_ref, bp_ref, gap_ref, gbp_ref, op_ref)

    @pl.when(mi == n_m - 1)
    def _():
        mix(as_ref, bs_ref, gas_ref, gbs_ref, os_ref)


def _merge(nsa_p, nsa_s, sg_p, sg_s, w_pa, w_pb, gates_p, gates_s, name):
    mp, ka = nsa_p.shape
    ms = nsa_s.shape[0]
    kb = sg_p.shape[1]
    d = w_pa.shape[1]
    tm = _tile(mp, (1024, 512, 256, 128))
    tn = _tile(d, (512, 256, 128))
    n_m, n_n = mp // tm, d // tn
    row_p = lambda w: pl.BlockSpec((tm, w), lambda j, i: (i, 0))
    row_s = lambda w: pl.BlockSpec((ms, w), lambda j, i: (0, 0))
    return pl.pallas_call(
        functools.partial(_merge_kernel, n_m=n_m),
        grid=(n_n, n_m),
        in_specs=[row_p(ka), row_s(ka), row_p(kb), row_s(kb),
                  pl.BlockSpec((ka, tn), lambda j, i: (0, j)), pl.BlockSpec((kb, tn), lambda j, i: (0, j)),
                  pl.BlockSpec((tm, tn), lambda j, i: (i, j)), pl.BlockSpec((tm, tn), lambda j, i: (i, j + n_n)),
                  pl.BlockSpec((ms, tn), lambda j, i: (0, j)), pl.BlockSpec((ms, tn), lambda j, i: (0, j + n_n))],
        out_specs=[pl.BlockSpec((tm, tn), lambda j, i: (i, j)), pl.BlockSpec((ms, tn), lambda j, i: (0, j))],
        out_shape=[jax.ShapeDtypeStruct((mp, d), bf16), jax.ShapeDtypeStruct((ms, d), bf16)],
        scratch_shapes=[pltpu.VMEM((ka, tn), bf16), pltpu.VMEM((kb, tn), bf16)],
        compiler_params=_params("parallel", "arbitrary"),
        name=name,
    )(nsa_p, nsa_s, sg_p, sg_s, w_pa, w_pb, gates_p, gates_p, gates_s, gates_s)


def _mm_resid_kernel(a_ref, w_ref, r_ref, o_ref):
    o_ref[...] = r_ref[...] + _dot(a_ref[...], w_ref[...])


def _matmul_residual(a, w, resid, name):
    m, k = a.shape
    n = w.shape[1]
    tm = _tile(m, (512, 256, 128, 64, 32, 16))
    tn = _tile(n, (256, 128))
    return pl.pallas_call(
        _mm_resid_kernel,
        grid=(m // tm, n // tn),
        in_specs=[pl.BlockSpec((tm, k), lambda i, j: (i, 0)), pl.BlockSpec((k, tn), lambda i, j: (0, j)),
                  pl.BlockSpec((tm, tn), lambda i, j: (i, j))],
        out_specs=pl.BlockSpec((tm, tn), lambda i, j: (i, j)),
        out_shape=jax.ShapeDtypeStruct((m, n), f32),
        compiler_params=_params("parallel", "arbitrary"),
        name=name,
    )(a, w, resid)


def _prep_kernel(kc, vc, ks, vs, kw, vw, c_ref, sa_ref, sb_ref, okc, ovc, oks, ovs, okw, ovw):
    okc[...] = kc[...]
    ovc[...] = vc[...]
    ovs[...] = vs[...]
    ovw[...] = vw[...]
    c, sa, sb = c_ref[...], sa_ref[...], sb_ref[...]
    oks[...] = _rope(ks[...], c, sa, sb)
    okw[...] = _rope(kw[...], c, sa, sb)


def _prep_kv(z1, tables, rows_per_seq, name):
    m = z1.shape[0]
    tr = _tile(rows_per_seq, (256, 128, 64, 32, 16, 8))
    n_tab = tables[0].shape[0] // tr
    col = lambda cb: pl.BlockSpec((tr, KV_W), lambda i, cb=cb: (i, cb))
    tab = pl.BlockSpec((tr, KV_W), lambda i: (i % n_tab, 0))
    out = pl.BlockSpec((tr, KV_W), lambda i: (i, 0))
    return pl.pallas_call(
        _prep_kernel,
        grid=(m // tr,),
        in_specs=[col(COL_KC), col(COL_VC), col(COL_KS), col(COL_VS), col(COL_KW), col(COL_VW), tab, tab, tab],
        out_specs=[out] * 6,
        out_shape=[jax.ShapeDtypeStruct((m, KV_W), f32)] * 6,
        compiler_params=_params("parallel"),
        name=name,
    )(z1, z1, z1, z1, z1, z1, *tables)


def _compress_prompt_kernel(k_ref, v_ref, pwk_ref, bk_ref, pwv_ref, bv_ref, ak_ref, av_ref, *, nsub):
    for x_ref, pw_ref, b_ref, o_ref in ((k_ref, pwk_ref, bk_ref, ak_ref), (v_ref, pwv_ref, bv_ref, av_ref)):
        first = None
        second = None
        for j in range(STRIDE):
            rows = x_ref[pl.ds(j, nsub, stride=STRIDE), :]
            fa = rows * pw_ref[j:j + 1, :]
            sa = rows * pw_ref[STRIDE + j:STRIDE + j + 1, :]
            first = fa if first is None else first + fa
            second = sa if second is None else second + sa
        o_ref[0] = _silu(pltpu.roll(first, 1, 0) + second + b_ref[...])


def _compress_prompt(kc, vc, pwk, bk, pwv, bv, batch, seq, name):
    nsub = seq // STRIDE
    xspec = pl.BlockSpec((seq, HEAD_DIM), lambda b, g: (b, g))
    pwspec = pl.BlockSpec((L_CMP, HEAD_DIM), lambda b, g: (0, 0))
    bspec = pl.BlockSpec((1, HEAD_DIM), lambda b, g: (0, 0))
    ospec = pl.BlockSpec((1, nsub, HEAD_DIM), lambda b, g: (b, 0, g))
    return pl.pallas_call(
        functools.partial(_compress_prompt_kernel, nsub=nsub),
        grid=(batch, N_KV_HEADS),
        in_specs=[xspec, xspec, pwspec, bspec, pwspec, bspec],
        out_specs=[ospec, ospec],
        out_shape=[jax.ShapeDtypeStruct((batch, nsub, KV_W), f32)] * 2,
        compiler_params=_params("parallel", "parallel"),
        name=name,
    )(kc, vc, pwk, bk.reshape(1, HEAD_DIM), pwv, bv.reshape(1, HEAD_DIM))


PAGES_PER_CHUNK = 8


def _compress_paged_kernel(pt_ref, wfk_ref, wsk_ref, bk_ref, wfv_ref, wsv_ref, bv_ref, poolk_ref, poolv_ref,
                           ak_ref, av_ref, bufk, bufv, sems, *, n_pages, page_rows):
    b = pl.program_id(0)
    pps = PAGES_PER_CHUNK
    n_chunks = n_pages // pps
    tiles_per_sub = STRIDE * N_KV_HEADS // SUBLANES
    nsub = page_rows // (STRIDE * N_KV_HEADS)

    def copies(chunk, slot):
        out = []
        for p in range(pps):
            phys = pt_ref[b * n_pages + chunk * pps + p]
            out.append(pltpu.make_async_copy(poolk_ref.at[phys], bufk.at[slot, p], sems.at[0, slot]))
            out.append(pltpu.make_async_copy(poolv_ref.at[phys], bufv.at[slot, p], sems.at[1, slot]))
        return out

    for cp in copies(0, 0):
        cp.start()
    carry = (jnp.zeros((SUBLANES, HEAD_DIM), f32), jnp.zeros((SUBLANES, HEAD_DIM), f32))
    for chunk in range(n_chunks):
        slot = chunk % 2
        if chunk + 1 < n_chunks:
            for cp in copies(chunk + 1, 1 - slot):
                cp.start()
        for cp in copies(chunk, slot):
            cp.wait()

        def page_body(p, prev, chunk=chunk, slot=slot):
            out_row0 = pl.multiple_of((chunk * pps + p) * nsub * SUBLANES, nsub * SUBLANES)
            new_prev = []
            for which, (buf, wf_ref, ws_ref, b_ref, o_ref) in enumerate(
                    ((bufk, wfk_ref, wsk_ref, bk_ref, ak_ref), (bufv, wfv_ref, wsv_ref, bv_ref, av_ref))):
                last = prev[which]
                for n in range(nsub):
                    first = None
                    second = None
                    for t in range(tiles_per_sub):
                        x = buf[slot, p, pl.ds((n * tiles_per_sub + t) * SUBLANES, SUBLANES), :]
                        fa = x * wf_ref[t]
                        sa = x * ws_ref[t]
                        first = fa if first is None else first + fa
                        second = sa if second is None else second + sa
                    first = first + pltpu.roll(first, SUBLANES // 2, 0)
                    second = second + pltpu.roll(second, SUBLANES // 2, 0)
                    o_ref[pl.ds(out_row0 + n * SUBLANES, SUBLANES), :] = _silu(last + second + b_ref[...])
                    last = first
                new_prev.append(last)
            return tuple(new_prev)

        carry = lax.fori_loop(0, pps, page_body, carry)


def _compress_paged(pool_k, pool_v, page_table, pwk, bk, pwv, bv, name):
    n_batch, n_pages = page_table.shape
    n_phys, page = pool_k.shape[:2]
    assert N_KV_HEADS * 2 == SUBLANES and n_pages % PAGES_PER_CHUNK == 0 and page % STRIDE == 0
    page_rows = page * N_KV_HEADS
    out_rows = n_pages * (page // STRIDE) * SUBLANES

    def tile_weights(pw):
        return jnp.repeat(pw.reshape(STRIDE // 2, 2, HEAD_DIM), N_KV_HEADS, axis=1)

    full3 = pl.BlockSpec((STRIDE // 2, SUBLANES, HEAD_DIM), lambda b, pt: (0, 0, 0))
    brow = pl.BlockSpec((SUBLANES, HEAD_DIM), lambda b, pt: (0, 0))
    o_spec = pl.BlockSpec((out_rows, HEAD_DIM), lambda b, pt: (b, 0))
    out_shape = jax.ShapeDtypeStruct((n_batch * out_rows, HEAD_DIM), f32)
    buf = pltpu.VMEM((2, PAGES_PER_CHUNK, page_rows, HEAD_DIM), f32)
    return pl.pallas_call(
        functools.partial(_compress_paged_kernel, n_pages=n_pages, page_rows=page_rows),
        grid_spec=pltpu.PrefetchScalarGridSpec(
            num_scalar_prefetch=1,
            grid=(n_batch,),
            in_specs=[full3, full3, brow, full3, full3, brow,
                      pl.BlockSpec(memory_space=pl.ANY), pl.BlockSpec(memory_space=pl.ANY)],
            out_specs=[o_spec, o_spec],
            scratch_shapes=[buf, buf, pltpu.SemaphoreType.DMA((2, 2))],
        ),
        out_shape=[out_shape, out_shape],
        compiler_params=_params("arbitrary"),
        name=name,
    )(page_table.reshape(-1),
      tile_weights(pwk[:STRIDE]), tile_weights(pwk[STRIDE:]), jnp.tile(bk.reshape(1, HEAD_DIM), (SUBLANES, 1)),
      tile_weights(pwv[:STRIDE]), tile_weights(pwv[STRIDE:]), jnp.tile(bv.reshape(1, HEAD_DIM), (SUBLANES, 1)),
      pool_k.reshape(n_phys, page_rows, HEAD_DIM), pool_v.reshape(n_phys, page_rows, HEAD_DIM))


def _block_overlap(blk, rown):
    c_start = (rown - 1) * STRIDE
    return (rown >= 1) & (c_start < blk * L_SEL + L_SEL) & (c_start + L_CMP > blk * L_SEL)


def _stable_rank(score, score_ref, blk, nblk):
    score_ref[...] = score

    def count_better(i, cnt):
        other = score_ref[pl.ds(i, 1), :]
        better = (other > score) | ((other == score) & (i < blk))
        return cnt + better.astype(f32)

    return lax.fori_loop(0, nblk, count_better, jnp.zeros(score.shape, f32))


def _cmp_select_prompt_kernel(q_ref, ak_ref, av_ref, wk_ref, wv_ref, ocmp_ref, sel_ref, score_ref, *,
                              tq, nrows, nblk):
    pos0 = pl.program_id(2) * tq
    pos_col = pos0 + lax.broadcasted_iota(jnp.int32, (tq, 1), 0)
    pos_row = pos0 + lax.broadcasted_iota(jnp.int32, (1, tq), 1)

    ck = _dot(ak_ref[0], wk_ref[...], precision=HIGHEST)
    cv = _dot(av_ref[0].astype(bf16), wv_ref[...].astype(bf16)).astype(bf16)

    rown = lax.broadcasted_iota(jnp.int32, (1, nrows), 1)
    visible = (rown >= 1) & (rown * STRIDE + (L_CMP - STRIDE - 1) <= pos_col)

    pg = jnp.zeros((tq, nrows), f32)
    for r in range(GROUP):
        sl = slice(r * HEAD_DIM, (r + 1) * HEAD_DIM)
        s = _dot_nt(q_ref[:, sl], ck, precision=HIGHEST) * SCALE
        s = jnp.where(visible, s, NEG_INF)
        e = jnp.where(visible, jnp.exp(s - jnp.max(s, axis=-1, keepdims=True)), 0.0)
        den = jnp.sum(e, axis=-1, keepdims=True)
        p = jnp.where(den > 0.0, e / jnp.where(den > 0.0, den, 1.0), 0.0)
        ocmp_ref[:, sl] = _dot(p.astype(bf16), cv)
        pg = pg + p

    blk = lax.broadcasted_iota(jnp.int32, (nblk, 1), 0)
    imp_t = _dot_nt(_block_overlap(blk, rown).astype(f32), pg, precision=HIGHEST)
    qb = pos_row // L_SEL
    causal = blk <= qb
    forced = (blk == 0) | (blk == qb) | (blk == qb - 1)
    score = jnp.where(causal, imp_t + FORCE_BONUS * forced.astype(f32), -jnp.inf)
    rank = _stable_rank(score, score_ref, blk, nblk)
    chosen = jnp.where(causal & (rank < float(N_SEL)), 1.0, 0.0).astype(bf16)
    eye = (lax.broadcasted_iota(jnp.int32, (tq, tq), 0)
           == lax.broadcasted_iota(jnp.int32, (tq, tq), 1)).astype(bf16)
    sel_ref[0, 0] = _dot_nt(eye, chosen)


def _cmp_select_prompt(q, ak, av, wk, wv, *, batch, seq, nblk, name):
    nrows = ak.shape[1]
    tq = _tile(seq, (512, 256, 128))
    n_q = seq // tq
    q_spec = pl.BlockSpec((tq, GROUP_W), lambda b, g, i: (b * n_q + i, g))
    a_spec = pl.BlockSpec((1, nrows, HEAD_DIM), lambda b, g, i: (b, 0, g))
    w_spec = pl.BlockSpec((HEAD_DIM, HEAD_DIM), lambda b, g, i: (0, 0))
    return pl.pallas_call(
        functools.partial(_cmp_select_prompt_kernel, tq=tq, nrows=nrows, nblk=nblk),
        grid=(batch, N_KV_HEADS, n_q),
        in_specs=[q_spec, a_spec, a_spec, w_spec, w_spec],
        out_specs=[pl.BlockSpec((tq, GROUP_W), lambda b, g, i: (b * n_q + i, g)),
                   pl.BlockSpec((1, 1, tq, nblk), lambda b, g, i: (b, g, i, 0))],
        out_shape=[jax.ShapeDtypeStruct((batch * seq, Q_W), f32),
                   jax.ShapeDtypeStruct((batch, N_KV_HEADS, seq, nblk), f32)],
        scratch_shapes=[pltpu.VMEM((nblk, tq), f32)],
        compiler_params=_params("parallel", "parallel", "arbitrary"),
        name=name,
    )(q, ak, av, wk, wv)


def _cmp_sample_kernel(q_ref, ak_ref, av_ref, wk_ref, wv_ref, ocmp_ref, imp_ref, *, nrows, nblk, pos):
    rown = lax.broadcasted_iota(jnp.int32, (nrows, 1), 0)
    visible = (rown >= 1) & (rown * STRIDE + (L_CMP - STRIDE - 1) <= pos)
    overlap = _block_overlap(lax.broadcasted_iota(jnp.int32, (nblk, 1), 0),
                             lax.broadcasted_iota(jnp.int32, (1, nrows), 1)).astype(f32)
    lane = lax.broadcasted_iota(jnp.int32, (1, SUBLANES), 1)
    is_head = lane < GROUP
    imp_all = jnp.zeros((nblk, SUBLANES), f32)
    for g in range(N_KV_HEADS):
        a_k = ak_ref[pl.ds(g, nrows, stride=SUBLANES), :]
        a_v = av_ref[pl.ds(g, nrows, stride=SUBLANES), :]
        ck = _dot(a_k, wk_ref[...], precision=HIGHEST)
        cv = _dot(a_v.astype(bf16), wv_ref[...].astype(bf16))
        s = _dot_nt(ck, q_ref[0, g], precision=HIGHEST) * SCALE
        s = jnp.where(visible, s, NEG_INF)
        e = jnp.where(visible, jnp.exp(s - jnp.max(s, axis=0, keepdims=True)), 0.0)
        den = jnp.sum(e, axis=0, keepdims=True)
        p = jnp.where(is_head & (den > 0.0), e / jnp.where(den > 0.0, den, 1.0), 0.0)
        for r in range(GROUP):
            ocmp_ref[0, g, r:r + 1, :] = jnp.sum(p[:, r:r + 1] * cv, axis=0, keepdims=True)
        ocmp_ref[0, g, GROUP:, :] = jnp.zeros((SUBLANES - GROUP, HEAD_DIM), f32)
        imp = jnp.sum(_dot(overlap, p, precision=HIGHEST), axis=1, keepdims=True)
        imp_all = jnp.where(lane == g, imp, imp_all)
    imp_ref[0] = imp_all


def _cmp_sample(q, ak, av, wk, wv, *, n_batch, nrows, nblk, pos, name):
    q_spec = pl.BlockSpec((1, N_KV_HEADS, SUBLANES, HEAD_DIM), lambda b: (b, 0, 0, 0))
    a_spec = pl.BlockSpec((nrows * SUBLANES, HEAD_DIM), lambda b: (b, 0))
    w_spec = pl.BlockSpec((HEAD_DIM, HEAD_DIM), lambda b: (0, 0))
    return pl.pallas_call(
        functools.partial(_cmp_sample_kernel, nrows=nrows, nblk=nblk, pos=pos),
        grid=(n_batch,),
        in_specs=[q_spec, a_spec, a_spec, w_spec, w_spec],
        out_specs=[q_spec, pl.BlockSpec((1, nblk, SUBLANES), lambda b: (b, 0, 0))],
        out_shape=[jax.ShapeDtypeStruct((n_batch, N_KV_HEADS, SUBLANES, HEAD_DIM), f32),
                   jax.ShapeDtypeStruct((n_batch, nblk, SUBLANES), f32)],
        compiler_params=_params("parallel"),
        name=name,
    )(q, ak, av, wk, wv)


def _rank_sample_kernel(imp_ref, idx_ref, score_ref, *, nblk, qb):
    blk = lax.broadcasted_iota(jnp.int32, (nblk, 1), 0)
    causal = blk <= qb
    forced = (blk == 0) | (blk == qb) | (blk == qb - 1)
    score = jnp.where(causal, imp_ref[...] + FORCE_BONUS * forced.astype(f32), -jnp.inf)
    rank = _stable_rank(score, score_ref, blk, nblk)
    for slot in range(N_SEL):
        hit = causal & (rank == float(slot))
        idx_ref[slot:slot + 1, :] = jnp.sum(jnp.where(hit, blk, 0), axis=0, keepdims=True)
        idx_ref[N_SEL + slot:N_SEL + slot + 1, :] = jnp.max(hit.astype(jnp.int32), axis=0, keepdims=True)


def _rank_sample(imp_t, *, nblk, qb, name):
    lanes = imp_t.shape[1]
    return pl.pallas_call(
        functools.partial(_rank_sample_kernel, nblk=nblk, qb=qb),
        grid=(1,),
        in_specs=[pl.BlockSpec((nblk, lanes), lambda i: (0, 0))],
        out_specs=pl.BlockSpec((2 * N_SEL, lanes), lambda i: (0, 0)),
        out_shape=jax.ShapeDtypeStruct((2 * N_SEL, lanes), jnp.int32),
        scratch_shapes=[pltpu.VMEM((nblk, lanes), f32)],
        compiler_params=_params("arbitrary"),
        name=name,
    )(imp_t)


def _softmax_pv(s, v):
    e = jnp.exp(s - jnp.max(s, axis=-1, keepdims=True))
    den = jnp.sum(e, axis=-1, keepdims=True)
    return _dot(e.astype(bf16), v) / den


def _masked_attention(q, k, v, mask, tq):
    heads = q.shape[0] // tq
    keys = k.shape[0]
    s = _dot_nt(q, k) * SCALE
    s = jnp.where(mask[None], s.reshape(heads, tq, keys), NEG_INF).reshape(heads * tq, keys)
    return _softmax_pv(s, v)


HEADS_PER_CHAIN = 2


def _nsa_prompt_kernel(q_ref, c_ref, sa_ref, sb_ref, ks_ref, vs_ref, kw_ref, vw_ref, sel_ref, ocmp_ref,
                       gate_ref, o_ref, *, tq, seq, nblk, span, n_q):
    qi = pl.program_id(2)
    t0 = qi * tq
    qr = _rope(q_ref[...], c_ref[...], sa_ref[...], sb_ref[...]).astype(bf16)
    n_chain = GROUP // HEADS_PER_CHAIN
    chains = [jnp.concatenate([qr[:, (c * HEADS_PER_CHAIN + h) * HEAD_DIM:(c * HEADS_PER_CHAIN + h + 1) * HEAD_DIM]
                               for h in range(HEADS_PER_CHAIN)], axis=0) for c in range(n_chain)]
    tpos = t0 + lax.broadcasted_iota(jnp.int32, (tq, 1), 0)

    start = pl.multiple_of(jnp.maximum(t0 - WINDOW, 0), tq)
    wpos = start + lax.broadcasted_iota(jnp.int32, (1, span), 1)
    wmask = (wpos <= tpos) & (wpos > tpos - WINDOW)
    kw = kw_ref[pl.ds(start, span), :].astype(bf16)
    vw = vw_ref[pl.ds(start, span), :].astype(bf16)
    o_win = [_masked_attention(qc, kw, vw, wmask, tq) for qc in chains]

    gates = _sigmoid(gate_ref[...])
    sel = sel_ref[0, 0].astype(bf16)

    tiles_per_class = 2
    for cls in range(-(-n_q // tiles_per_class)):
        ext = min(seq, (cls + 1) * tiles_per_class * tq)

        @pl.when(qi // tiles_per_class == cls)
        def _(ext=ext):
            kpos = lax.broadcasted_iota(jnp.int32, (1, ext), 1)
            expand = (kpos // L_SEL == lax.broadcasted_iota(jnp.int32, (nblk, 1), 0)).astype(bf16)
            mask = (_dot(sel, expand) > 0.5) & (kpos <= tpos)
            k = ks_ref[0:ext, :].astype(bf16)
            v = vs_ref[0:ext, :].astype(bf16)
            for c, qc in enumerate(chains):
                o_slc = _masked_attention(qc, k, v, mask, tq)
                for h in range(HEADS_PER_CHAIN):
                    r = c * HEADS_PER_CHAIN + h
                    sl = slice(r * HEAD_DIM, (r + 1) * HEAD_DIM)
                    rows = slice(h * tq, (h + 1) * tq)
                    o = (gates[:, 3 * r:3 * r + 1] * ocmp_ref[:, sl] + gates[:, 3 * r + 1:3 * r + 2] * o_slc[rows]
                         + gates[:, 3 * r + 2:3 * r + 3] * o_win[c][rows])
                    o_ref[:, sl] = o.astype(o_ref.dtype)


def _nsa_prompt(z1, tables, ks, vs, kw, vw, sel, ocmp, gates, *, batch, seq, nblk, name):
    tq = _tile(seq, (256, 128))
    n_q = seq // tq
    span = min(WINDOW + tq, seq)
    row_blk = lambda w: pl.BlockSpec((tq, w), lambda b, g, i: (b * n_q + i, g))
    tab = pl.BlockSpec((tq, GROUP_W), lambda b, g, i: (i, 0))
    kv = pl.BlockSpec((seq, HEAD_DIM), lambda b, g, i: (b, g))
    return pl.pallas_call(
        functools.partial(_nsa_prompt_kernel, tq=tq, seq=seq, nblk=nblk, span=span, n_q=n_q),
        grid=(batch, N_KV_HEADS, n_q),
        in_specs=[row_blk(GROUP_W), tab, tab, tab, kv, kv, kv, kv,
                  pl.BlockSpec((1, 1, tq, nblk), lambda b, g, i: (b, g, i, 0)),
                  row_blk(GROUP_W),
                  pl.BlockSpec((None, tq, 3 * GROUP), lambda b, g, i: (g, b * n_q + i, 0))],
        out_specs=row_blk(GROUP_W),
        out_shape=jax.ShapeDtypeStruct((batch * seq, Q_W), bf16),
        compiler_params=_params("parallel", "parallel", "arbitrary"),
        name=name,
    )(z1, *tables, ks, vs, kw, vw, sel, ocmp, gates)


def _nsa_sample_kernel(idx_ref, pt_ref, q_ref, c_ref, sa_ref, sb_ref, ksn_ref, vsn_ref, kwin_ref, vwin_ref,
                       ocmp_ref, gate_ref, poolk_ref, poolv_ref, o_ref, kbuf, vbuf, sems, *,
                       n_pages, page, pos):
    b = pl.program_id(0)
    per_page = page // L_SEL
    past_blocks = n_pages * per_page
    n_keys = N_SEL * L_SEL

    def block_copies(g, slot):
        blk = idx_ref[(b * N_KV_HEADS + g) * 2 * N_SEL + slot]
        jp = jnp.minimum(blk, past_blocks - 1)
        phys = pt_ref[b * n_pages + jp // per_page]
        off = pl.multiple_of((jp % per_page) * L_SEL, L_SEL)
        dst = pl.ds(slot * L_SEL, L_SEL)
        return (pltpu.make_async_copy(poolk_ref.at[phys, pl.ds(off, L_SEL), g], kbuf.at[g, dst], sems.at[0]),
                pltpu.make_async_copy(poolv_ref.at[phys, pl.ds(off, L_SEL), g], vbuf.at[g, dst], sems.at[1]))

    for g in range(N_KV_HEADS):
        for slot in range(N_SEL):
            for cp in block_copies(g, slot):
                cp.start()
    for g in range(N_KV_HEADS):
        for slot in range(N_SEL):
            for cp in block_copies(g, slot):
                cp.wait()

    lane = lax.broadcasted_iota(jnp.int32, (1, n_keys), 1)
    lane_slot = lane // L_SEL
    for g in range(N_KV_HEADS):
        qr = _rope(q_ref[0, g], c_ref[...], sa_ref[...], sb_ref[...]).astype(bf16)
        blk_vec = jnp.zeros((1, n_keys), jnp.int32)
        ok_vec = jnp.zeros((1, n_keys), jnp.int32)
        base = (b * N_KV_HEADS + g) * 2 * N_SEL
        for slot in range(N_SEL):
            blk_vec = jnp.where(lane_slot == slot, idx_ref[base + slot], blk_vec)
            ok_vec = jnp.where(lane_slot == slot, idx_ref[base + N_SEL + slot], ok_vec)
        in_past = blk_vec < past_blocks
        kpos = blk_vec * L_SEL + lane % L_SEL
        mask = (ok_vec > 0) & (kpos <= pos)
        gsl = slice(g * HEAD_DIM, (g + 1) * HEAD_DIM)
        k_new = ksn_ref[0][:, gsl].astype(bf16).astype(f32)
        v_new = vsn_ref[0][:, gsl].astype(bf16).astype(f32)
        s_new = jnp.sum(qr.astype(f32) * k_new, axis=-1, keepdims=True)
        s = jnp.where(in_past, _dot_nt(qr, kbuf[g].astype(bf16)), s_new) * SCALE
        s = jnp.where(mask, s, NEG_INF)
        e = jnp.exp(s - jnp.max(s, axis=-1, keepdims=True))
        den = jnp.sum(e, axis=-1, keepdims=True)
        e_new = jnp.sum(jnp.where(in_past, 0.0, e), axis=-1, keepdims=True)
        e_past = jnp.where(in_past, e, 0.0).astype(bf16)
        o_slc = (_dot(e_past, vbuf[g].astype(bf16)) + e_new.astype(bf16).astype(f32) * v_new) / den
        o_win = _softmax_pv(_dot_nt(qr, kwin_ref[0][:, gsl].astype(bf16)) * SCALE,
                            vwin_ref[0][:, gsl].astype(bf16))
        gt = _sigmoid(gate_ref[0, g])
        o_ref[0, g] = gt[:, 0:1] * ocmp_ref[0, g] + gt[:, 1:2] * o_slc + gt[:, 2:3] * o_win


def _nsa_sample(idx, page_table, q, tables, ks_new, vs_new, kwin, vwin, ocmp, gates, pool_k, pool_v, *,
                pos, name):
    n_batch, n_pages = page_table.shape
    page = pool_k.shape[1]
    window = kwin.shape[1]
    tab = pl.BlockSpec((SUBLANES, HEAD_DIM), lambda b, *_: (0, 0))
    heads = pl.BlockSpec((1, N_KV_HEADS, SUBLANES, HEAD_DIM), lambda b, *_: (b, 0, 0, 0))
    per_b3 = lambda s, w: pl.BlockSpec((1, s, w), lambda b, *_: (b, 0, 0))
    return pl.pallas_call(
        functools.partial(_nsa_sample_kernel, n_pages=n_pages, page=page, pos=pos),
        grid_spec=pltpu.PrefetchScalarGridSpec(
            num_scalar_prefetch=2,
            grid=(n_batch,),
            in_specs=[heads, tab, tab, tab,
                      per_b3(1, KV_W), per_b3(1, KV_W), per_b3(window, KV_W), per_b3(window, KV_W),
                      heads, heads,
                      pl.BlockSpec(memory_space=pl.ANY), pl.BlockSpec(memory_space=pl.ANY)],
            out_specs=heads,
            scratch_shapes=[pltpu.VMEM((N_KV_HEADS, N_SEL * L_SEL, HEAD_DIM), f32),
                            pltpu.VMEM((N_KV_HEADS, N_SEL * L_SEL, HEAD_DIM), f32),
                            pltpu.SemaphoreType.DMA((2,))],
        ),
        out_shape=jax.ShapeDtypeStruct((n_batch, N_KV_HEADS, SUBLANES, HEAD_DIM), f32),
        compiler_params=_params("arbitrary"),
        name=name,
    )(idx, page_table.reshape(-1), q, *tables, ks_new, vs_new, kwin, vwin, ocmp, gates, pool_k, pool_v)


def _sgu_prompt_kernel(u_ref, v_ref, g_ref, w_ref, bt_ref, o_ref):
    v = v_ref[...].astype(f32)
    vn = (v * lax.rsqrt(jnp.mean(v * v, axis=-1, keepdims=True) + EPS) * g_ref[...]).astype(bf16)
    tril = (lax.broadcasted_iota(jnp.int32, (SG_CHUNK, SG_CHUNK), 0)
            >= lax.broadcasted_iota(jnp.int32, (SG_CHUNK, SG_CHUNK), 1))
    bt = bt_ref[...]
    group_dim = v.shape[-1] // SG_GROUPS
    for gi in range(SG_GROUPS):
        sl = slice(gi * group_dim, (gi + 1) * group_dim)
        w = jnp.where(tril, w_ref[gi], 0.0).astype(bf16)
        mixed = _dot(w, vn[:, sl]) + bt[:, gi:gi + 1]
        o_ref[:, sl] = (u_ref[:, sl].astype(f32) * mixed).astype(o_ref.dtype)


def _sgu_prompt(uv, norm_g, w_s, b_s, name):
    m, two_w = uv.shape
    width = two_w // 2
    return pl.pallas_call(
        _sgu_prompt_kernel,
        grid=(m // SG_CHUNK,),
        in_specs=[pl.BlockSpec((SG_CHUNK, width), lambda i: (i, 0)),
                  pl.BlockSpec((SG_CHUNK, width), lambda i: (i, 1)),
                  pl.BlockSpec((1, width), lambda i: (0, 0)),
                  pl.BlockSpec((SG_GROUPS, SG_CHUNK, SG_CHUNK), lambda i: (0, 0, 0)),
                  pl.BlockSpec((SG_CHUNK, SG_GROUPS), lambda i: (0, 0))],
        out_specs=pl.BlockSpec((SG_CHUNK, width), lambda i: (i, 0)),
        out_shape=jax.ShapeDtypeStruct((m, width), bf16),
        compiler_params=_params("parallel"),
        name=name,
    )(uv, uv, norm_g.reshape(1, width), w_s, b_s.T)


def _sgu_sample_kernel(u_ref, v_ref, g_ref, w0_ref, b0_ref, o_ref, vn_ref):
    v = v_ref[...]
    vn = v * lax.rsqrt(jnp.mean(v * v, axis=-1, keepdims=True) + EPS) * g_ref[...]
    vn_ref[...] = vn
    o_ref[...] = (u_ref[...] * (vn * w0_ref[...] + b0_ref[...])).astype(o_ref.dtype)


def _sgu_sample(uv, norm_g, w_s, b_s, name):
    m, two_w = uv.shape
    width = two_w // 2
    group_dim = width // SG_GROUPS
    w0 = jnp.repeat(w_s[:, 0, 0], group_dim).reshape(1, width)
    b0 = jnp.repeat(b_s[:, 0], group_dim).reshape(1, width)
    row = pl.BlockSpec((1, width), lambda i: (0, 0))
    return pl.pallas_call(
        _sgu_sample_kernel,
        grid=(1,),
        in_specs=[pl.BlockSpec((m, width), lambda i: (0, 0)), pl.BlockSpec((m, width), lambda i: (0, 1)),
                  row, row, row],
        out_specs=[pl.BlockSpec((m, width), lambda i: (0, 0))] * 2,
        out_shape=[jax.ShapeDtypeStruct((m, width), bf16), jax.ShapeDtypeStruct((m, width), f32)],
        compiler_params=_params("arbitrary"),
        name=name,
    )(uv, uv, norm_g.reshape(1, width), w0, b0)


CONV_HALO = SUBLANES
FFN_SUB_ROWS = 256


def _ffn_up_kernel(xp_ref, xs_ref, wa_ref, wb_ref, cwa_ref, cwb_ref, cba_ref, cbb_ref,
                   h0a_ref, h0b_ref, h1a_ref, h1b_ref,
                   act_ref, ta_ref, tb_ref, acts_ref, upa_ref, upb_ref,
                   wabf_ref, wbbf_ref, bufa, bufb, *, tm, tiles_per_seq, n_m):
    mi = pl.program_id(1)

    @pl.when(mi == 0)
    def _():
        _cast_weight_tile(wa_ref, None, wabf_ref, 0)
        _cast_weight_tile(wb_ref, None, wbbf_ref, 0)

    @pl.when(mi % tiles_per_seq == 0)
    def _():
        bufa[0:CONV_HALO, :] = jnp.zeros((CONV_HALO, bufa.shape[1]), f32)
        bufb[0:CONV_HALO, :] = jnp.zeros((CONV_HALO, bufb.shape[1]), f32)

    halves_of = ((wabf_ref, cwa_ref, cba_ref, bufa), (wbbf_ref, cwb_ref, cbb_ref, bufb))
    sub = _tile(tm, (FFN_SUB_ROWS, 128))
    for r0 in range(0, tm, sub):
        x = xp_ref[r0:r0 + sub, :]
        halves = []
        for wbf_ref, cw_ref, cb_ref, buf in halves_of:
            up = _dot(x, wbf_ref[...])
            lo = CONV_HALO + r0
            buf[lo:lo + sub, :] = up
            conv = cb_ref[...] + buf[lo - 2:lo - 2 + sub, :] * cw_ref[0:1, :]
            conv = conv + buf[lo - 1:lo - 1 + sub, :] * cw_ref[1:2, :]
            halves.append(conv + up * cw_ref[2:3, :])
        act_ref[r0:r0 + sub, :] = (_silu(halves[0]) * halves[1]).astype(act_ref.dtype)
    for buf, tail_ref in ((bufa, ta_ref), (bufb, tb_ref)):
        tail_ref[0] = buf[CONV_HALO + tm - (CONV_W - 1):CONV_HALO + tm, :]
        buf[0:CONV_HALO, :] = buf[tm:tm + CONV_HALO, :]

    @pl.when(mi == n_m - 1)
    def _():
        xs = xs_ref[...]
        halves = []
        for wbf_ref, cw_ref, cb_ref, h0_ref, h1_ref, up_ref in (
                (wabf_ref, cwa_ref, cba_ref, h0a_ref, h1a_ref, upa_ref),
                (wbbf_ref, cwb_ref, cbb_ref, h0b_ref, h1b_ref, upb_ref)):
            up = _dot(xs, wbf_ref[...])
            up_ref[...] = up
            conv = cb_ref[...] + h0_ref[...] * cw_ref[0:1, :]
            conv = conv + h1_ref[...] * cw_ref[1:2, :]
            halves.append(conv + up * cw_ref[2:3, :])
        acts_ref[...] = (_silu(halves[0]) * halves[1]).astype(acts_ref.dtype)


def _ffn_up(hn_p, hn_s, w_up, conv_w, conv_b, hist0, hist1, batch, seq, name):
    mp, d = hn_p.shape
    ms = hn_s.shape[0]
    d_ff = w_up.shape[1] // 2
    tm = _tile(seq, (1024, 512, 256, 128))
    tn = _tile(d_ff, (256, 128))
    nb = d_ff // tn
    n_m = mp // tm
    tiles_per_seq = seq // tm
    wspec = lambda off: pl.BlockSpec((d, tn), lambda j, i, off=off: (0, j + off))
    cwspec = lambda off: pl.BlockSpec((CONV_W, tn), lambda j, i, off=off: (0, j + off))
    cbspec = lambda off: pl.BlockSpec((1, tn), lambda j, i, off=off: (0, j + off))
    srow = lambda off: pl.BlockSpec((ms, tn), lambda j, i, off=off: (0, j + off))
    tail = pl.BlockSpec((1, CONV_W - 1, tn), lambda j, i: (i // tiles_per_seq, 0, j))
    tail_shape = jax.ShapeDtypeStruct((batch, CONV_W - 1, d_ff), f32)
    up_shape = jax.ShapeDtypeStruct((ms, d_ff), f32)
    return pl.pallas_call(
        functools.partial(_ffn_up_kernel, tm=tm, tiles_per_seq=tiles_per_seq, n_m=n_m),
        grid=(nb, n_m),
        in_specs=[pl.BlockSpec((tm, d), lambda j, i: (i, 0)), pl.BlockSpec((ms, d), lambda j, i: (0, 0)),
                  wspec(0), wspec(nb), cwspec(0), cwspec(nb), cbspec(0), cbspec(nb),
                  srow(0), srow(nb), srow(0), srow(nb)],
        out_specs=[pl.BlockSpec((tm, tn), lambda j, i: (i, j)), tail, tail, srow(0), srow(0), srow(0)],
        out_shape=[jax.ShapeDtypeStruct((mp, d_ff), bf16), tail_shape, tail_shape,
                   jax.ShapeDtypeStruct((ms, d_ff), bf16), up_shape, up_shape],
        scratch_shapes=[pltpu.VMEM((d, tn), bf16), pltpu.VMEM((d, tn), bf16),
                        pltpu.VMEM((CONV_HALO + tm, tn), f32), pltpu.VMEM((CONV_HALO + tm, tn), f32)],
        compiler_params=_params("parallel", "arbitrary"),
        name=name,
    )(hn_p, hn_s, w_up, w_up, conv_w, conv_w, conv_b, conv_b, hist0, hist0, hist1, hist1)


def kernel(x_prompt, x_sample, cache_k_cmp, cache_v_cmp, cache_k_slc, cache_v_slc, cache_k_win, cache_v_win,
           state_ffn_conv, page_table, norm1_g, w_in, cmp_pool_k, cmp_bias_k, cmp_w_k, cmp_pool_v, cmp_bias_v,
           cmp_w_v, sg_norm_g, sg_w, sg_b, w_proj_a, w_proj_b, w_out, norm2_g, w_up, conv_w, conv_b, w_down,
           norm_f_g):
    batch, seq, d_model = x_prompt.shape
    n_dec, dec_seq, _ = x_sample.shape
    depth = w_in.shape[0]
    page = cache_k_cmp.shape[2]
    past = page_table.shape[1] * page
    assert dec_seq == 1 and cache_k_win.shape[2] == WINDOW and past >= WINDOW
    assert seq % SG_CHUNK == 0 and seq % L_SEL == 0 and past % L_SEL == 0
    nblk_p = max(seq // L_SEL, N_SEL)
    nblk_s = -(-max(-(-(past + 1) // L_SEL), N_SEL) // SUBLANES) * SUBLANES
    gate_tn = GROUP_W
    uv_block0 = QKV_W // gate_tn
    gm_block0 = uv_block0 + d_model // gate_tn

    tables_p = _rope_tables(jnp.arange(seq), GROUP)
    tables_s_kv = _rope_tables(jnp.full((n_dec,), past), GROUP)
    tables_s_q = _rope_tables(jnp.full((SUBLANES,), past), 1)

    hp = x_prompt.reshape(batch * seq, d_model)
    hs = x_sample.reshape(n_dec, d_model)
    new_p = [[] for _ in range(7)]
    new_s = [[] for _ in range(8)]
    for l in range(depth):
        pwk, bk, pwv, bv = cmp_pool_k[l], cmp_bias_k[l], cmp_pool_v[l], cmp_bias_v[l]
        cw = conv_w[l]
        cb = conv_b[l].reshape(1, -1)

        xn_p = _rmsnorm(hp, norm1_g[l], bf16, "norm1_p")
        xn_s = _rmsnorm(hs, norm1_g[l], bf16, "norm1_s")
        z1_p, z1_s = _matmul_wres(xn_p, xn_s, w_in[l], Z1_W, (f32, f32), "in_proj_qkv")
        uv_p, uv_s = _matmul_wres(xn_p, xn_s, w_in[l], d_model, (bf16, f32), "in_proj_uv",
                                  col_block0=uv_block0, shift=UV_SHIFT, epilogue=_gelu_tanh)
        gm_p, gm_s = _matmul_wres(xn_p, xn_s, w_in[l], 2 * d_model, (bf16, bf16), "in_proj_gate",
                                  col_block0=gm_block0, shift=UV_SHIFT, epilogue=_sigmoid)

        kc_p, vc_p, ks_p, vs_p, kw_p, vw_p = _prep_kv(z1_p, tables_p, seq, "prep_p")
        ak, av = _compress_prompt(kc_p, vc_p, pwk, bk, pwv, bv, batch, seq, "compress_p")
        ocmp, sel = _cmp_select_prompt(z1_p, ak, av, cmp_w_k[l], cmp_w_v[l], batch=batch, seq=seq, nblk=nblk_p,
                                       name="cmp_select_p")
        gates = z1_p[:, QKV_W:QKV_W + NSA_GATE_W].reshape(batch * seq, N_KV_HEADS, 3 * GROUP).transpose(1, 0, 2)
        nsa_p = _nsa_prompt(z1_p, tables_p, ks_p, vs_p, kw_p, vw_p, sel, ocmp, gates, batch=batch, seq=seq,
                            nblk=nblk_p, name="nsa_p")
        sg_p = _sgu_prompt(uv_p, sg_norm_g[l], sg_w[l], sg_b[l], "sgu_p")

        kc_s, vc_s, ks_s, vs_s, kw_s, vw_s = _prep_kv(z1_s, tables_s_kv, n_dec, "prep_s")
        ak, av = _compress_paged(cache_k_cmp[l], cache_v_cmp[l], page_table, pwk, bk, pwv, bv, "compress_s")
        heads_as_rows = lambda t, w: jnp.pad(t.reshape(n_dec, N_KV_HEADS, GROUP, w),
                                             ((0, 0), (0, 0), (0, SUBLANES - GROUP), (0, HEAD_DIM - w)))
        q_s = heads_as_rows(z1_s[:, :Q_W], HEAD_DIM)
        ocmp, imp = _cmp_sample(q_s, ak, av, cmp_w_k[l], cmp_w_v[l], n_batch=n_dec, nrows=past // STRIDE,
                                nblk=nblk_s, pos=past, name="cmp_s")
        imp_t = imp.transpose(1, 0, 2).reshape(nblk_s, n_dec * SUBLANES)
        idx = _rank_sample(imp_t, nblk=nblk_s, qb=past // L_SEL, name="rank_s")
        idx = idx.T.reshape(n_dec, SUBLANES, 2 * N_SEL)[:, :N_KV_HEADS].reshape(-1)
        kv3 = lambda t: t.reshape(n_dec, 1, KV_W)
        win_k = jnp.concatenate([cache_k_win[l].reshape(n_dec, WINDOW, KV_W)[:, 1:], kv3(kw_s)], axis=1)
        win_v = jnp.concatenate([cache_v_win[l].reshape(n_dec, WINDOW, KV_W)[:, 1:], kv3(vw_s)], axis=1)
        gates = heads_as_rows(z1_s[:, QKV_W:QKV_W + NSA_GATE_W], 3)
        nsa_s = _nsa_sample(idx, page_table, q_s, tables_s_q, kv3(ks_s), kv3(vs_s), win_k, win_v, ocmp, gates,
                            cache_k_slc[l], cache_v_slc[l], pos=past, name="nsa_s")
        nsa_s = nsa_s[:, :, :GROUP].reshape(n_dec, Q_W).astype(bf16)
        sg_s, sg_v = _sgu_sample(uv_s, sg_norm_g[l], sg_w[l], sg_b[l], "sgu_s")

        m_p, m_s = _merge(nsa_p, nsa_s, sg_p, sg_s, w_proj_a[l], w_proj_b[l], gm_p, gm_s, "merge")
        h_p, h_s = _matmul_wres(m_p, m_s, w_out[l], d_model, (f32, f32), "out_proj",
                                epilogue=lambda acc, x: x + acc, extras_p=(hp,), extras_s=(hs,))
        hn_p = _rmsnorm(h_p, norm2_g[l], bf16, "norm2_p")
        hn_s = _rmsnorm(h_s, norm2_g[l], bf16, "norm2_s")
        hist = state_ffn_conv[l]
        act_p, tail_a, tail_b, act_s, up_a, up_b = _ffn_up(hn_p, hn_s, w_up[l], cw, cb, hist[:, 0], hist[:, 1],
                                                           batch, seq, "ffn_up")
        wdown = w_down[l].astype(bf16)
        hp = _matmul_residual(act_p, wdown, h_p, "ffn_down_p")
        hs = _matmul_residual(act_s, wdown, h_s, "ffn_down_s")

        kv5 = lambda t: t.reshape(batch, seq, N_KV_HEADS, HEAD_DIM)
        keep_p = min(WINDOW, seq)
        for lst, val in zip(new_p, (kv5(kc_p), kv5(vc_p), kv5(ks_p), kv5(vs_p), kv5(kw_p)[:, seq - keep_p:],
                                    kv5(vw_p)[:, seq - keep_p:], jnp.concatenate([tail_a, tail_b], axis=-1))):
            lst.append(val)
        kv5 = lambda t: t.reshape(n_dec, 1, N_KV_HEADS, HEAD_DIM)
        conv_s = jnp.stack([hist[:, 1], jnp.concatenate([up_a, up_b], axis=-1)], axis=1)
        for lst, val in zip(new_s, (kv5(kc_s), kv5(vc_s), kv5(ks_s), kv5(vs_s),
                                    win_k.reshape(n_dec, WINDOW, N_KV_HEADS, HEAD_DIM),
                                    win_v.reshape(n_dec, WINDOW, N_KV_HEADS, HEAD_DIM),
                                    sg_v.reshape(n_dec, 1, -1), conv_s)):
            lst.append(val)

    y_prompt = _rmsnorm(hp, norm_f_g, f32, "norm_f_p").reshape(batch, seq, d_model)
    y_sample = _rmsnorm(hs, norm_f_g, f32, "norm_f_s").reshape(n_dec, 1, d_model)
    return (y_prompt, y_sample, *(jnp.stack(v) for v in new_p), *(jnp.stack(v) for v in new_s))
```

```python
import functools

import jax
import jax.numpy as jnp
from jax import lax
from jax.experimental import pallas as pl
from jax.experimental.pallas import tpu as pltpu

f32 = jnp.float32
bf16 = jnp.bfloat16
HIGHEST = lax.Precision.HIGHEST

N_HEADS = 16
N_KV_HEADS = 4
GROUP = N_HEADS // N_KV_HEADS
HEAD_DIM = 128
ROT_DIM = HEAD_DIM // 4
ROT_HALF = ROT_DIM // 2
ROPE_THETA = 500000.0
L_CMP = 32
STRIDE = 16
L_SEL = 64
N_SEL = 16
WINDOW = 512
FORCE_BONUS = 1000.0
SCALE = HEAD_DIM ** -0.5
NEG_INF = -1e30
SG_GROUPS = 16
SG_CHUNK = 128
CONV_W = 3
EPS = 1e-6

SUBLANES = 8
LANES = 128
GROUP_W = GROUP * HEAD_DIM
Q_W = N_HEADS * HEAD_DIM
KV_W = N_KV_HEADS * HEAD_DIM
NSA_GATE_W = 3 * N_HEADS
QKV_W = Q_W + 6 * KV_W
Z1_W = QKV_W + GROUP_W
COL_KC, COL_VC, COL_KS, COL_VS, COL_KW, COL_VW, COL_GATE = (Q_W // GROUP_W + i for i in range(7))

VMEM_LIMIT_BYTES = 56 * 1024 * 1024


def _params(*semantics):
    return pltpu.CompilerParams(dimension_semantics=semantics, vmem_limit_bytes=VMEM_LIMIT_BYTES)


def _tile(n, candidates):
    for c in candidates:
        if n % c == 0:
            return c
    return n


def _sigmoid(x):
    return 1.0 / (1.0 + jnp.exp(-x))


def _silu(x):
    return x * _sigmoid(x)


def _gelu_tanh(x):
    return x * (0.5 * (1.0 + jnp.tanh(0.7978845608028654 * (x + 0.044715 * (x * x * x)))))


def _dot(a, b, precision=None):
    return jnp.dot(a, b, preferred_element_type=f32, precision=precision)


def _dot_nt(a, b, precision=None):
    return lax.dot_general(a, b, (((1,), (1,)), ((), ())), precision=precision, preferred_element_type=f32)


def _rope(x, c, sa, sb):
    lanes = x.shape[-1]
    return x * c + pltpu.roll(x, lanes - ROT_HALF, 1) * sa + pltpu.roll(x, ROT_HALF, 1) * sb


def _rope_tables(pos, reps):
    n = pos.shape[0]
    inv_freq = jnp.power(jnp.float32(ROPE_THETA), -jnp.arange(ROT_HALF, dtype=f32) / ROT_HALF)
    ang = pos.astype(f32)[:, None] * inv_freq[None, :]
    cos, sin = jnp.cos(ang), jnp.sin(ang)
    zeros = lambda w: jnp.zeros((n, w), f32)
    c = jnp.concatenate([cos, cos, jnp.ones((n, HEAD_DIM - ROT_DIM), f32)], axis=1)
    sa = jnp.concatenate([-sin, zeros(HEAD_DIM - ROT_HALF)], axis=1)
    sb = jnp.concatenate([zeros(ROT_HALF), sin, zeros(HEAD_DIM - ROT_DIM)], axis=1)
    return tuple(jnp.tile(t, (1, reps)) for t in (c, sa, sb))


def _rmsnorm_kernel(x_ref, g_ref, o_ref):
    x = x_ref[...].astype(f32)
    y = x * lax.rsqrt(jnp.mean(x * x, axis=-1, keepdims=True) + EPS)
    o_ref[...] = (y * g_ref[...]).astype(o_ref.dtype)


def _rmsnorm(x, g, out_dtype, name):
    m, d = x.shape
    tm = _tile(m, (256, 128, 64, 32, 16, 8))
    return pl.pallas_call(
        _rmsnorm_kernel,
        grid=(m // tm,),
        in_specs=[pl.BlockSpec((tm, d), lambda i: (i, 0)), pl.BlockSpec((1, d), lambda i: (0, 0))],
        out_specs=pl.BlockSpec((tm, d), lambda i: (i, 0)),
        out_shape=jax.ShapeDtypeStruct((m, d), out_dtype),
        compiler_params=_params("parallel"),
        name=name,
    )(x, g.reshape(1, d).astype(f32))


CAST_ROWS = 128


def _cast_weight_tile(w_ref, wbf_ref):
    rows_total = w_ref.shape[0]
    step = _tile(rows_total, (CAST_ROWS,))
    for r0 in range(0, rows_total, step):
        wbf_ref[r0:r0 + step, :] = w_ref[r0:r0 + step, :].astype(bf16)


def _mm_wres_kernel(ap_ref, as_ref, w_ref, *rest, epilogue, n_extra, n_m, w_is_nk):
    extras_p, extras_s = rest[:n_extra], rest[n_extra:2 * n_extra]
    op_ref, os_ref, wbf_ref = rest[2 * n_extra:]
    mi = pl.program_id(1)
    mm = _dot_nt if w_is_nk else _dot

    @pl.when(mi == 0)
    def _():
        _cast_weight_tile(w_ref, wbf_ref)

    w = wbf_ref[...]
    op_ref[...] = epilogue(mm(ap_ref[...], w), *[e[...] for e in extras_p]).astype(op_ref.dtype)

    @pl.when(mi == n_m - 1)
    def _():
        os_ref[...] = epilogue(mm(as_ref[...], w), *[e[...] for e in extras_s]).astype(os_ref.dtype)


def _matmul_wres(a_p, a_s, w, n_cols, out_dtypes, name, *, w_is_nk=False, col0=0, epilogue=lambda acc: acc,
                 extras_p=(), extras_s=()):
    mp, k = a_p.shape
    ms = a_s.shape[0]
    tm = _tile(mp, (1024, 512, 256, 128))
    tn = _tile(n_cols, (512, 256, 128))
    n_m, n_n = mp // tm, n_cols // tn
    if w_is_nk:
        assert col0 % SUBLANES == 0
        w_spec = pl.BlockSpec((pl.Element(tn), pl.Element(k)),
                              lambda j, i: (pl.multiple_of(col0 + j * tn, SUBLANES), 0))
        wbf_shape = (tn, k)
    else:
        assert col0 % tn == 0
        w_spec = pl.BlockSpec((k, tn), lambda j, i: (0, col0 // tn + j))
        wbf_shape = (k, tn)
    in_specs = [pl.BlockSpec((tm, k), lambda j, i: (i, 0)), pl.BlockSpec((ms, k), lambda j, i: (0, 0)), w_spec]
    in_specs += [pl.BlockSpec((tm, tn), lambda j, i: (i, j)) for _ in extras_p]
    in_specs += [pl.BlockSpec((ms, tn), lambda j, i: (0, j)) for _ in extras_s]
    return pl.pallas_call(
        functools.partial(_mm_wres_kernel, epilogue=epilogue, n_extra=len(extras_p), n_m=n_m, w_is_nk=w_is_nk),
        grid=(n_n, n_m),
        in_specs=in_specs,
        out_specs=[pl.BlockSpec((tm, tn), lambda j, i: (i, j)), pl.BlockSpec((ms, tn), lambda j, i: (0, j))],
        out_shape=[jax.ShapeDtypeStruct((mp, n_cols), out_dtypes[0]),
                   jax.ShapeDtypeStruct((ms, n_cols), out_dtypes[1])],
        scratch_shapes=[pltpu.VMEM(wbf_shape, bf16)],
        compiler_params=_params("parallel", "arbitrary"),
        name=name,
    )(a_p, a_s, w, *extras_p, *extras_s)


def _merge_kernel(ap_ref, as_ref, bp_ref, bs_ref, wa_ref, wb_ref, gap_ref, gbp_ref, gas_ref, gbs_ref,
                  op_ref, os_ref, wabf_ref, wbbf_ref, *, n_m):
    mi = pl.program_id(1)

    @pl.when(mi == 0)
    def _():
        _cast_weight_tile(wa_ref, wabf_ref)
        _cast_weight_tile(wb_ref, wbbf_ref)

    def mix(a_ref, b_ref, ga_ref, gb_ref, o_ref):
        pa = _dot(a_ref[...], wabf_ref[...])
        pb = _dot(b_ref[...], wbbf_ref[...])
        o_ref[...] = (ga_ref[...].astype(f32) * pa + gb_ref[...].astype(f32) * pb).astype(o_ref.dtype)

    mix(ap_ref, bp_ref, gap_ref, gbp_ref, op_ref)

    @pl.when(mi == n_m - 1)
    def _():
        mix(as_ref, bs_ref, gas_ref, gbs_ref, os_ref)


def _merge(nsa_p, nsa_s, sg_p, sg_s, w_pa, w_pb, gates_p, gates_s, name):
    mp, ka = nsa_p.shape
    ms = nsa_s.shape[0]
    kb = sg_p.shape[1]
    d = w_pa.shape[1]
    tm = _tile(mp, (1024, 512, 256, 128))
    tn = _tile(d, (512, 256, 128))
    n_m, n_n = mp // tm, d // tn
    row_p = lambda w: pl.BlockSpec((tm, w), lambda j, i: (i, 0))
    row_s = lambda w: pl.BlockSpec((ms, w), lambda j, i: (0, 0))
    return pl.pallas_call(
        functools.partial(_merge_kernel, n_m=n_m),
        grid=(n_n, n_m),
        in_specs=[row_p(ka), row_s(ka), row_p(kb), row_s(kb),
                  pl.BlockSpec((ka, tn), lambda j, i: (0, j)), pl.BlockSpec((kb, tn), lambda j, i: (0, j)),
                  pl.BlockSpec((tm, tn), lambda j, i: (i, j)), pl.BlockSpec((tm, tn), lambda j, i: (i, j + n_n)),
                  pl.BlockSpec((ms, tn), lambda j, i: (0, j)), pl.BlockSpec((ms, tn), lambda j, i: (0, j + n_n))],
        out_specs=[pl.BlockSpec((tm, tn), lambda j, i: (i, j)), pl.BlockSpec((ms, tn), lambda j, i: (0, j))],
        out_shape=[jax.ShapeDtypeStruct((mp, d), bf16), jax.ShapeDtypeStruct((ms, d), bf16)],
        scratch_shapes=[pltpu.VMEM((ka, tn), bf16), pltpu.VMEM((kb, tn), bf16)],
        compiler_params=_params("parallel", "arbitrary"),
        name=name,
    )(nsa_p, nsa_s, sg_p, sg_s, w_pa, w_pb, gates_p, gates_p, gates_s, gates_s)


def _mm_resid_kernel(a_ref, w_ref, r_ref, o_ref):
    o_ref[...] = r_ref[...] + _dot(a_ref[...], w_ref[...])


def _matmul_residual(a, w, resid, name):
    m, k = a.shape
    n = w.shape[1]
    tm = _tile(m, (512, 256, 128, 64, 32, 16))
    tn = _tile(n, (256, 128))
    return pl.pallas_call(
        _mm_resid_kernel,
        grid=(m // tm, n // tn),
        in_specs=[pl.BlockSpec((tm, k), lambda i, j: (i, 0)), pl.BlockSpec((k, tn), lambda i, j: (0, j)),
                  pl.BlockSpec((tm, tn), lambda i, j: (i, j))],
        out_specs=pl.BlockSpec((tm, tn), lambda i, j: (i, j)),
        out_shape=jax.ShapeDtypeStruct((m, n), f32),
        compiler_params=_params("parallel", "arbitrary"),
        name=name,
    )(a, w, resid)


def _prep_kernel(kc, vc, ks, vs, kw, vw, c_ref, sa_ref, sb_ref, okc, ovc, oks, ovs, okw, ovw,
                 bks, bvs, bkw, bvw, *, tr):
    c, sa, sb = c_ref[...], sa_ref[...], sb_ref[...]
    values = (kc[...], vc[...], _rope(ks[...], c, sa, sb), vs[...], _rope(kw[...], c, sa, sb), vw[...])
    for val, o_ref in zip(values, (okc, ovc, oks, ovs, okw, ovw)):
        for g in range(N_KV_HEADS):
            o_ref[pl.ds(g, tr, stride=N_KV_HEADS), :] = val[:, g * HEAD_DIM:(g + 1) * HEAD_DIM]
    for val, o_ref in zip(values[2:], (bks, bvs, bkw, bvw)):
        o_ref[...] = val.astype(bf16)


def _prep_kv(z1, tables, rows_per_seq, name):
    m = z1.shape[0]
    tr = _tile(rows_per_seq, (256, 128, 64, 32, 16, 8))
    n_tab = tables[0].shape[0] // tr
    col = lambda cb: pl.BlockSpec((tr, KV_W), lambda i, cb=cb: (i, cb))
    tab = pl.BlockSpec((tr, KV_W), lambda i: (i % n_tab, 0))
    out = pl.BlockSpec((tr * N_KV_HEADS, HEAD_DIM), lambda i: (i, 0))
    out_b = pl.BlockSpec((tr, KV_W), lambda i: (i, 0))
    return pl.pallas_call(
        functools.partial(_prep_kernel, tr=tr),
        grid=(m // tr,),
        in_specs=[col(COL_KC), col(COL_VC), col(COL_KS), col(COL_VS), col(COL_KW), col(COL_VW), tab, tab, tab],
        out_specs=[out] * 6 + [out_b] * 4,
        out_shape=[jax.ShapeDtypeStruct((m * N_KV_HEADS, HEAD_DIM), f32)] * 6
        + [jax.ShapeDtypeStruct((m, KV_W), bf16)] * 4,
        compiler_params=_params("parallel"),
        name=name,
    )(z1, z1, z1, z1, z1, z1, *tables)


def _compress_prompt_kernel(k_ref, v_ref, pwk_ref, bk_ref, pwv_ref, bv_ref, ak_ref, av_ref, *, nsub):
    for x_ref, pw_ref, b_ref, o_ref in ((k_ref, pwk_ref, bk_ref, ak_ref), (v_ref, pwv_ref, bv_ref, av_ref)):
        for g in range(N_KV_HEADS):
            first = None
            second = None
            for j in range(STRIDE):
                rows = x_ref[pl.ds(j * N_KV_HEADS + g, nsub, stride=STRIDE * N_KV_HEADS), :]
                fa = rows * pw_ref[j:j + 1, :]
                sa = rows * pw_ref[STRIDE + j:STRIDE + j + 1, :]
                first = fa if first is None else first + fa
                second = sa if second is None else second + sa
            o_ref[0, :, g * HEAD_DIM:(g + 1) * HEAD_DIM] = _silu(pltpu.roll(first, 1, 0) + second + b_ref[...])


def _compress_prompt(kc, vc, pwk, bk, pwv, bv, batch, seq, name):
    nsub = seq // STRIDE
    xspec = pl.BlockSpec((seq * N_KV_HEADS, HEAD_DIM), lambda b: (b, 0))
    pwspec = pl.BlockSpec((L_CMP, HEAD_DIM), lambda b: (0, 0))
    bspec = pl.BlockSpec((1, HEAD_DIM), lambda b: (0, 0))
    ospec = pl.BlockSpec((1, nsub, KV_W), lambda b: (b, 0, 0))
    return pl.pallas_call(
        functools.partial(_compress_prompt_kernel, nsub=nsub),
        grid=(batch,),
        in_specs=[xspec, xspec, pwspec, bspec, pwspec, bspec],
        out_specs=[ospec, ospec],
        out_shape=[jax.ShapeDtypeStruct((batch, nsub, KV_W), f32)] * 2,
        compiler_params=_params("parallel"),
        name=name,
    )(kc, vc, pwk, bk.reshape(1, HEAD_DIM), pwv, bv.reshape(1, HEAD_DIM))


PAGES_PER_CHUNK = 16


def _compress_paged_kernel(pt_ref, wfk_ref, wsk_ref, bk_ref, wfv_ref, wsv_ref, bv_ref, poolk_ref, poolv_ref,
                           ak_ref, av_ref, bufk, bufv, sems, *, n_pages, page_rows):
    b = pl.program_id(0)
    pps = PAGES_PER_CHUNK
    n_chunks = n_pages // pps
    tiles_per_sub = STRIDE * N_KV_HEADS // SUBLANES
    nsub = page_rows // (STRIDE * N_KV_HEADS)

    def copies(chunk, slot):
        out = []
        for p in range(pps):
            phys = pt_ref[b * n_pages + chunk * pps + p]
            out.append(pltpu.make_async_copy(poolk_ref.at[phys], bufk.at[slot, p], sems.at[0, slot]))
            out.append(pltpu.make_async_copy(poolv_ref.at[phys], bufv.at[slot, p], sems.at[1, slot]))
        return out

    for cp in copies(0, 0):
        cp.start()
    carry = (jnp.zeros((SUBLANES, HEAD_DIM), f32), jnp.zeros((SUBLANES, HEAD_DIM), f32))
    for chunk in range(n_chunks):
        slot = chunk % 2
        if chunk + 1 < n_chunks:
            for cp in copies(chunk + 1, 1 - slot):
                cp.start()
        for cp in copies(chunk, slot):
            cp.wait()

        def page_body(p, prev, chunk=chunk, slot=slot):
            out_row0 = pl.multiple_of((chunk * pps + p) * nsub * SUBLANES, nsub * SUBLANES)
            new_prev = []
            for which, (buf, wf_ref, ws_ref, b_ref, o_ref) in enumerate(
                    ((bufk, wfk_ref, wsk_ref, bk_ref, ak_ref), (bufv, wfv_ref, wsv_ref, bv_ref, av_ref))):
                last = prev[which]
                for n in range(nsub):
                    first = None
                    second = None
                    for t in range(tiles_per_sub):
                        x = buf[slot, p, pl.ds((n * tiles_per_sub + t) * SUBLANES, SUBLANES), :]
                        fa = x * wf_ref[t]
                        sa = x * ws_ref[t]
                        first = fa if first is None else first + fa
                        second = sa if second is None else second + sa
                    first = first + pltpu.roll(first, SUBLANES // 2, 0)
                    second = second + pltpu.roll(second, SUBLANES // 2, 0)
                    o_ref[pl.ds(out_row0 + n * SUBLANES, SUBLANES), :] = _silu(last + second + b_ref[...])
                    last = first
                new_prev.append(last)
            return tuple(new_prev)

        carry = lax.fori_loop(0, pps, page_body, carry)


def _compress_paged(pool_k, pool_v, page_table, pwk, bk, pwv, bv, name):
    n_batch, n_pages = page_table.shape
    n_phys, page = pool_k.shape[:2]
    assert N_KV_HEADS * 2 == SUBLANES and n_pages % PAGES_PER_CHUNK == 0 and page % STRIDE == 0
    page_rows = page * N_KV_HEADS
    out_rows = n_pages * (page // STRIDE) * SUBLANES

    def tile_weights(pw):
        return jnp.repeat(pw.reshape(STRIDE // 2, 2, HEAD_DIM), N_KV_HEADS, axis=1)

    full3 = pl.BlockSpec((STRIDE // 2, SUBLANES, HEAD_DIM), lambda b, pt: (0, 0, 0))
    brow = pl.BlockSpec((SUBLANES, HEAD_DIM), lambda b, pt: (0, 0))
    o_spec = pl.BlockSpec((out_rows, HEAD_DIM), lambda b, pt: (b, 0))
    out_shape = jax.ShapeDtypeStruct((n_batch * out_rows, HEAD_DIM), f32)
    buf = pltpu.VMEM((2, PAGES_PER_CHUNK, page_rows, HEAD_DIM), f32)
    return pl.pallas_call(
        functools.partial(_compress_paged_kernel, n_pages=n_pages, page_rows=page_rows),
        grid_spec=pltpu.PrefetchScalarGridSpec(
            num_scalar_prefetch=1,
            grid=(n_batch,),
            in_specs=[full3, full3, brow, full3, full3, brow,
                      pl.BlockSpec(memory_space=pl.ANY), pl.BlockSpec(memory_space=pl.ANY)],
            out_specs=[o_spec, o_spec],
            scratch_shapes=[buf, buf, pltpu.SemaphoreType.DMA((2, 2))],
        ),
        out_shape=[out_shape, out_shape],
        compiler_params=_params("arbitrary"),
        name=name,
    )(page_table.reshape(-1),
      tile_weights(pwk[:STRIDE]), tile_weights(pwk[STRIDE:]), jnp.tile(bk.reshape(1, HEAD_DIM), (SUBLANES, 1)),
      tile_weights(pwv[:STRIDE]), tile_weights(pwv[STRIDE:]), jnp.tile(bv.reshape(1, HEAD_DIM), (SUBLANES, 1)),
      pool_k.reshape(n_phys, page_rows, HEAD_DIM), pool_v.reshape(n_phys, page_rows, HEAD_DIM))


def _block_overlap(blk, rown):
    c_start = (rown - 1) * STRIDE
    return (rown >= 1) & (c_start < blk * L_SEL + L_SEL) & (c_start + L_CMP > blk * L_SEL)


def _stable_rank(score, score_ref, blk, nblk):
    score_ref[...] = score

    def count_better(i, cnt):
        other = score_ref[pl.ds(i, 1), :]
        better = (other > score) | ((other == score) & (i < blk))
        return cnt + better.astype(f32)

    return lax.fori_loop(0, nblk, count_better, jnp.zeros(score.shape, f32))


def _cmp_select_prompt_kernel(q_ref, ak_ref, av_ref, wk_ref, wv_ref, ocmp_ref, sel_ref, score_ref, *,
                              tq, nrows, nblk):
    pos0 = pl.program_id(2) * tq
    pos_col = pos0 + lax.broadcasted_iota(jnp.int32, (tq, 1), 0)
    pos_row = pos0 + lax.broadcasted_iota(jnp.int32, (1, tq), 1)

    ck = _dot(ak_ref[0], wk_ref[...], precision=HIGHEST)
    cv = _dot(av_ref[0].astype(bf16), wv_ref[...].astype(bf16)).astype(bf16)

    rown = lax.broadcasted_iota(jnp.int32, (1, nrows), 1)
    visible = (rown >= 1) & (rown * STRIDE + (L_CMP - STRIDE - 1) <= pos_col)

    pg = jnp.zeros((tq, nrows), f32)
    for r in range(GROUP):
        sl = slice(r * HEAD_DIM, (r + 1) * HEAD_DIM)
        s = _dot_nt(q_ref[:, sl], ck, precision=HIGHEST) * SCALE
        s = jnp.where(visible, s, NEG_INF)
        e = jnp.where(visible, jnp.exp(s - jnp.max(s, axis=-1, keepdims=True)), 0.0)
        den = jnp.sum(e, axis=-1, keepdims=True)
        p = jnp.where(den > 0.0, e / jnp.where(den > 0.0, den, 1.0), 0.0)
        ocmp_ref[:, sl] = _dot(p.astype(bf16), cv)
        pg = pg + p

    blk = lax.broadcasted_iota(jnp.int32, (nblk, 1), 0)
    imp_t = _dot_nt(_block_overlap(blk, rown).astype(f32), pg, precision=HIGHEST)
    qb = pos_row // L_SEL
    causal = blk <= qb
    forced = (blk == 0) | (blk == qb) | (blk == qb - 1)
    score = jnp.where(causal, imp_t + FORCE_BONUS * forced.astype(f32), -jnp.inf)
    rank = _stable_rank(score, score_ref, blk, nblk)
    chosen = jnp.where(causal & (rank < float(N_SEL)), 1.0, 0.0).astype(bf16)
    eye = (lax.broadcasted_iota(jnp.int32, (tq, tq), 0)
           == lax.broadcasted_iota(jnp.int32, (tq, tq), 1)).astype(bf16)
    sel_ref[0, 0] = _dot_nt(eye, chosen)


def _cmp_select_prompt(q, ak, av, wk, wv, *, batch, seq, nblk, name):
    nrows = ak.shape[1]
    tq = _tile(seq, (512, 256, 128))
    n_q = seq // tq
    q_spec = pl.BlockSpec((tq, GROUP_W), lambda b, g, i: (b * n_q + i, g))
    a_spec = pl.BlockSpec((1, nrows, HEAD_DIM), lambda b, g, i: (b, 0, g))
    w_spec = pl.BlockSpec((HEAD_DIM, HEAD_DIM), lambda b, g, i: (0, 0))
    return pl.pallas_call(
        functools.partial(_cmp_select_prompt_kernel, tq=tq, nrows=nrows, nblk=nblk),
        grid=(batch, N_KV_HEADS, n_q),
        in_specs=[q_spec, a_spec, a_spec, w_spec, w_spec],
        out_specs=[pl.BlockSpec((tq, GROUP_W), lambda b, g, i: (b * n_q + i, g)),
                   pl.BlockSpec((1, 1, tq, nblk), lambda b, g, i: (b, g, i, 0))],
        out_shape=[jax.ShapeDtypeStruct((batch * seq, Q_W), f32),
                   jax.ShapeDtypeStruct((batch, N_KV_HEADS, seq, nblk), f32)],
        scratch_shapes=[pltpu.VMEM((nblk, tq), f32)],
        compiler_params=_params("parallel", "parallel", "arbitrary"),
        name=name,
    )(q, ak, av, wk, wv)


def _cmp_sample_kernel(q_ref, ak_ref, av_ref, wk_ref, wv_ref, ocmp_ref, imp_ref, *, nrows, nblk, pos):
    rown = lax.broadcasted_iota(jnp.int32, (nrows, 1), 0)
    visible = (rown >= 1) & (rown * STRIDE + (L_CMP - STRIDE - 1) <= pos)
    overlap = _block_overlap(lax.broadcasted_iota(jnp.int32, (nblk, 1), 0),
                             lax.broadcasted_iota(jnp.int32, (1, nrows), 1)).astype(f32)
    lane = lax.broadcasted_iota(jnp.int32, (1, SUBLANES), 1)
    is_head = lane < GROUP
    imp_all = jnp.zeros((nblk, SUBLANES), f32)
    for g in range(N_KV_HEADS):
        a_k = ak_ref[pl.ds(g, nrows, stride=SUBLANES), :]
        a_v = av_ref[pl.ds(g, nrows, stride=SUBLANES), :]
        ck = _dot(a_k, wk_ref[...], precision=HIGHEST)
        cv = _dot(a_v.astype(bf16), wv_ref[...].astype(bf16))
        s = _dot_nt(ck, q_ref[0, g], precision=HIGHEST) * SCALE
        s = jnp.where(visible, s, NEG_INF)
        e = jnp.where(visible, jnp.exp(s - jnp.max(s, axis=0, keepdims=True)), 0.0)
        den = jnp.sum(e, axis=0, keepdims=True)
        p = jnp.where(is_head & (den > 0.0), e / jnp.where(den > 0.0, den, 1.0), 0.0)
        for r in range(GROUP):
            ocmp_ref[0, g, r:r + 1, :] = jnp.sum(p[:, r:r + 1] * cv, axis=0, keepdims=True)
        ocmp_ref[0, g, GROUP:, :] = jnp.zeros((SUBLANES - GROUP, HEAD_DIM), f32)
        imp = jnp.sum(_dot(overlap, p, precision=HIGHEST), axis=1, keepdims=True)
        imp_all = jnp.where(lane == g, imp, imp_all)
    imp_ref[0] = imp_all


def _cmp_sample(q, ak, av, wk, wv, *, n_batch, nrows, nblk, pos, name):
    q_spec = pl.BlockSpec((1, N_KV_HEADS, SUBLANES, HEAD_DIM), lambda b: (b, 0, 0, 0))
    a_spec = pl.BlockSpec((nrows * SUBLANES, HEAD_DIM), lambda b: (b, 0))
    w_spec = pl.BlockSpec((HEAD_DIM, HEAD_DIM), lambda b: (0, 0))
    return pl.pallas_call(
        functools.partial(_cmp_sample_kernel, nrows=nrows, nblk=nblk, pos=pos),
        grid=(n_batch,),
        in_specs=[q_spec, a_spec, a_spec, w_spec, w_spec],
        out_specs=[q_spec, pl.BlockSpec((1, nblk, SUBLANES), lambda b: (b, 0, 0))],
        out_shape=[jax.ShapeDtypeStruct((n_batch, N_KV_HEADS, SUBLANES, HEAD_DIM), f32),
                   jax.ShapeDtypeStruct((n_batch, nblk, SUBLANES), f32)],
        compiler_params=_params("parallel"),
        name=name,
    )(q, ak, av, wk, wv)


def _rank_sample_kernel(imp_ref, idx_ref, score_ref, *, nblk, qb):
    blk = lax.broadcasted_iota(jnp.int32, (nblk, 1), 0)
    causal = blk <= qb
    forced = (blk == 0) | (blk == qb) | (blk == qb - 1)
    score = jnp.where(causal, imp_ref[...] + FORCE_BONUS * forced.astype(f32), -jnp.inf)
    rank = _stable_rank(score, score_ref, blk, nblk)
    for slot in range(N_SEL):
        hit = causal & (rank == float(slot))
        idx_ref[slot:slot + 1, :] = jnp.sum(jnp.where(hit, blk, 0), axis=0, keepdims=True)
        idx_ref[N_SEL + slot:N_SEL + slot + 1, :] = jnp.max(hit.astype(jnp.int32), axis=0, keepdims=True)


def _rank_sample(imp_t, *, nblk, qb, name):
    lanes = imp_t.shape[1]
    return pl.pallas_call(
        functools.partial(_rank_sample_kernel, nblk=nblk, qb=qb),
        grid=(1,),
        in_specs=[pl.BlockSpec((nblk, lanes), lambda i: (0, 0))],
        out_specs=pl.BlockSpec((2 * N_SEL, lanes), lambda i: (0, 0)),
        out_shape=jax.ShapeDtypeStruct((2 * N_SEL, lanes), jnp.int32),
        scratch_shapes=[pltpu.VMEM((nblk, lanes), f32)],
        compiler_params=_params("arbitrary"),
        name=name,
    )(imp_t)


def _softmax_pv(s, v):
    e = jnp.exp(s - jnp.max(s, axis=-1, keepdims=True))
    den = jnp.sum(e, axis=-1, keepdims=True)
    return _dot(e.astype(bf16), v) / den


def _masked_attention(q, k, v, mask, tq):
    heads = q.shape[0] // tq
    keys = k.shape[0]
    s = _dot_nt(q, k) * SCALE
    s = jnp.where(mask[None], s.reshape(heads, tq, keys), NEG_INF).reshape(heads * tq, keys)
    return _softmax_pv(s, v)


HEADS_PER_CHAIN = 2


def _nsa_prompt_kernel(q_ref, c_ref, sa_ref, sb_ref, ks_ref, vs_ref, kw_ref, vw_ref, sel_ref, ocmp_ref,
                       gate_ref, o_ref, *, tq, seq, nblk, span, n_q):
    qi = pl.program_id(2)
    t0 = qi * tq
    qr = _rope(q_ref[...], c_ref[...], sa_ref[...], sb_ref[...]).astype(bf16)
    n_chain = GROUP // HEADS_PER_CHAIN
    chains = [jnp.concatenate([qr[:, (c * HEADS_PER_CHAIN + h) * HEAD_DIM:(c * HEADS_PER_CHAIN + h + 1) * HEAD_DIM]
                               for h in range(HEADS_PER_CHAIN)], axis=0) for c in range(n_chain)]
    tpos = t0 + lax.broadcasted_iota(jnp.int32, (tq, 1), 0)

    start = pl.multiple_of(jnp.maximum(t0 - WINDOW, 0), tq)
    wpos = start + lax.broadcasted_iota(jnp.int32, (1, span), 1)
    wmask = (wpos <= tpos) & (wpos > tpos - WINDOW)
    kw = kw_ref[pl.ds(start, span), :].astype(bf16)
    vw = vw_ref[pl.ds(start, span), :].astype(bf16)
    o_win = [_masked_attention(qc, kw, vw, wmask, tq) for qc in chains]

    gates = _sigmoid(gate_ref[...])
    sel = sel_ref[0, 0].astype(bf16)

    tiles_per_class = 2
    for cls in range(-(-n_q // tiles_per_class)):
        ext = min(seq, (cls + 1) * tiles_per_class * tq)

        @pl.when(qi // tiles_per_class == cls)
        def _(ext=ext):
            kpos = lax.broadcasted_iota(jnp.int32, (1, ext), 1)
            expand = (kpos // L_SEL == lax.broadcasted_iota(jnp.int32, (nblk, 1), 0)).astype(bf16)
            mask = (_dot(sel, expand) > 0.5) & (kpos <= tpos)
            k = ks_ref[0:ext, :].astype(bf16)
            v = vs_ref[0:ext, :].astype(bf16)
            for c, qc in enumerate(chains):
                o_slc = _masked_attention(qc, k, v, mask, tq)
                for h in range(HEADS_PER_CHAIN):
                    r = c * HEADS_PER_CHAIN + h
                    sl = slice(r * HEAD_DIM, (r + 1) * HEAD_DIM)
                    rows = slice(h * tq, (h + 1) * tq)
                    o = (gates[:, 3 * r:3 * r + 1] * ocmp_ref[:, sl] + gates[:, 3 * r + 1:3 * r + 2] * o_slc[rows]
                         + gates[:, 3 * r + 2:3 * r + 3] * o_win[c][rows])
                    o_ref[:, sl] = o.astype(o_ref.dtype)


def _nsa_prompt(z1, tables, ks, vs, kw, vw, sel, ocmp, gates, *, batch, seq, nblk, name):
    tq = _tile(seq, (256, 128))
    n_q = seq // tq
    span = min(WINDOW + tq, seq)
    row_blk = lambda w: pl.BlockSpec((tq, w), lambda b, g, i: (b * n_q + i, g))
    tab = pl.BlockSpec((tq, GROUP_W), lambda b, g, i: (i, 0))
    kv = pl.BlockSpec((seq, HEAD_DIM), lambda b, g, i: (b, g))
    return pl.pallas_call(
        functools.partial(_nsa_prompt_kernel, tq=tq, seq=seq, nblk=nblk, span=span, n_q=n_q),
        grid=(batch, N_KV_HEADS, n_q),
        in_specs=[row_blk(GROUP_W), tab, tab, tab, kv, kv, kv, kv,
                  pl.BlockSpec((1, 1, tq, nblk), lambda b, g, i: (b, g, i, 0)),
                  row_blk(GROUP_W),
                  pl.BlockSpec((None, tq, 3 * GROUP), lambda b, g, i: (g, b * n_q + i, 0))],
        out_specs=row_blk(GROUP_W),
        out_shape=jax.ShapeDtypeStruct((batch * seq, Q_W), bf16),
        compiler_params=_params("parallel", "parallel", "arbitrary"),
        name=name,
    )(z1, *tables, ks, vs, kw, vw, sel, ocmp, gates)


def _nsa_sample_kernel(idx_ref, pt_ref, q_ref, c_ref, sa_ref, sb_ref, ksn_ref, vsn_ref, kwin_ref, vwin_ref,
                       ocmp_ref, gate_ref, poolk_ref, poolv_ref, o_ref, kbuf, vbuf, sems, *,
                       n_pages, page, pos, window):
    b = pl.program_id(0)
    per_page = page // L_SEL
    past_blocks = n_pages * per_page
    n_keys = N_SEL * L_SEL

    def block_copies(g, slot):
        blk = idx_ref[(b * N_KV_HEADS + g) * 2 * N_SEL + slot]
        jp = jnp.minimum(blk, past_blocks - 1)
        phys = pt_ref[b * n_pages + jp // per_page]
        off = pl.multiple_of((jp % per_page) * L_SEL, L_SEL)
        dst = pl.ds(slot * L_SEL, L_SEL)
        return (pltpu.make_async_copy(poolk_ref.at[phys, pl.ds(off, L_SEL), g], kbuf.at[g, dst], sems.at[0]),
                pltpu.make_async_copy(poolv_ref.at[phys, pl.ds(off, L_SEL), g], vbuf.at[g, dst], sems.at[1]))

    for g in range(N_KV_HEADS):
        for slot in range(N_SEL):
            for cp in block_copies(g, slot):
                cp.start()
    for g in range(N_KV_HEADS):
        for slot in range(N_SEL):
            for cp in block_copies(g, slot):
                cp.wait()

    lane = lax.broadcasted_iota(jnp.int32, (1, n_keys), 1)
    lane_slot = lane // L_SEL
    for g in range(N_KV_HEADS):
        qr = _rope(q_ref[0, g], c_ref[...], sa_ref[...], sb_ref[...]).astype(bf16)
        blk_vec = jnp.zeros((1, n_keys), jnp.int32)
        ok_vec = jnp.zeros((1, n_keys), jnp.int32)
        base = (b * N_KV_HEADS + g) * 2 * N_SEL
        for slot in range(N_SEL):
            blk_vec = jnp.where(lane_slot == slot, idx_ref[base + slot], blk_vec)
            ok_vec = jnp.where(lane_slot == slot, idx_ref[base + N_SEL + slot], ok_vec)
        in_past = blk_vec < past_blocks
        kpos = blk_vec * L_SEL + lane % L_SEL
        mask = (ok_vec > 0) & (kpos <= pos)
        k_new = ksn_ref[0, g:g + 1, :].astype(bf16).astype(f32)
        v_new = vsn_ref[0, g:g + 1, :].astype(bf16).astype(f32)
        s_new = jnp.sum(qr.astype(f32) * k_new, axis=-1, keepdims=True)
        s = jnp.where(in_past, _dot_nt(qr, kbuf[g].astype(bf16)), s_new) * SCALE
        s = jnp.where(mask, s, NEG_INF)
        e = jnp.exp(s - jnp.max(s, axis=-1, keepdims=True))
        den = jnp.sum(e, axis=-1, keepdims=True)
        e_new = jnp.sum(jnp.where(in_past, 0.0, e), axis=-1, keepdims=True)
        e_past = jnp.where(in_past, e, 0.0).astype(bf16)
        o_slc = (_dot(e_past, vbuf[g].astype(bf16)) + e_new.astype(bf16).astype(f32) * v_new) / den
        win_rows = pl.ds(g, window, stride=N_KV_HEADS)
        o_win = _softmax_pv(_dot_nt(qr, kwin_ref[win_rows, :].astype(bf16)) * SCALE,
                            vwin_ref[win_rows, :].astype(bf16))
        gt = _sigmoid(gate_ref[0, g])
        o_ref[0, g] = gt[:, 0:1] * ocmp_ref[0, g] + gt[:, 1:2] * o_slc + gt[:, 2:3] * o_win


def _nsa_sample(idx, page_table, q, tables, ks_new, vs_new, kwin, vwin, ocmp, gates, pool_k, pool_v, *,
                pos, name):
    n_batch, n_pages = page_table.shape
    page = pool_k.shape[1]
    window = kwin.shape[0] // (n_batch * N_KV_HEADS)
    tab = pl.BlockSpec((SUBLANES, HEAD_DIM), lambda b, *_: (0, 0))
    heads = pl.BlockSpec((1, N_KV_HEADS, SUBLANES, HEAD_DIM), lambda b, *_: (b, 0, 0, 0))
    new_row = pl.BlockSpec((1, N_KV_HEADS, HEAD_DIM), lambda b, *_: (b, 0, 0))
    win = pl.BlockSpec((window * N_KV_HEADS, HEAD_DIM), lambda b, *_: (b, 0))
    return pl.pallas_call(
        functools.partial(_nsa_sample_kernel, n_pages=n_pages, page=page, pos=pos, window=window),
        grid_spec=pltpu.PrefetchScalarGridSpec(
            num_scalar_prefetch=2,
            grid=(n_batch,),
            in_specs=[heads, tab, tab, tab, new_row, new_row, win, win, heads, heads,
                      pl.BlockSpec(memory_space=pl.ANY), pl.BlockSpec(memory_space=pl.ANY)],
            out_specs=heads,
            scratch_shapes=[pltpu.VMEM((N_KV_HEADS, N_SEL * L_SEL, HEAD_DIM), f32),
                            pltpu.VMEM((N_KV_HEADS, N_SEL * L_SEL, HEAD_DIM), f32),
                            pltpu.SemaphoreType.DMA((2,))],
        ),
        out_shape=jax.ShapeDtypeStruct((n_batch, N_KV_HEADS, SUBLANES, HEAD_DIM), f32),
        compiler_params=_params("arbitrary"),
        name=name,
    )(idx, page_table.reshape(-1), q, *tables, ks_new, vs_new, kwin, vwin, ocmp, gates, pool_k, pool_v)


def _window_update_kernel(kin_ref, vin_ref, knew_ref, vnew_ref, kout_ref, vout_ref, sems):
    n_batch, window = kin_ref.shape[:2]
    copies = []
    for i, (src, new, dst) in enumerate(((kin_ref, knew_ref, kout_ref), (vin_ref, vnew_ref, vout_ref))):
        copies.append(pltpu.make_async_copy(src.at[:, pl.ds(1, window - 1)], dst.at[:, pl.ds(0, window - 1)],
                                            sems.at[2 * i]))
        copies.append(pltpu.make_async_copy(new, dst.at[:, window - 1], sems.at[2 * i + 1]))
    for cp in copies:
        cp.start()
    for cp in copies:
        cp.wait()


def _window_update(win_k, win_v, k_new, v_new, name):
    any_spec = pl.BlockSpec(memory_space=pl.ANY)
    shape = jax.ShapeDtypeStruct(win_k.shape, win_k.dtype)
    return pl.pallas_call(
        _window_update_kernel,
        in_specs=[any_spec] * 4,
        out_specs=[any_spec] * 2,
        out_shape=[shape, shape],
        scratch_shapes=[pltpu.SemaphoreType.DMA((4,))],
        name=name,
    )(win_k, win_v, k_new, v_new)


def _sgu_prompt_kernel(u_ref, v_ref, g_ref, w_ref, bt_ref, o_ref):
    v = v_ref[...].astype(f32)
    vn = (v * lax.rsqrt(jnp.mean(v * v, axis=-1, keepdims=True) + EPS) * g_ref[...]).astype(bf16)
    tril = (lax.broadcasted_iota(jnp.int32, (SG_CHUNK, SG_CHUNK), 0)
            >= lax.broadcasted_iota(jnp.int32, (SG_CHUNK, SG_CHUNK), 1))
    bt = bt_ref[...]
    group_dim = v.shape[-1] // SG_GROUPS
    for gi in range(SG_GROUPS):
        sl = slice(gi * group_dim, (gi + 1) * group_dim)
        w = jnp.where(tril, w_ref[gi], 0.0).astype(bf16)
        mixed = _dot(w, vn[:, sl]) + bt[:, gi:gi + 1]
        o_ref[:, sl] = (u_ref[:, sl].astype(f32) * mixed).astype(o_ref.dtype)


def _sgu_prompt(uv, norm_g, w_s, b_s, name):
    m, two_w = uv.shape
    width = two_w // 2
    return pl.pallas_call(
        _sgu_prompt_kernel,
        grid=(m // SG_CHUNK,),
        in_specs=[pl.BlockSpec((SG_CHUNK, width), lambda i: (i, 0)),
                  pl.BlockSpec((SG_CHUNK, width), lambda i: (i, 1)),
                  pl.BlockSpec((1, width), lambda i: (0, 0)),
                  pl.BlockSpec((SG_GROUPS, SG_CHUNK, SG_CHUNK), lambda i: (0, 0, 0)),
                  pl.BlockSpec((SG_CHUNK, SG_GROUPS), lambda i: (0, 0))],
        out_specs=pl.BlockSpec((SG_CHUNK, width), lambda i: (i, 0)),
        out_shape=jax.ShapeDtypeStruct((m, width), bf16),
        compiler_params=_params("parallel"),
        name=name,
    )(uv, uv, norm_g.reshape(1, width), w_s, b_s.T)


def _sgu_sample_kernel(u_ref, v_ref, g_ref, w0_ref, b0_ref, o_ref, vn_ref):
    v = v_ref[...]
    vn = v * lax.rsqrt(jnp.mean(v * v, axis=-1, keepdims=True) + EPS) * g_ref[...]
    vn_ref[...] = vn
    o_ref[...] = (u_ref[...] * (vn * w0_ref[...] + b0_ref[...])).astype(o_ref.dtype)


def _sgu_sample(uv, norm_g, w_s, b_s, name):
    m, two_w = uv.shape
    width = two_w // 2
    group_dim = width // SG_GROUPS
    w0 = jnp.repeat(w_s[:, 0, 0], group_dim).reshape(1, width)
    b0 = jnp.repeat(b_s[:, 0], group_dim).reshape(1, width)
    row = pl.BlockSpec((1, width), lambda i: (0, 0))
    return pl.pallas_call(
        _sgu_sample_kernel,
        grid=(1,),
        in_specs=[pl.BlockSpec((m, width), lambda i: (0, 0)), pl.BlockSpec((m, width), lambda i: (0, 1)),
                  row, row, row],
        out_specs=[pl.BlockSpec((m, width), lambda i: (0, 0))] * 2,
        out_shape=[jax.ShapeDtypeStruct((m, width), bf16), jax.ShapeDtypeStruct((m, width), f32)],
        compiler_params=_params("arbitrary"),
        name=name,
    )(uv, uv, norm_g.reshape(1, width), w0, b0)


CONV_HALO = SUBLANES


def _ffn_up_kernel(xp_ref, xs_ref, wa_ref, wb_ref, cwa_ref, cwb_ref, cba_ref, cbb_ref,
                   h0a_ref, h0b_ref, h1a_ref, h1b_ref,
                   act_ref, ta_ref, tb_ref, acts_ref, upa_ref, upb_ref,
                   wabf_ref, wbbf_ref, bufa, bufb, *, tm, tiles_per_seq, n_m):
    mi = pl.program_id(1)
    lo, hi = CONV_HALO, CONV_HALO + tm

    @pl.when(mi == 0)
    def _():
        _cast_weight_tile(wa_ref, wabf_ref)
        _cast_weight_tile(wb_ref, wbbf_ref)

    @pl.when(mi % tiles_per_seq == 0)
    def _():
        bufa[0:lo, :] = jnp.zeros((lo, bufa.shape[1]), f32)
        bufb[0:lo, :] = jnp.zeros((lo, bufb.shape[1]), f32)

    x = xp_ref[...]
    halves = []
    for wbf_ref, cw_ref, cb_ref, buf, tail_ref in ((wabf_ref, cwa_ref, cba_ref, bufa, ta_ref),
                                                   (wbbf_ref, cwb_ref, cbb_ref, bufb, tb_ref)):
        up = _dot(x, wbf_ref[...])
        buf[lo:hi, :] = up
        conv = cb_ref[...] + buf[lo - 2:hi - 2, :] * cw_ref[0:1, :]
        conv = conv + buf[lo - 1:hi - 1, :] * cw_ref[1:2, :]
        halves.append(conv + up * cw_ref[2:3, :])
        tail_ref[0] = up[tm - (CONV_W - 1):, :]
        buf[0:lo, :] = up[tm - CONV_HALO:, :]
    act_ref[...] = (_silu(halves[0]) * halves[1]).astype(act_ref.dtype)

    @pl.when(mi == n_m - 1)
    def _():
        xs = xs_ref[...]
        halves = []
        for wbf_ref, cw_ref, cb_ref, h0_ref, h1_ref, up_ref in (
                (wabf_ref, cwa_ref, cba_ref, h0a_ref, h1a_ref, upa_ref),
                (wbbf_ref, cwb_ref, cbb_ref, h0b_ref, h1b_ref, upb_ref)):
            up = _dot(xs, wbf_ref[...])
            up_ref[...] = up
            conv = cb_ref[...] + h0_ref[...] * cw_ref[0:1, :]
            conv = conv + h1_ref[...] * cw_ref[1:2, :]
            halves.append(conv + up * cw_ref[2:3, :])
        acts_ref[...] = (_silu(halves[0]) * halves[1]).astype(acts_ref.dtype)


def _ffn_up(hn_p, hn_s, w_up, conv_w, conv_b, hist0, hist1, batch, seq, name):
    mp, d = hn_p.shape
    ms = hn_s.shape[0]
    d_ff = w_up.shape[1] // 2
    tm = _tile(seq, (1024, 512, 256, 128))
    tn = _tile(d_ff, (256, 128))
    nb = d_ff // tn
    n_m = mp // tm
    tiles_per_seq = seq // tm
    col = lambda shape, off: pl.BlockSpec(shape, lambda j, i: (0, j + off))
    tail = pl.BlockSpec((1, CONV_W - 1, tn), lambda j, i: (i // tiles_per_seq, 0, j))
    tail_shape = jax.ShapeDtypeStruct((batch, CONV_W - 1, d_ff), f32)
    up_shape = jax.ShapeDtypeStruct((ms, d_ff), f32)
    buf = pltpu.VMEM((CONV_HALO + tm, tn), f32)
    return pl.pallas_call(
        functools.partial(_ffn_up_kernel, tm=tm, tiles_per_seq=tiles_per_seq, n_m=n_m),
        grid=(nb, n_m),
        in_specs=[pl.BlockSpec((tm, d), lambda j, i: (i, 0)), pl.BlockSpec((ms, d), lambda j, i: (0, 0)),
                  col((d, tn), 0), col((d, tn), nb),
                  col((CONV_W, tn), 0), col((CONV_W, tn), nb), col((1, tn), 0), col((1, tn), nb),
                  col((ms, tn), 0), col((ms, tn), nb), col((ms, tn), 0), col((ms, tn), nb)],
        out_specs=[pl.BlockSpec((tm, tn), lambda j, i: (i, j)), tail, tail,
                   col((ms, tn), 0), col((ms, tn), 0), col((ms, tn), 0)],
        out_shape=[jax.ShapeDtypeStruct((mp, d_ff), bf16), tail_shape, tail_shape,
                   jax.ShapeDtypeStruct((ms, d_ff), bf16), up_shape, up_shape],
        scratch_shapes=[pltpu.VMEM((d, tn), bf16), pltpu.VMEM((d, tn), bf16), buf, buf],
        compiler_params=_params("parallel", "arbitrary"),
        name=name,
    )(hn_p, hn_s, w_up, w_up, conv_w, conv_w, conv_b, conv_b, hist0, hist0, hist1, hist1)


def kernel(x_prompt, x_sample, cache_k_cmp, cache_v_cmp, cache_k_slc, cache_v_slc, cache_k_win, cache_v_win,
           state_ffn_conv, page_table, norm1_g, w_in, cmp_pool_k, cmp_bias_k, cmp_w_k, cmp_pool_v, cmp_bias_v,
           cmp_w_v, sg_norm_g, sg_w, sg_b, w_proj_a, w_proj_b, w_out, norm2_g, w_up, conv_w, conv_b, w_down,
           norm_f_g):
    batch, seq, d_model = x_prompt.shape
    n_dec, dec_seq, _ = x_sample.shape
    depth = w_in.shape[0]
    page = cache_k_cmp.shape[2]
    past = page_table.shape[1] * page
    assert dec_seq == 1 and cache_k_win.shape[2] == WINDOW and past >= WINDOW
    assert seq % SG_CHUNK == 0 and seq % L_SEL == 0 and past % L_SEL == 0
    nblk_p = max(seq // L_SEL, N_SEL)
    nblk_s = -(-max(-(-(past + 1) // L_SEL), N_SEL) // SUBLANES) * SUBLANES

    tables_p = _rope_tables(jnp.arange(seq), GROUP)
    tables_s_kv = _rope_tables(jnp.full((n_dec,), past), GROUP)
    tables_s_q = _rope_tables(jnp.full((SUBLANES,), past), 1)

    hp = x_prompt.reshape(batch * seq, d_model)
    hs = x_sample.reshape(n_dec, d_model)
    new_p = [[] for _ in range(7)]
    new_s = [[] for _ in range(8)]
    for l in range(depth):
        pwk, bk, pwv, bv = cmp_pool_k[l], cmp_bias_k[l], cmp_pool_v[l], cmp_bias_v[l]
        cw = conv_w[l]
        cb = conv_b[l].reshape(1, -1)

        xn_p = _rmsnorm(hp, norm1_g[l], bf16, "norm1_p")
        xn_s = _rmsnorm(hs, norm1_g[l], bf16, "norm1_s")
        w_in_nk = jnp.swapaxes(w_in[l], 0, 1)
        z1_p, z1_s = _matmul_wres(xn_p, xn_s, w_in_nk, Z1_W, (f32, f32), "in_proj_qkv", w_is_nk=True)
        uv_p, uv_s = _matmul_wres(xn_p, xn_s, w_in_nk, d_model, (bf16, f32), "in_proj_uv", w_is_nk=True,
                                  col0=QKV_W + NSA_GATE_W, epilogue=_gelu_tanh)
        gm_p, gm_s = _matmul_wres(xn_p, xn_s, w_in_nk, 2 * d_model, (bf16, bf16), "in_proj_gate", w_is_nk=True,
                                  col0=QKV_W + NSA_GATE_W + d_model, epilogue=_sigmoid)

        kc_p, vc_p, ks_p, vs_p, kw_p, vw_p, ksb, vsb, kwb, vwb = _prep_kv(z1_p, tables_p, seq, "prep_p")
        ak, av = _compress_prompt(kc_p, vc_p, pwk, bk, pwv, bv, batch, seq, "compress_p")
        ocmp, sel = _cmp_select_prompt(z1_p, ak, av, cmp_w_k[l], cmp_w_v[l], batch=batch, seq=seq, nblk=nblk_p,
                                       name="cmp_select_p")
        gates = z1_p[:, QKV_W:QKV_W + NSA_GATE_W].reshape(batch * seq, N_KV_HEADS, 3 * GROUP).transpose(1, 0, 2)
        nsa_p = _nsa_prompt(z1_p, tables_p, ksb, vsb, kwb, vwb, sel, ocmp, gates, batch=batch, seq=seq,
                            nblk=nblk_p, name="nsa_p")
        sg_p = _sgu_prompt(uv_p, sg_norm_g[l], sg_w[l], sg_b[l], "sgu_p")

        kc_s, vc_s, ks_s, vs_s, kw_s, vw_s = _prep_kv(z1_s, tables_s_kv, n_dec, "prep_s")[:6]
        ak, av = _compress_paged(cache_k_cmp[l], cache_v_cmp[l], page_table, pwk, bk, pwv, bv, "compress_s")
        heads_as_rows = lambda t, w: jnp.pad(t.reshape(n_dec, N_KV_HEADS, GROUP, w),
                                             ((0, 0), (0, 0), (0, SUBLANES - GROUP), (0, HEAD_DIM - w)))
        q_s = heads_as_rows(z1_s[:, :Q_W], HEAD_DIM)
        ocmp, imp = _cmp_sample(q_s, ak, av, cmp_w_k[l], cmp_w_v[l], n_batch=n_dec, nrows=past // STRIDE,
                                nblk=nblk_s, pos=past, name="cmp_s")
        imp_t = imp.transpose(1, 0, 2).reshape(nblk_s, n_dec * SUBLANES)
        idx = _rank_sample(imp_t, nblk=nblk_s, qb=past // L_SEL, name="rank_s")
        idx = idx.T.reshape(n_dec, SUBLANES, 2 * N_SEL)[:, :N_KV_HEADS].reshape(-1)
        new_rows = lambda t: t.reshape(n_dec, N_KV_HEADS, HEAD_DIM)
        win_k, win_v = _window_update(cache_k_win[l], cache_v_win[l], new_rows(kw_s), new_rows(vw_s), "window_s")
        gates = heads_as_rows(z1_s[:, QKV_W:QKV_W + NSA_GATE_W], 3)
        nsa_s = _nsa_sample(idx, page_table, q_s, tables_s_q, new_rows(ks_s), new_rows(vs_s),
                            win_k.reshape(-1, HEAD_DIM), win_v.reshape(-1, HEAD_DIM), ocmp, gates,
                            cache_k_slc[l], cache_v_slc[l], pos=past, name="nsa_s")
        nsa_s = nsa_s[:, :, :GROUP].reshape(n_dec, Q_W).astype(bf16)
        sg_s, sg_v = _sgu_sample(uv_s, sg_norm_g[l], sg_w[l], sg_b[l], "sgu_s")

        m_p, m_s = _merge(nsa_p, nsa_s, sg_p, sg_s, w_proj_a[l], w_proj_b[l], gm_p, gm_s, "merge")
        h_p, h_s = _matmul_wres(m_p, m_s, w_out[l], d_model, (f32, f32), "out_proj",
                                epilogue=lambda acc, x: x + acc, extras_p=(hp,), extras_s=(hs,))
        hn_p = _rmsnorm(h_p, norm2_g[l], bf16, "norm2_p")
        hn_s = _rmsnorm(h_s, norm2_g[l], bf16, "norm2_s")
        hist = state_ffn_conv[l]
        act_p, tail_a, tail_b, act_s, up_a, up_b = _ffn_up(hn_p, hn_s, w_up[l], cw, cb, hist[:, 0], hist[:, 1],
                                                           batch, seq, "ffn_up")
        wdown = w_down[l].astype(bf16)
        hp = _matmul_residual(act_p, wdown, h_p, "ffn_down_p")
        hs = _matmul_residual(act_s, wdown, h_s, "ffn_down_s")

        kv5 = lambda t: t.reshape(batch, seq, N_KV_HEADS, HEAD_DIM)
        keep_p = min(WINDOW, seq)
        for lst, val in zip(new_p, (kv5(kc_p), kv5(vc_p), kv5(ks_p), kv5(vs_p), kv5(kw_p)[:, seq - keep_p:],
                                    kv5(vw_p)[:, seq - keep_p:], jnp.concatenate([tail_a, tail_b], axis=-1))):
            lst.append(val)
        kv5 = lambda t: t.reshape(n_dec, 1, N_KV_HEADS, HEAD_DIM)
        conv_s = jnp.stack([hist[:, 1], jnp.concatenate([up_a, up_b], axis=-1)], axis=1)
        for lst, val in zip(new_s, (kv5(kc_s), kv5(vc_s), kv5(ks_s), kv5(vs_s), win_k, win_v,
                                    sg_v.reshape(n_dec, 1, -1), conv_s)):
            lst.append(val)

    y_prompt = _rmsnorm(hp, norm_f_g, f32, "norm_f_p").reshape(batch, seq, d_model)
    y_sample = _rmsnorm(hs, norm_f_g, f32, "norm_f_s").reshape(n_dec, 1, d_model)
    return (y_prompt, y_sample, *(jnp.stack(v) for v in new_p), *(jnp.stack(v) for v in new_s))
```

```python
import functools

import jax
import jax.numpy as jnp
from jax import lax
from jax.experimental import pallas as pl
from jax.experimental.pallas import tpu as pltpu

f32 = jnp.float32
bf16 = jnp.bfloat16
HIGHEST = lax.Precision.HIGHEST

N_HEADS = 16
N_KV_HEADS = 4
GROUP = N_HEADS // N_KV_HEADS
HEAD_DIM = 128
ROT_DIM = HEAD_DIM // 4
ROT_HALF = ROT_DIM // 2
ROPE_THETA = 500000.0
L_CMP = 32
STRIDE = 16
L_SEL = 64
N_SEL = 16
WINDOW = 512
FORCE_BONUS = 1000.0
SCALE = HEAD_DIM ** -0.5
NEG_INF = -1e30
SG_GROUPS = 16
SG_CHUNK = 128
CONV_W = 3
EPS = 1e-6

SUBLANES = 8
LANES = 128
GROUP_W = GROUP * HEAD_DIM
Q_W = N_HEADS * HEAD_DIM
KV_W = N_KV_HEADS * HEAD_DIM
NSA_GATE_W = 3 * N_HEADS
QKV_W = Q_W + 6 * KV_W
Z1_W = QKV_W + GROUP_W
COL_KC, COL_VC, COL_KS, COL_VS, COL_KW, COL_VW, COL_GATE = (Q_W // GROUP_W + i for i in range(7))

VMEM_LIMIT_BYTES = 56 * 1024 * 1024


def _params(*semantics):
    return pltpu.CompilerParams(dimension_semantics=semantics, vmem_limit_bytes=VMEM_LIMIT_BYTES)


def _tile(n, candidates):
    for c in candidates:
        if n % c == 0:
            return c
    return n


def _sigmoid(x):
    return 1.0 / (1.0 + jnp.exp(-x))


def _silu(x):
    return x * _sigmoid(x)


def _gelu_tanh(x):
    return x * (0.5 * (1.0 + jnp.tanh(0.7978845608028654 * (x + 0.044715 * (x * x * x)))))


def _dot(a, b, precision=None):
    return jnp.dot(a, b, preferred_element_type=f32, precision=precision)


def _dot_nt(a, b, precision=None):
    return lax.dot_general(a, b, (((1,), (1,)), ((), ())), precision=precision, preferred_element_type=f32)


def _rope(x, c, sa, sb):
    lanes = x.shape[-1]
    return x * c + pltpu.roll(x, lanes - ROT_HALF, 1) * sa + pltpu.roll(x, ROT_HALF, 1) * sb


def _rope_tables(pos, reps):
    n = pos.shape[0]
    inv_freq = jnp.power(jnp.float32(ROPE_THETA), -jnp.arange(ROT_HALF, dtype=f32) / ROT_HALF)
    ang = pos.astype(f32)[:, None] * inv_freq[None, :]
    cos, sin = jnp.cos(ang), jnp.sin(ang)
    zeros = lambda w: jnp.zeros((n, w), f32)
    c = jnp.concatenate([cos, cos, jnp.ones((n, HEAD_DIM - ROT_DIM), f32)], axis=1)
    sa = jnp.concatenate([-sin, zeros(HEAD_DIM - ROT_HALF)], axis=1)
    sb = jnp.concatenate([zeros(ROT_HALF), sin, zeros(HEAD_DIM - ROT_DIM)], axis=1)
    return tuple(jnp.tile(t, (1, reps)) for t in (c, sa, sb))


def _rmsnorm_kernel(x_ref, g_ref, o_ref):
    x = x_ref[...].astype(f32)
    y = x * lax.rsqrt(jnp.mean(x * x, axis=-1, keepdims=True) + EPS)
    o_ref[...] = (y * g_ref[...]).astype(o_ref.dtype)


def _rmsnorm(x, g, out_dtype, name):
    m, d = x.shape
    tm = _tile(m, (256, 128, 64, 32, 16, 8))
    return pl.pallas_call(
        _rmsnorm_kernel,
        grid=(m // tm,),
        in_specs=[pl.BlockSpec((tm, d), lambda i: (i, 0)), pl.BlockSpec((1, d), lambda i: (0, 0))],
        out_specs=pl.BlockSpec((tm, d), lambda i: (i, 0)),
        out_shape=jax.ShapeDtypeStruct((m, d), out_dtype),
        compiler_params=_params("parallel"),
        name=name,
    )(x, g.reshape(1, d).astype(f32))


CAST_ROWS = 128


def _cast_weight_tile(w_ref, wbf_ref):
    rows_total = w_ref.shape[0]
    step = _tile(rows_total, (CAST_ROWS,))
    for r0 in range(0, rows_total, step):
        wbf_ref[r0:r0 + step, :] = w_ref[r0:r0 + step, :].astype(bf16)


def _mm_wres_kernel(ap_ref, as_ref, w_ref, *rest, epilogue, n_extra, n_m, w_is_nk):
    extras_p, extras_s = rest[:n_extra], rest[n_extra:2 * n_extra]
    op_ref, os_ref, wbf_ref = rest[2 * n_extra:]
    mi = pl.program_id(1)
    mm = _dot_nt if w_is_nk else _dot

    @pl.when(mi == 0)
    def _():
        _cast_weight_tile(w_ref, wbf_ref)

    w = wbf_ref[...]
    op_ref[...] = epilogue(mm(ap_ref[...], w), *[e[...] for e in extras_p]).astype(op_ref.dtype)

    @pl.when(mi == n_m - 1)
    def _():
        os_ref[...] = epilogue(mm(as_ref[...], w), *[e[...] for e in extras_s]).astype(os_ref.dtype)


def _matmul_wres(a_p, a_s, w, n_cols, out_dtypes, name, *, w_is_nk=False, col0=0, epilogue=lambda acc: acc,
                 extras_p=(), extras_s=()):
    mp, k = a_p.shape
    ms = a_s.shape[0]
    tm = _tile(mp, (1024, 512, 256, 128))
    tn = _tile(n_cols, (512, 256, 128))
    n_m, n_n = mp // tm, n_cols // tn
    if w_is_nk:
        assert col0 % SUBLANES == 0
        w_spec = pl.BlockSpec((pl.Element(tn), pl.Element(k)),
                              lambda j, i: (pl.multiple_of(col0 + j * tn, SUBLANES), 0))
        wbf_shape = (tn, k)
    else:
        assert col0 % tn == 0
        w_spec = pl.BlockSpec((k, tn), lambda j, i: (0, col0 // tn + j))
        wbf_shape = (k, tn)
    in_specs = [pl.BlockSpec((tm, k), lambda j, i: (i, 0)), pl.BlockSpec((ms, k), lambda j, i: (0, 0)), w_spec]
    in_specs += [pl.BlockSpec((tm, tn), lambda j, i: (i, j)) for _ in extras_p]
    in_specs += [pl.BlockSpec((ms, tn), lambda j, i: (0, j)) for _ in extras_s]
    return pl.pallas_call(
        functools.partial(_mm_wres_kernel, epilogue=epilogue, n_extra=len(extras_p), n_m=n_m, w_is_nk=w_is_nk),
        grid=(n_n, n_m),
        in_specs=in_specs,
        out_specs=[pl.BlockSpec((tm, tn), lambda j, i: (i, j)), pl.BlockSpec((ms, tn), lambda j, i: (0, j))],
        out_shape=[jax.ShapeDtypeStruct((mp, n_cols), out_dtypes[0]),
                   jax.ShapeDtypeStruct((ms, n_cols), out_dtypes[1])],
        scratch_shapes=[pltpu.VMEM(wbf_shape, bf16)],
        compiler_params=_params("parallel", "arbitrary"),
        name=name,
    )(a_p, a_s, w, *extras_p, *extras_s)


def _merge_kernel(ap_ref, as_ref, bp_ref, bs_ref, wa_ref, wb_ref, gap_ref, gbp_ref, gas_ref, gbs_ref,
                  op_ref, os_ref, wabf_ref, wbbf_ref, *, n_m):
    mi = pl.program_id(1)

    @pl.when(mi == 0)
    def _():
        _cast_weight_tile(wa_ref, wabf_ref)
        _cast_weight_tile(wb_ref, wbbf_ref)

    def mix(a_ref, b_ref, ga_ref, gb_ref, o_ref):
        pa = _dot(a_ref[...], wabf_ref[...])
        pb = _dot(b_ref[...], wbbf_ref[...])
        o_ref[...] = (ga_ref[...].astype(f32) * pa + gb_ref[...].astype(f32) * pb).astype(o_ref.dtype)

    mix(ap_ref, bp_ref, gap_ref, gbp_ref, op_ref)

    @pl.when(mi == n_m - 1)
    def _():
        mix(as_ref, bs_ref, gas_ref, gbs_ref, os_ref)


def _merge(nsa_p, nsa_s, sg_p, sg_s, w_pa, w_pb, gates_p, gates_s, name):
    mp, ka = nsa_p.shape
    ms = nsa_s.shape[0]
    kb = sg_p.shape[1]
    d = w_pa.shape[1]
    tm = _tile(mp, (1024, 512, 256, 128))
    tn = _tile(d, (512, 256, 128))
    n_m, n_n = mp // tm, d // tn
    row_p = lambda w: pl.BlockSpec((tm, w), lambda j, i: (i, 0))
    row_s = lambda w: pl.BlockSpec((ms, w), lambda j, i: (0, 0))
    return pl.pallas_call(
        functools.partial(_merge_kernel, n_m=n_m),
        grid=(n_n, n_m),
        in_specs=[row_p(ka), row_s(ka), row_p(kb), row_s(kb),
                  pl.BlockSpec((ka, tn), lambda j, i: (0, j)), pl.BlockSpec((kb, tn), lambda j, i: (0, j)),
                  pl.BlockSpec((tm, tn), lambda j, i: (i, j)), pl.BlockSpec((tm, tn), lambda j, i: (i, j + n_n)),
                  pl.BlockSpec((ms, tn), lambda j, i: (0, j)), pl.BlockSpec((ms, tn), lambda j, i: (0, j + n_n))],
        out_specs=[pl.BlockSpec((tm, tn), lambda j, i: (i, j)), pl.BlockSpec((ms, tn), lambda j, i: (0, j))],
        out_shape=[jax.ShapeDtypeStruct((mp, d), bf16), jax.ShapeDtypeStruct((ms, d), bf16)],
        scratch_shapes=[pltpu.VMEM((ka, tn), bf16), pltpu.VMEM((kb, tn), bf16)],
        compiler_params=_params("parallel", "arbitrary"),
        name=name,
    )(nsa_p, nsa_s, sg_p, sg_s, w_pa, w_pb, gates_p, gates_p, gates_s, gates_s)


def _mm_resid_kernel(a_ref, w_ref, r_ref, o_ref):
    o_ref[...] = r_ref[...] + _dot(a_ref[...], w_ref[...])


def _matmul_residual(a, w, resid, name):
    m, k = a.shape
    n = w.shape[1]
    tm = _tile(m, (512, 256, 128, 64, 32, 16))
    tn = _tile(n, (256, 128))
    return pl.pallas_call(
        _mm_resid_kernel,
        grid=(m // tm, n // tn),
        in_specs=[pl.BlockSpec((tm, k), lambda i, j: (i, 0)), pl.BlockSpec((k, tn), lambda i, j: (0, j)),
                  pl.BlockSpec((tm, tn), lambda i, j: (i, j))],
        out_specs=pl.BlockSpec((tm, tn), lambda i, j: (i, j)),
        out_shape=jax.ShapeDtypeStruct((m, n), f32),
        compiler_params=_params("parallel", "arbitrary"),
        name=name,
    )(a, w, resid)


def _prep_kernel(kc, vc, ks, vs, kw, vw, c_ref, sa_ref, sb_ref, okc, ovc, oks, ovs, okw, ovw,
                 bks, bvs, bkw, bvw, *, tr):
    c, sa, sb = c_ref[...], sa_ref[...], sb_ref[...]
    values = (kc[...], vc[...], _rope(ks[...], c, sa, sb), vs[...], _rope(kw[...], c, sa, sb), vw[...])
    for val, o_ref in zip(values, (okc, ovc, oks, ovs, okw, ovw)):
        for g in range(N_KV_HEADS):
            o_ref[pl.ds(g, tr, stride=N_KV_HEADS), :] = val[:, g * HEAD_DIM:(g + 1) * HEAD_DIM]
    for val, o_ref in zip(values[2:], (bks, bvs, bkw, bvw)):
        o_ref[...] = val.astype(bf16)


def _prep_kv(z1, tables, rows_per_seq, name):
    m = z1.shape[0]
    tr = _tile(rows_per_seq, (256, 128, 64, 32, 16, 8))
    n_tab = tables[0].shape[0] // tr
    col = lambda cb: pl.BlockSpec((tr, KV_W), lambda i, cb=cb: (i, cb))
    tab = pl.BlockSpec((tr, KV_W), lambda i: (i % n_tab, 0))
    out = pl.BlockSpec((tr * N_KV_HEADS, HEAD_DIM), lambda i: (i, 0))
    out_b = pl.BlockSpec((tr, KV_W), lambda i: (i, 0))
    return pl.pallas_call(
        functools.partial(_prep_kernel, tr=tr),
        grid=(m // tr,),
        in_specs=[col(COL_KC), col(COL_VC), col(COL_KS), col(COL_VS), col(COL_KW), col(COL_VW), tab, tab, tab],
        out_specs=[out] * 6 + [out_b] * 4,
        out_shape=[jax.ShapeDtypeStruct((m * N_KV_HEADS, HEAD_DIM), f32)] * 6
        + [jax.ShapeDtypeStruct((m, KV_W), bf16)] * 4,
        compiler_params=_params("parallel"),
        name=name,
    )(z1, z1, z1, z1, z1, z1, *tables)


def _compress_prompt_kernel(k_ref, v_ref, pwk_ref, bk_ref, pwv_ref, bv_ref, ak_ref, av_ref, *, nsub):
    for x_ref, pw_ref, b_ref, o_ref in ((k_ref, pwk_ref, bk_ref, ak_ref), (v_ref, pwv_ref, bv_ref, av_ref)):
        for g in range(N_KV_HEADS):
            first = None
            second = None
            for j in range(STRIDE):
                rows = x_ref[pl.ds(j * N_KV_HEADS + g, nsub, stride=STRIDE * N_KV_HEADS), :]
                fa = rows * pw_ref[j:j + 1, :]
                sa = rows * pw_ref[STRIDE + j:STRIDE + j + 1, :]
                first = fa if first is None else first + fa
                second = sa if second is None else second + sa
            o_ref[0, :, g * HEAD_DIM:(g + 1) * HEAD_DIM] = _silu(pltpu.roll(first, 1, 0) + second + b_ref[...])


def _compress_prompt(kc, vc, pwk, bk, pwv, bv, batch, seq, name):
    nsub = seq // STRIDE
    xspec = pl.BlockSpec((seq * N_KV_HEADS, HEAD_DIM), lambda b: (b, 0))
    pwspec = pl.BlockSpec((L_CMP, HEAD_DIM), lambda b: (0, 0))
    bspec = pl.BlockSpec((1, HEAD_DIM), lambda b: (0, 0))
    ospec = pl.BlockSpec((1, nsub, KV_W), lambda b: (b, 0, 0))
    return pl.pallas_call(
        functools.partial(_compress_prompt_kernel, nsub=nsub),
        grid=(batch,),
        in_specs=[xspec, xspec, pwspec, bspec, pwspec, bspec],
        out_specs=[ospec, ospec],
        out_shape=[jax.ShapeDtypeStruct((batch, nsub, KV_W), f32)] * 2,
        compiler_params=_params("parallel"),
        name=name,
    )(kc, vc, pwk, bk.reshape(1, HEAD_DIM), pwv, bv.reshape(1, HEAD_DIM))


PAGES_PER_CHUNK = 16


def _compress_paged_kernel(pt_ref, wfk_ref, wsk_ref, bk_ref, wfv_ref, wsv_ref, bv_ref, poolk_ref, poolv_ref,
                           ak_ref, av_ref, bufk, bufv, sems, *, n_pages, page_rows):
    b = pl.program_id(0)
    pps = PAGES_PER_CHUNK
    n_chunks = n_pages // pps
    tiles_per_sub = STRIDE * N_KV_HEADS // SUBLANES
    nsub = page_rows // (STRIDE * N_KV_HEADS)

    def copies(chunk, slot):
        out = []
        for p in range(pps):
            phys = pt_ref[b * n_pages + chunk * pps + p]
            out.append(pltpu.make_async_copy(poolk_ref.at[phys], bufk.at[slot, p], sems.at[0, slot]))
            out.append(pltpu.make_async_copy(poolv_ref.at[phys], bufv.at[slot, p], sems.at[1, slot]))
        return out

    for cp in copies(0, 0):
        cp.start()
    carry = (jnp.zeros((SUBLANES, HEAD_DIM), f32), jnp.zeros((SUBLANES, HEAD_DIM), f32))
    for chunk in range(n_chunks):
        slot = chunk % 2
        if chunk + 1 < n_chunks:
            for cp in copies(chunk + 1, 1 - slot):
                cp.start()
        for cp in copies(chunk, slot):
            cp.wait()

        def page_body(p, prev, chunk=chunk, slot=slot):
            out_row0 = pl.multiple_of((chunk * pps + p) * nsub * SUBLANES, nsub * SUBLANES)
            new_prev = []
            for which, (buf, wf_ref, ws_ref, b_ref, o_ref) in enumerate(
                    ((bufk, wfk_ref, wsk_ref, bk_ref, ak_ref), (bufv, wfv_ref, wsv_ref, bv_ref, av_ref))):
                last = prev[which]
                for n in range(nsub):
                    first = None
                    second = None
                    for t in range(tiles_per_sub):
                        x = buf[slot, p, pl.ds((n * tiles_per_sub + t) * SUBLANES, SUBLANES), :]
                        fa = x * wf_ref[t]
                        sa = x * ws_ref[t]
                        first = fa if first is None else first + fa
                        second = sa if second is None else second + sa
                    first = first + pltpu.roll(first, SUBLANES // 2, 0)
                    second = second + pltpu.roll(second, SUBLANES // 2, 0)
                    o_ref[pl.ds(out_row0 + n * SUBLANES, SUBLANES), :] = _silu(last + second + b_ref[...])
                    last = first
                new_prev.append(last)
            return tuple(new_prev)

        carry = lax.fori_loop(0, pps, page_body, carry)


def _compress_paged(pool_k, pool_v, page_table, pwk, bk, pwv, bv, name):
    n_batch, n_pages = page_table.shape
    n_phys, page = pool_k.shape[:2]
    assert N_KV_HEADS * 2 == SUBLANES and n_pages % PAGES_PER_CHUNK == 0 and page % STRIDE == 0
    page_rows = page * N_KV_HEADS
    out_rows = n_pages * (page // STRIDE) * SUBLANES

    def tile_weights(pw):
        return jnp.repeat(pw.reshape(STRIDE // 2, 2, HEAD_DIM), N_KV_HEADS, axis=1)

    full3 = pl.BlockSpec((STRIDE // 2, SUBLANES, HEAD_DIM), lambda b, pt: (0, 0, 0))
    brow = pl.BlockSpec((SUBLANES, HEAD_DIM), lambda b, pt: (0, 0))
    o_spec = pl.BlockSpec((out_rows, HEAD_DIM), lambda b, pt: (b, 0))
    out_shape = jax.ShapeDtypeStruct((n_batch * out_rows, HEAD_DIM), f32)
    buf = pltpu.VMEM((2, PAGES_PER_CHUNK, page_rows, HEAD_DIM), f32)
    return pl.pallas_call(
        functools.partial(_compress_paged_kernel, n_pages=n_pages, page_rows=page_rows),
        grid_spec=pltpu.PrefetchScalarGridSpec(
            num_scalar_prefetch=1,
            grid=(n_batch,),
            in_specs=[full3, full3, brow, full3, full3, brow,
                      pl.BlockSpec(memory_space=pl.ANY), pl.BlockSpec(memory_space=pl.ANY)],
            out_specs=[o_spec, o_spec],
            scratch_shapes=[buf, buf, pltpu.SemaphoreType.DMA((2, 2))],
        ),
        out_shape=[out_shape, out_shape],
        compiler_params=_params("arbitrary"),
        name=name,
    )(page_table.reshape(-1),
      tile_weights(pwk[:STRIDE]), tile_weights(pwk[STRIDE:]), jnp.tile(bk.reshape(1, HEAD_DIM), (SUBLANES, 1)),
      tile_weights(pwv[:STRIDE]), tile_weights(pwv[STRIDE:]), jnp.tile(bv.reshape(1, HEAD_DIM), (SUBLANES, 1)),
      pool_k.reshape(n_phys, page_rows, HEAD_DIM), pool_v.reshape(n_phys, page_rows, HEAD_DIM))


def _block_overlap(blk, rown):
    c_start = (rown - 1) * STRIDE
    return (rown >= 1) & (c_start < blk * L_SEL + L_SEL) & (c_start + L_CMP > blk * L_SEL)


def _stable_rank(score, score_ref, blk, nblk):
    score_ref[...] = score

    def count_better(i, cnt):
        other = score_ref[pl.ds(i, 1), :]
        better = (other > score) | ((other == score) & (i < blk))
        return cnt + better.astype(f32)

    return lax.fori_loop(0, nblk, count_better, jnp.zeros(score.shape, f32))


def _cmp_select_prompt_kernel(q_ref, ak_ref, av_ref, wk_ref, wv_ref, ocmp_ref, sel_ref, score_ref, *,
                              tq, nrows, nblk):
    pos0 = pl.program_id(2) * tq
    pos_col = pos0 + lax.broadcasted_iota(jnp.int32, (tq, 1), 0)
    pos_row = pos0 + lax.broadcasted_iota(jnp.int32, (1, tq), 1)

    ck = _dot(ak_ref[0], wk_ref[...], precision=HIGHEST)
    cv = _dot(av_ref[0].astype(bf16), wv_ref[...].astype(bf16)).astype(bf16)

    rown = lax.broadcasted_iota(jnp.int32, (1, nrows), 1)
    visible = (rown >= 1) & (rown * STRIDE + (L_CMP - STRIDE - 1) <= pos_col)

    pg = jnp.zeros((tq, nrows), f32)
    for r in range(GROUP):
        sl = slice(r * HEAD_DIM, (r + 1) * HEAD_DIM)
        s = _dot_nt(q_ref[:, sl], ck, precision=HIGHEST) * SCALE
        s = jnp.where(visible, s, NEG_INF)
        e = jnp.where(visible, jnp.exp(s - jnp.max(s, axis=-1, keepdims=True)), 0.0)
        den = jnp.sum(e, axis=-1, keepdims=True)
        p = jnp.where(den > 0.0, e / jnp.where(den > 0.0, den, 1.0), 0.0)
        ocmp_ref[:, sl] = _dot(p.astype(bf16), cv)
        pg = pg + p

    blk = lax.broadcasted_iota(jnp.int32, (nblk, 1), 0)
    imp_t = _dot_nt(_block_overlap(blk, rown).astype(f32), pg, precision=HIGHEST)
    qb = pos_row // L_SEL
    causal = blk <= qb
    forced = (blk == 0) | (blk == qb) | (blk == qb - 1)
    score = jnp.where(causal, imp_t + FORCE_BONUS * forced.astype(f32), -jnp.inf)
    rank = _stable_rank(score, score_ref, blk, nblk)
    chosen = jnp.where(causal & (rank < float(N_SEL)), 1.0, 0.0).astype(bf16)
    eye = (lax.broadcasted_iota(jnp.int32, (tq, tq), 0)
           == lax.broadcasted_iota(jnp.int32, (tq, tq), 1)).astype(bf16)
    sel_ref[0, 0] = _dot_nt(eye, chosen)


def _cmp_select_prompt(q, ak, av, wk, wv, *, batch, seq, nblk, name):
    nrows = ak.shape[1]
    tq = _tile(seq, (512, 256, 128))
    n_q = seq // tq
    q_spec = pl.BlockSpec((tq, GROUP_W), lambda b, g, i: (b * n_q + i, g))
    a_spec = pl.BlockSpec((1, nrows, HEAD_DIM), lambda b, g, i: (b, 0, g))
    w_spec = pl.BlockSpec((HEAD_DIM, HEAD_DIM), lambda b, g, i: (0, 0))
    return pl.pallas_call(
        functools.partial(_cmp_select_prompt_kernel, tq=tq, nrows=nrows, nblk=nblk),
        grid=(batch, N_KV_HEADS, n_q),
        in_specs=[q_spec, a_spec, a_spec, w_spec, w_spec],
        out_specs=[pl.BlockSpec((tq, GROUP_W), lambda b, g, i: (b * n_q + i, g)),
                   pl.BlockSpec((1, 1, tq, nblk), lambda b, g, i: (b, g, i, 0))],
        out_shape=[jax.ShapeDtypeStruct((batch * seq, Q_W), f32),
                   jax.ShapeDtypeStruct((batch, N_KV_HEADS, seq, nblk), f32)],
        scratch_shapes=[pltpu.VMEM((nblk, tq), f32)],
        compiler_params=_params("parallel", "parallel", "arbitrary"),
        name=name,
    )(q, ak, av, wk, wv)


def _cmp_sample_kernel(q_ref, ak_ref, av_ref, wk_ref, wv_ref, ocmp_ref, imp_ref, *, nrows, nblk, pos):
    rown = lax.broadcasted_iota(jnp.int32, (nrows, 1), 0)
    visible = (rown >= 1) & (rown * STRIDE + (L_CMP - STRIDE - 1) <= pos)
    overlap = _block_overlap(lax.broadcasted_iota(jnp.int32, (nblk, 1), 0),
                             lax.broadcasted_iota(jnp.int32, (1, nrows), 1)).astype(f32)
    lane = lax.broadcasted_iota(jnp.int32, (1, SUBLANES), 1)
    is_head = lane < GROUP
    imp_all = jnp.zeros((nblk, SUBLANES), f32)
    for g in range(N_KV_HEADS):
        a_k = ak_ref[pl.ds(g, nrows, stride=SUBLANES), :]
        a_v = av_ref[pl.ds(g, nrows, stride=SUBLANES), :]
        ck = _dot(a_k, wk_ref[...], precision=HIGHEST)
        cv = _dot(a_v.astype(bf16), wv_ref[...].astype(bf16))
        s = _dot_nt(ck, q_ref[0, g], precision=HIGHEST) * SCALE
        s = jnp.where(visible, s, NEG_INF)
        e = jnp.where(visible, jnp.exp(s - jnp.max(s, axis=0, keepdims=True)), 0.0)
        den = jnp.sum(e, axis=0, keepdims=True)
        p = jnp.where(is_head & (den > 0.0), e / jnp.where(den > 0.0, den, 1.0), 0.0)
        for r in range(GROUP):
            ocmp_ref[0, g, r:r + 1, :] = jnp.sum(p[:, r:r + 1] * cv, axis=0, keepdims=True)
        ocmp_ref[0, g, GROUP:, :] = jnp.zeros((SUBLANES - GROUP, HEAD_DIM), f32)
        imp = jnp.sum(_dot(overlap, p, precision=HIGHEST), axis=1, keepdims=True)
        imp_all = jnp.where(lane == g, imp, imp_all)
    imp_ref[0] = imp_all


def _cmp_sample(q, ak, av, wk, wv, *, n_batch, nrows, nblk, pos, name):
    q_spec = pl.BlockSpec((1, N_KV_HEADS, SUBLANES, HEAD_DIM), lambda b: (b, 0, 0, 0))
    a_spec = pl.BlockSpec((nrows * SUBLANES, HEAD_DIM), lambda b: (b, 0))
    w_spec = pl.BlockSpec((HEAD_DIM, HEAD_DIM), lambda b: (0, 0))
    return pl.pallas_call(
        functools.partial(_cmp_sample_kernel, nrows=nrows, nblk=nblk, pos=pos),
        grid=(n_batch,),
        in_specs=[q_spec, a_spec, a_spec, w_spec, w_spec],
        out_specs=[q_spec, pl.BlockSpec((1, nblk, SUBLANES), lambda b: (b, 0, 0))],
        out_shape=[jax.ShapeDtypeStruct((n_batch, N_KV_HEADS, SUBLANES, HEAD_DIM), f32),
                   jax.ShapeDtypeStruct((n_batch, nblk, SUBLANES), f32)],
        compiler_params=_params("parallel"),
        name=name,
    )(q, ak, av, wk, wv)


def _rank_sample_kernel(imp_ref, idx_ref, score_ref, *, nblk, qb):
    blk = lax.broadcasted_iota(jnp.int32, (nblk, 1), 0)
    causal = blk <= qb
    forced = (blk == 0) | (blk == qb) | (blk == qb - 1)
    score = jnp.where(causal, imp_ref[...] + FORCE_BONUS * forced.astype(f32), -jnp.inf)
    rank = _stable_rank(score, score_ref, blk, nblk)
    for slot in range(N_SEL):
        hit = causal & (rank == float(slot))
        idx_ref[slot:slot + 1, :] = jnp.sum(jnp.where(hit, blk, 0), axis=0, keepdims=True)
        idx_ref[N_SEL + slot:N_SEL + slot + 1, :] = jnp.max(hit.astype(jnp.int32), axis=0, keepdims=True)


def _rank_sample(imp_t, *, nblk, qb, name):
    lanes = imp_t.shape[1]
    return pl.pallas_call(
        functools.partial(_rank_sample_kernel, nblk=nblk, qb=qb),
        grid=(1,),
        in_specs=[pl.BlockSpec((nblk, lanes), lambda i: (0, 0))],
        out_specs=pl.BlockSpec((2 * N_SEL, lanes), lambda i: (0, 0)),
        out_shape=jax.ShapeDtypeStruct((2 * N_SEL, lanes), jnp.int32),
        scratch_shapes=[pltpu.VMEM((nblk, lanes), f32)],
        compiler_params=_params("arbitrary"),
        name=name,
    )(imp_t)


def _softmax_pv(s, v, exp=jnp.exp):
    e = exp(s - jnp.max(s, axis=-1, keepdims=True))
    den = jnp.sum(e, axis=-1, keepdims=True)
    return _dot(e.astype(bf16), v) / den


LOG2_E = 1.4426950408889634


def _masked_attention(q, k, v, mask, tq):
    heads = q.shape[0] // tq
    keys = k.shape[0]
    s = _dot_nt(q, k)
    s = jnp.where(mask[None], s.reshape(heads, tq, keys), NEG_INF).reshape(heads * tq, keys)
    return _softmax_pv(s, v, exp=jnp.exp2)


HEADS_PER_CHAIN = 2


def _nsa_prompt_kernel(q_ref, c_ref, sa_ref, sb_ref, ks_ref, vs_ref, kw_ref, vw_ref, sel_ref, ocmp_ref,
                       gate_ref, o_ref, *, tq, seq, nblk, span, n_q):
    qi = pl.program_id(2)
    t0 = qi * tq
    qr = (_rope(q_ref[...], c_ref[...], sa_ref[...], sb_ref[...]) * (SCALE * LOG2_E)).astype(bf16)
    n_chain = GROUP // HEADS_PER_CHAIN
    chains = [jnp.concatenate([qr[:, (c * HEADS_PER_CHAIN + h) * HEAD_DIM:(c * HEADS_PER_CHAIN + h + 1) * HEAD_DIM]
                               for h in range(HEADS_PER_CHAIN)], axis=0) for c in range(n_chain)]
    tpos = t0 + lax.broadcasted_iota(jnp.int32, (tq, 1), 0)

    start = pl.multiple_of(jnp.maximum(t0 - WINDOW, 0), tq)
    wpos = start + lax.broadcasted_iota(jnp.int32, (1, span), 1)
    wmask = (wpos <= tpos) & (wpos > tpos - WINDOW)
    kw = kw_ref[pl.ds(start, span), :].astype(bf16)
    vw = vw_ref[pl.ds(start, span), :].astype(bf16)
    o_win = [_masked_attention(qc, kw, vw, wmask, tq) for qc in chains]

    gates = _sigmoid(gate_ref[...])
    sel = sel_ref[0, 0].astype(bf16)

    tiles_per_class = 2
    for cls in range(-(-n_q // tiles_per_class)):
        ext = min(seq, (cls + 1) * tiles_per_class * tq)

        @pl.when(qi // tiles_per_class == cls)
        def _(ext=ext):
            kpos = lax.broadcasted_iota(jnp.int32, (1, ext), 1)
            expand = (kpos // L_SEL == lax.broadcasted_iota(jnp.int32, (nblk, 1), 0)).astype(bf16)
            mask = (_dot(sel, expand) > 0.5) & (kpos <= tpos)
            k = ks_ref[0:ext, :].astype(bf16)
            v = vs_ref[0:ext, :].astype(bf16)
            for c, qc in enumerate(chains):
                o_slc = _masked_attention(qc, k, v, mask, tq)
                for h in range(HEADS_PER_CHAIN):
                    r = c * HEADS_PER_CHAIN + h
                    sl = slice(r * HEAD_DIM, (r + 1) * HEAD_DIM)
                    rows = slice(h * tq, (h + 1) * tq)
                    o = (gates[:, 3 * r:3 * r + 1] * ocmp_ref[:, sl] + gates[:, 3 * r + 1:3 * r + 2] * o_slc[rows]
                         + gates[:, 3 * r + 2:3 * r + 3] * o_win[c][rows])
                    o_ref[:, sl] = o.astype(o_ref.dtype)


def _nsa_prompt(z1, tables, ks, vs, kw, vw, sel, ocmp, gates, *, batch, seq, nblk, name):
    tq = _tile(seq, (256, 128))
    n_q = seq // tq
    span = min(WINDOW + tq, seq)
    row_blk = lambda w: pl.BlockSpec((tq, w), lambda b, g, i: (b * n_q + i, g))
    tab = pl.BlockSpec((tq, GROUP_W), lambda b, g, i: (i, 0))
    kv = pl.BlockSpec((seq, HEAD_DIM), lambda b, g, i: (b, g))
    return pl.pallas_call(
        functools.partial(_nsa_prompt_kernel, tq=tq, seq=seq, nblk=nblk, span=span, n_q=n_q),
        grid=(batch, N_KV_HEADS, n_q),
        in_specs=[row_blk(GROUP_W), tab, tab, tab, kv, kv, kv, kv,
                  pl.BlockSpec((1, 1, tq, nblk), lambda b, g, i: (b, g, i, 0)),
                  row_blk(GROUP_W),
                  pl.BlockSpec((None, tq, 3 * GROUP), lambda b, g, i: (g, b * n_q + i, 0))],
        out_specs=row_blk(GROUP_W),
        out_shape=jax.ShapeDtypeStruct((batch * seq, Q_W), bf16),
        compiler_params=_params("parallel", "parallel", "arbitrary"),
        name=name,
    )(z1, *tables, ks, vs, kw, vw, sel, ocmp, gates)


def _nsa_sample_kernel(idx_ref, pt_ref, q_ref, c_ref, sa_ref, sb_ref, ksn_ref, vsn_ref, kwin_ref, vwin_ref,
                       ocmp_ref, gate_ref, poolk_ref, poolv_ref, o_ref, kbuf, vbuf, sems, *,
                       n_pages, page, pos, window):
    b = pl.program_id(0)
    per_page = page // L_SEL
    past_blocks = n_pages * per_page
    n_keys = N_SEL * L_SEL

    def block_copies(g, slot):
        blk = idx_ref[(b * N_KV_HEADS + g) * 2 * N_SEL + slot]
        jp = jnp.minimum(blk, past_blocks - 1)
        phys = pt_ref[b * n_pages + jp // per_page]
        off = pl.multiple_of((jp % per_page) * L_SEL, L_SEL)
        dst = pl.ds(slot * L_SEL, L_SEL)
        return (pltpu.make_async_copy(poolk_ref.at[phys, pl.ds(off, L_SEL), g], kbuf.at[g, dst], sems.at[0]),
                pltpu.make_async_copy(poolv_ref.at[phys, pl.ds(off, L_SEL), g], vbuf.at[g, dst], sems.at[1]))

    for g in range(N_KV_HEADS):
        for slot in range(N_SEL):
            for cp in block_copies(g, slot):
                cp.start()
    for g in range(N_KV_HEADS):
        for slot in range(N_SEL):
            for cp in block_copies(g, slot):
                cp.wait()

    lane = lax.broadcasted_iota(jnp.int32, (1, n_keys), 1)
    lane_slot = lane // L_SEL
    for g in range(N_KV_HEADS):
        qr = _rope(q_ref[0, g], c_ref[...], sa_ref[...], sb_ref[...]).astype(bf16)
        blk_vec = jnp.zeros((1, n_keys), jnp.int32)
        ok_vec = jnp.zeros((1, n_keys), jnp.int32)
        base = (b * N_KV_HEADS + g) * 2 * N_SEL
        for slot in range(N_SEL):
            blk_vec = jnp.where(lane_slot == slot, idx_ref[base + slot], blk_vec)
            ok_vec = jnp.where(lane_slot == slot, idx_ref[base + N_SEL + slot], ok_vec)
        in_past = blk_vec < past_blocks
        kpos = blk_vec * L_SEL + lane % L_SEL
        mask = (ok_vec > 0) & (kpos <= pos)
        k_new = ksn_ref[0, g:g + 1, :].astype(bf16).astype(f32)
        v_new = vsn_ref[0, g:g + 1, :].astype(bf16).astype(f32)
        s_new = jnp.sum(qr.astype(f32) * k_new, axis=-1, keepdims=True)
        s = jnp.where(in_past, _dot_nt(qr, kbuf[g].astype(bf16)), s_new) * SCALE
        s = jnp.where(mask, s, NEG_INF)
        e = jnp.exp(s - jnp.max(s, axis=-1, keepdims=True))
        den = jnp.sum(e, axis=-1, keepdims=True)
        e_new = jnp.sum(jnp.where(in_past, 0.0, e), axis=-1, keepdims=True)
        e_past = jnp.where(in_past, e, 0.0).astype(bf16)
        o_slc = (_dot(e_past, vbuf[g].astype(bf16)) + e_new.astype(bf16).astype(f32) * v_new) / den
        win_rows = pl.ds(g, window, stride=N_KV_HEADS)
        o_win = _softmax_pv(_dot_nt(qr, kwin_ref[win_rows, :].astype(bf16)) * SCALE,
                            vwin_ref[win_rows, :].astype(bf16))
        gt = _sigmoid(gate_ref[0, g])
        o_ref[0, g] = gt[:, 0:1] * ocmp_ref[0, g] + gt[:, 1:2] * o_slc + gt[:, 2:3] * o_win


def _nsa_sample(idx, page_table, q, tables, ks_new, vs_new, kwin, vwin, ocmp, gates, pool_k, pool_v, *,
                pos, name):
    n_batch, n_pages = page_table.shape
    page = pool_k.shape[1]
    window = kwin.shape[0] // (n_batch * N_KV_HEADS)
    tab = pl.BlockSpec((SUBLANES, HEAD_DIM), lambda b, *_: (0, 0))
    heads = pl.BlockSpec((1, N_KV_HEADS, SUBLANES, HEAD_DIM), lambda b, *_: (b, 0, 0, 0))
    new_row = pl.BlockSpec((1, N_KV_HEADS, HEAD_DIM), lambda b, *_: (b, 0, 0))
    win = pl.BlockSpec((window * N_KV_HEADS, HEAD_DIM), lambda b, *_: (b, 0))
    return pl.pallas_call(
        functools.partial(_nsa_sample_kernel, n_pages=n_pages, page=page, pos=pos, window=window),
        grid_spec=pltpu.PrefetchScalarGridSpec(
            num_scalar_prefetch=2,
            grid=(n_batch,),
            in_specs=[heads, tab, tab, tab, new_row, new_row, win, win, heads, heads,
                      pl.BlockSpec(memory_space=pl.ANY), pl.BlockSpec(memory_space=pl.ANY)],
            out_specs=heads,
            scratch_shapes=[pltpu.VMEM((N_KV_HEADS, N_SEL * L_SEL, HEAD_DIM), f32),
                            pltpu.VMEM((N_KV_HEADS, N_SEL * L_SEL, HEAD_DIM), f32),
                            pltpu.SemaphoreType.DMA((2,))],
        ),
        out_shape=jax.ShapeDtypeStruct((n_batch, N_KV_HEADS, SUBLANES, HEAD_DIM), f32),
        compiler_params=_params("arbitrary"),
        name=name,
    )(idx, page_table.reshape(-1), q, *tables, ks_new, vs_new, kwin, vwin, ocmp, gates, pool_k, pool_v)


def _window_update_kernel(kin_ref, vin_ref, knew_ref, vnew_ref, kout_ref, vout_ref):
    rows = kin_ref.shape[1]
    keep = rows - N_KV_HEADS
    for src, new, dst in ((kin_ref, knew_ref, kout_ref), (vin_ref, vnew_ref, vout_ref)):
        dst[0, 0:keep, :] = src[0, N_KV_HEADS:rows, :]
        dst[0, keep:rows, :] = new[0]


def _window_update(win_k, win_v, k_new, v_new, name):
    n_batch, rows, d = win_k.shape
    win = pl.BlockSpec((1, rows, d), lambda b: (b, 0, 0))
    new = pl.BlockSpec((1, N_KV_HEADS, d), lambda b: (b, 0, 0))
    shape = jax.ShapeDtypeStruct(win_k.shape, win_k.dtype)
    return pl.pallas_call(
        _window_update_kernel,
        grid=(n_batch,),
        in_specs=[win, win, new, new],
        out_specs=[win, win],
        out_shape=[shape, shape],
        compiler_params=_params("parallel"),
        name=name,
    )(win_k, win_v, k_new, v_new)


def _sgu_prompt_kernel(u_ref, v_ref, g_ref, w_ref, bt_ref, o_ref):
    v = v_ref[...].astype(f32)
    vn = (v * lax.rsqrt(jnp.mean(v * v, axis=-1, keepdims=True) + EPS) * g_ref[...]).astype(bf16)
    tril = (lax.broadcasted_iota(jnp.int32, (SG_CHUNK, SG_CHUNK), 0)
            >= lax.broadcasted_iota(jnp.int32, (SG_CHUNK, SG_CHUNK), 1))
    bt = bt_ref[...]
    group_dim = v.shape[-1] // SG_GROUPS
    for gi in range(SG_GROUPS):
        sl = slice(gi * group_dim, (gi + 1) * group_dim)
        w = jnp.where(tril, w_ref[gi], 0.0).astype(bf16)
        mixed = _dot(w, vn[:, sl]) + bt[:, gi:gi + 1]
        o_ref[:, sl] = (u_ref[:, sl].astype(f32) * mixed).astype(o_ref.dtype)


def _sgu_prompt(uv, norm_g, w_s, b_s, name):
    m, two_w = uv.shape
    width = two_w // 2
    return pl.pallas_call(
        _sgu_prompt_kernel,
        grid=(m // SG_CHUNK,),
        in_specs=[pl.BlockSpec((SG_CHUNK, width), lambda i: (i, 0)),
                  pl.BlockSpec((SG_CHUNK, width), lambda i: (i, 1)),
                  pl.BlockSpec((1, width), lambda i: (0, 0)),
                  pl.BlockSpec((SG_GROUPS, SG_CHUNK, SG_CHUNK), lambda i: (0, 0, 0)),
                  pl.BlockSpec((SG_CHUNK, SG_GROUPS), lambda i: (0, 0))],
        out_specs=pl.BlockSpec((SG_CHUNK, width), lambda i: (i, 0)),
        out_shape=jax.ShapeDtypeStruct((m, width), bf16),
        compiler_params=_params("parallel"),
        name=name,
    )(uv, uv, norm_g.reshape(1, width), w_s, b_s.T)


def _sgu_sample_kernel(u_ref, v_ref, g_ref, w0_ref, b0_ref, o_ref, vn_ref):
    v = v_ref[...]
    vn = v * lax.rsqrt(jnp.mean(v * v, axis=-1, keepdims=True) + EPS) * g_ref[...]
    vn_ref[...] = vn
    o_ref[...] = (u_ref[...] * (vn * w0_ref[...] + b0_ref[...])).astype(o_ref.dtype)


def _sgu_sample(uv, norm_g, w_s, b_s, name):
    m, two_w = uv.shape
    width = two_w // 2
    group_dim = width // SG_GROUPS
    w0 = jnp.repeat(w_s[:, 0, 0], group_dim).reshape(1, width)
    b0 = jnp.repeat(b_s[:, 0], group_dim).reshape(1, width)
    row = pl.BlockSpec((1, width), lambda i: (0, 0))
    return pl.pallas_call(
        _sgu_sample_kernel,
        grid=(1,),
        in_specs=[pl.BlockSpec((m, width), lambda i: (0, 0)), pl.BlockSpec((m, width), lambda i: (0, 1)),
                  row, row, row],
        out_specs=[pl.BlockSpec((m, width), lambda i: (0, 0))] * 2,
        out_shape=[jax.ShapeDtypeStruct((m, width), bf16), jax.ShapeDtypeStruct((m, width), f32)],
        compiler_params=_params("arbitrary"),
        name=name,
    )(uv, uv, norm_g.reshape(1, width), w0, b0)


CONV_HALO = SUBLANES
FFN_SUB_ROWS = 1024


def _ffn_up_kernel(xp_ref, xs_ref, wa_ref, wb_ref, cwa_ref, cwb_ref, cba_ref, cbb_ref,
                   h0a_ref, h0b_ref, h1a_ref, h1b_ref,
                   act_ref, ta_ref, tb_ref, acts_ref, upa_ref, upb_ref,
                   wabf_ref, wbbf_ref, bufa, bufb, *, tm, tiles_per_seq, n_m):
    mi = pl.program_id(1)
    lo, hi = CONV_HALO, CONV_HALO + tm

    @pl.when(mi == 0)
    def _():
        _cast_weight_tile(wa_ref, wabf_ref)
        _cast_weight_tile(wb_ref, wbbf_ref)

    @pl.when(mi % tiles_per_seq == 0)
    def _():
        bufa[0:lo, :] = jnp.zeros((lo, bufa.shape[1]), f32)
        bufb[0:lo, :] = jnp.zeros((lo, bufb.shape[1]), f32)

    sub = _tile(tm, (FFN_SUB_ROWS,))
    for r0 in range(0, tm, sub):
        x = xp_ref[r0:r0 + sub, :]
        halves = []
        for wbf_ref, cw_ref, cb_ref, buf in ((wabf_ref, cwa_ref, cba_ref, bufa), (wbbf_ref, cwb_ref, cbb_ref, bufb)):
            up = _dot(x, wbf_ref[...])
            buf[lo + r0:lo + r0 + sub, :] = up
            conv = cb_ref[...] + buf[lo + r0 - 2:lo + r0 + sub - 2, :] * cw_ref[0:1, :]
            conv = conv + buf[lo + r0 - 1:lo + r0 + sub - 1, :] * cw_ref[1:2, :]
            halves.append(conv + up * cw_ref[2:3, :])
        act_ref[r0:r0 + sub, :] = (_silu(halves[0]) * halves[1]).astype(act_ref.dtype)
    for buf, tail_ref in ((bufa, ta_ref), (bufb, tb_ref)):
        tail_ref[0] = buf[hi - (CONV_W - 1):hi, :]
        buf[0:lo, :] = buf[hi - CONV_HALO:hi, :]

    @pl.when(mi == n_m - 1)
    def _():
        xs = xs_ref[...]
        halves = []
        for wbf_ref, cw_ref, cb_ref, h0_ref, h1_ref, up_ref in (
                (wabf_ref, cwa_ref, cba_ref, h0a_ref, h1a_ref, upa_ref),
                (wbbf_ref, cwb_ref, cbb_ref, h0b_ref, h1b_ref, upb_ref)):
            up = _dot(xs, wbf_ref[...])
            up_ref[...] = up
            conv = cb_ref[...] + h0_ref[...] * cw_ref[0:1, :]
            conv = conv + h1_ref[...] * cw_ref[1:2, :]
            halves.append(conv + up * cw_ref[2:3, :])
        acts_ref[...] = (_silu(halves[0]) * halves[1]).astype(acts_ref.dtype)


def _ffn_up(hn_p, hn_s, w_up, conv_w, conv_b, hist0, hist1, batch, seq, name):
    mp, d = hn_p.shape
    ms = hn_s.shape[0]
    d_ff = w_up.shape[1] // 2
    tm = _tile(seq, (1024, 512, 256, 128))
    tn = _tile(d_ff, (256, 128))
    nb = d_ff // tn
    n_m = mp // tm
    tiles_per_seq = seq // tm
    col = lambda shape, off: pl.BlockSpec(shape, lambda j, i: (0, j + off))
    tail = pl.BlockSpec((1, CONV_W - 1, tn), lambda j, i: (i // tiles_per_seq, 0, j))
    tail_shape = jax.ShapeDtypeStruct((batch, CONV_W - 1, d_ff), f32)
    up_shape = jax.ShapeDtypeStruct((ms, d_ff), f32)
    buf = pltpu.VMEM((CONV_HALO + tm, tn), f32)
    return pl.pallas_call(
        functools.partial(_ffn_up_kernel, tm=tm, tiles_per_seq=tiles_per_seq, n_m=n_m),
        grid=(nb, n_m),
        in_specs=[pl.BlockSpec((tm, d), lambda j, i: (i, 0)), pl.BlockSpec((ms, d), lambda j, i: (0, 0)),
                  col((d, tn), 0), col((d, tn), nb),
                  col((CONV_W, tn), 0), col((CONV_W, tn), nb), col((1, tn), 0), col((1, tn), nb),
                  col((ms, tn), 0), col((ms, tn), nb), col((ms, tn), 0), col((ms, tn), nb)],
        out_specs=[pl.BlockSpec((tm, tn), lambda j, i: (i, j)), tail, tail,
                   col((ms, tn), 0), col((ms, tn), 0), col((ms, tn), 0)],
        out_shape=[jax.ShapeDtypeStruct((mp, d_ff), bf16), tail_shape, tail_shape,
                   jax.ShapeDtypeStruct((ms, d_ff), bf16), up_shape, up_shape],
        scratch_shapes=[pltpu.VMEM((d, tn), bf16), pltpu.VMEM((d, tn), bf16), buf, buf],
        compiler_params=_params("parallel", "arbitrary"),
        name=name,
    )(hn_p, hn_s, w_up, w_up, conv_w, conv_w, conv_b, conv_b, hist0, hist0, hist1, hist1)


def kernel(x_prompt, x_sample, cache_k_cmp, cache_v_cmp, cache_k_slc, cache_v_slc, cache_k_win, cache_v_win,
           state_ffn_conv, page_table, norm1_g, w_in, cmp_pool_k, cmp_bias_k, cmp_w_k, cmp_pool_v, cmp_bias_v,
           cmp_w_v, sg_norm_g, sg_w, sg_b, w_proj_a, w_proj_b, w_out, norm2_g, w_up, conv_w, conv_b, w_down,
           norm_f_g):
    batch, seq, d_model = x_prompt.shape
    n_dec, dec_seq, _ = x_sample.shape
    depth = w_in.shape[0]
    page = cache_k_cmp.shape[2]
    past = page_table.shape[1] * page
    assert dec_seq == 1 and cache_k_win.shape[2] == WINDOW and past >= WINDOW
    assert seq % SG_CHUNK == 0 and seq % L_SEL == 0 and past % L_SEL == 0
    nblk_p = max(seq // L_SEL, N_SEL)
    nblk_s = -(-max(-(-(past + 1) // L_SEL), N_SEL) // SUBLANES) * SUBLANES

    tables_p = _rope_tables(jnp.arange(seq), GROUP)
    tables_s_kv = _rope_tables(jnp.full((n_dec,), past), GROUP)
    tables_s_q = _rope_tables(jnp.full((SUBLANES,), past), 1)

    hp = x_prompt.reshape(batch * seq, d_model)
    hs = x_sample.reshape(n_dec, d_model)
    new_p = [[] for _ in range(7)]
    new_s = [[] for _ in range(8)]
    for l in range(depth):
        pwk, bk, pwv, bv = cmp_pool_k[l], cmp_bias_k[l], cmp_pool_v[l], cmp_bias_v[l]
        cw = conv_w[l]
        cb = conv_b[l].reshape(1, -1)

        xn_p = _rmsnorm(hp, norm1_g[l], bf16, "norm1_p")
        xn_s = _rmsnorm(hs, norm1_g[l], bf16, "norm1_s")
        w_in_nk = jnp.swapaxes(w_in[l], 0, 1)
        z1_p, z1_s = _matmul_wres(xn_p, xn_s, w_in_nk, Z1_W, (f32, f32), "in_proj_qkv", w_is_nk=True)
        uv_p, uv_s = _matmul_wres(xn_p, xn_s, w_in_nk, d_model, (bf16, f32), "in_proj_uv", w_is_nk=True,
                                  col0=QKV_W + NSA_GATE_W, epilogue=_gelu_tanh)
        gm_p, gm_s = _matmul_wres(xn_p, xn_s, w_in_nk, 2 * d_model, (bf16, bf16), "in_proj_gate", w_is_nk=True,
                                  col0=QKV_W + NSA_GATE_W + d_model, epilogue=_sigmoid)

        kc_p, vc_p, ks_p, vs_p, kw_p, vw_p, ksb, vsb, kwb, vwb = _prep_kv(z1_p, tables_p, seq, "prep_p")
        ak, av = _compress_prompt(kc_p, vc_p, pwk, bk, pwv, bv, batch, seq, "compress_p")
        ocmp, sel = _cmp_select_prompt(z1_p, ak, av, cmp_w_k[l], cmp_w_v[l], batch=batch, seq=seq, nblk=nblk_p,
                                       name="cmp_select_p")
        gates = z1_p[:, QKV_W:QKV_W + NSA_GATE_W].reshape(batch * seq, N_KV_HEADS, 3 * GROUP).transpose(1, 0, 2)
        nsa_p = _nsa_prompt(z1_p, tables_p, ksb, vsb, kwb, vwb, sel, ocmp, gates, batch=batch, seq=seq,
                            nblk=nblk_p, name="nsa_p")
        sg_p = _sgu_prompt(uv_p, sg_norm_g[l], sg_w[l], sg_b[l], "sgu_p")

        kc_s, vc_s, ks_s, vs_s, kw_s, vw_s = _prep_kv(z1_s, tables_s_kv, n_dec, "prep_s")[:6]
        ak, av = _compress_paged(cache_k_cmp[l], cache_v_cmp[l], page_table, pwk, bk, pwv, bv, "compress_s")
        heads_as_rows = lambda t, w: jnp.pad(t.reshape(n_dec, N_KV_HEADS, GROUP, w),
                                             ((0, 0), (0, 0), (0, SUBLANES - GROUP), (0, HEAD_DIM - w)))
        q_s = heads_as_rows(z1_s[:, :Q_W], HEAD_DIM)
        ocmp, imp = _cmp_sample(q_s, ak, av, cmp_w_k[l], cmp_w_v[l], n_batch=n_dec, nrows=past // STRIDE,
                                nblk=nblk_s, pos=past, name="cmp_s")
        imp_t = imp.transpose(1, 0, 2).reshape(nblk_s, n_dec * SUBLANES)
        idx = _rank_sample(imp_t, nblk=nblk_s, qb=past // L_SEL, name="rank_s")
        idx = idx.T.reshape(n_dec, SUBLANES, 2 * N_SEL)[:, :N_KV_HEADS].reshape(-1)
        new_rows = lambda t: t.reshape(n_dec, N_KV_HEADS, HEAD_DIM)
        cache_rows = lambda t: t.reshape(n_dec, WINDOW * N_KV_HEADS, HEAD_DIM)
        win_k, win_v = _window_update(cache_rows(cache_k_win[l]), cache_rows(cache_v_win[l]), new_rows(kw_s),
                                      new_rows(vw_s), "window_s")
        gates = heads_as_rows(z1_s[:, QKV_W:QKV_W + NSA_GATE_W], 3)
        nsa_s = _nsa_sample(idx, page_table, q_s, tables_s_q, new_rows(ks_s), new_rows(vs_s),
                            win_k.reshape(-1, HEAD_DIM), win_v.reshape(-1, HEAD_DIM), ocmp, gates,
                            cache_k_slc[l], cache_v_slc[l], pos=past, name="nsa_s")
        nsa_s = nsa_s[:, :, :GROUP].reshape(n_dec, Q_W).astype(bf16)
        sg_s, sg_v = _sgu_sample(uv_s, sg_norm_g[l], sg_w[l], sg_b[l], "sgu_s")

        m_p, m_s = _merge(nsa_p, nsa_s, sg_p, sg_s, w_proj_a[l], w_proj_b[l], gm_p, gm_s, "merge")
        h_p, h_s = _matmul_wres(m_p, m_s, w_out[l], d_model, (f32, f32), "out_proj",
                                epilogue=lambda acc, x: x + acc, extras_p=(hp,), extras_s=(hs,))
        hn_p = _rmsnorm(h_p, norm2_g[l], bf16, "norm2_p")
        hn_s = _rmsnorm(h_s, norm2_g[l], bf16, "norm2_s")
        hist = state_ffn_conv[l]
        act_p, tail_a, tail_b, act_s, up_a, up_b = _ffn_up(hn_p, hn_s, w_up[l], cw, cb, hist[:, 0], hist[:, 1],
                                                           batch, seq, "ffn_up")
        wdown = w_down[l].astype(bf16)
        hp = _matmul_residual(act_p, wdown, h_p, "ffn_down_p")
        hs = _matmul_residual(act_s, wdown, h_s, "ffn_down_s")

        kv5 = lambda t: t.reshape(batch, seq, N_KV_HEADS, HEAD_DIM)
        keep_p = min(WINDOW, seq)
        for lst, val in zip(new_p, (kv5(kc_p), kv5(vc_p), kv5(ks_p), kv5(vs_p), kv5(kw_p)[:, seq - keep_p:],
                                    kv5(vw_p)[:, seq - keep_p:], jnp.concatenate([tail_a, tail_b], axis=-1))):
            lst.append(val)
        kv5 = lambda t: t.reshape(n_dec, 1, N_KV_HEADS, HEAD_DIM)
        conv_s = jnp.stack([hist[:, 1], jnp.concatenate([up_a, up_b], axis=-1)], axis=1)
        win5 = lambda t: t.reshape(n_dec, WINDOW, N_KV_HEADS, HEAD_DIM)
        for lst, val in zip(new_s, (kv5(kc_s), kv5(vc_s), kv5(ks_s), kv5(vs_s), win5(win_k), win5(win_v),
                                    sg_v.reshape(n_dec, 1, -1), conv_s)):
            lst.append(val)

    y_prompt = _rmsnorm(hp, norm_f_g, f32, "norm_f_p").reshape(batch, seq, d_model)
    y_sample = _rmsnorm(hs, norm_f_g, f32, "norm_f_s").reshape(n_dec, 1, d_model)
    return (y_prompt, y_sample, *(jnp.stack(v) for v in new_p), *(jnp.stack(v) for v in new_s))
```

```python
import functools

import jax
import jax.numpy as jnp
from jax import lax
from jax.experimental import pallas as pl
from jax.experimental.pallas import tpu as pltpu

f32 = jnp.float32
bf16 = jnp.bfloat16

N_HEADS = 16
N_KV_HEADS = 4
GROUP = N_HEADS // N_KV_HEADS
HEAD_DIM = 128
ROT_DIM = HEAD_DIM // 4
ROT_HALF = ROT_DIM // 2
ROPE_THETA = 500000.0
L_CMP = 32
STRIDE = 16
L_SEL = 64
N_SEL = 16
WINDOW = 512
FORCE_BONUS = 1000.0
SCALE = HEAD_DIM ** -0.5
NEG_INF = -1e30
SG_GROUPS = 16
SG_CHUNK = 128
CONV_W = 3
EPS = 1e-6

SUBLANES = 8
LANES = 128
GROUP_W = GROUP * HEAD_DIM
Q_W = N_HEADS * HEAD_DIM
KV_W = N_KV_HEADS * HEAD_DIM
NSA_GATE_W = 3 * N_HEADS
QKV_W = Q_W + 6 * KV_W
Z1_W = QKV_W + GROUP_W
COL_KC, COL_VC, COL_KS, COL_VS, COL_KW, COL_VW, COL_GATE = (Q_W // GROUP_W + i for i in range(7))

VMEM_LIMIT_BYTES = 56 * 1024 * 1024


def _params(*semantics):
    return pltpu.CompilerParams(dimension_semantics=semantics, vmem_limit_bytes=VMEM_LIMIT_BYTES)


def _tile(n, candidates):
    for c in candidates:
        if n % c == 0:
            return c
    return n


def _sigmoid(x):
    return 0.5 * jnp.tanh(0.5 * x) + 0.5


def _silu(x):
    return x * _sigmoid(x)


def _gelu_tanh(x):
    return x * (0.5 * (1.0 + jnp.tanh(0.7978845608028654 * (x + 0.044715 * (x * x * x)))))


def _dot(a, b, precision=None):
    return jnp.dot(a, b, preferred_element_type=f32, precision=precision)


def _dot_nt(a, b, precision=None):
    return lax.dot_general(a, b, (((1,), (1,)), ((), ())), precision=precision, preferred_element_type=f32)


def _split2(x):
    hi = x.astype(bf16)
    return hi, (x - hi.astype(f32)).astype(bf16)


def _dot_3pass(a, b, mm=_dot):
    a_hi, a_lo = a if isinstance(a, tuple) else _split2(a)
    b_hi, b_lo = b if isinstance(b, tuple) else _split2(b)
    return mm(a_hi, b_hi) + (mm(a_hi, b_lo) + mm(a_lo, b_hi))


def _dot_mask(m, x, mm=_dot):
    hi = x.astype(bf16)
    mid, lo = _split2(x - hi.astype(f32))
    return mm(m, hi) + (mm(m, mid) + mm(m, lo))


def _rope(x, c, sa, sb):
    lanes = x.shape[-1]
    return x * c + pltpu.roll(x, lanes - ROT_HALF, 1) * sa + pltpu.roll(x, ROT_HALF, 1) * sb


def _rope_tables(pos, reps):
    n = pos.shape[0]
    inv_freq = jnp.power(jnp.float32(ROPE_THETA), -jnp.arange(ROT_HALF, dtype=f32) / ROT_HALF)
    ang = pos.astype(f32)[:, None] * inv_freq[None, :]
    cos, sin = jnp.cos(ang), jnp.sin(ang)
    zeros = lambda w: jnp.zeros((n, w), f32)
    c = jnp.concatenate([cos, cos, jnp.ones((n, HEAD_DIM - ROT_DIM), f32)], axis=1)
    sa = jnp.concatenate([-sin, zeros(HEAD_DIM - ROT_HALF)], axis=1)
    sb = jnp.concatenate([zeros(ROT_HALF), sin, zeros(HEAD_DIM - ROT_DIM)], axis=1)
    return tuple(jnp.tile(t, (1, reps)) for t in (c, sa, sb))


def _rmsnorm_kernel(x_ref, g_ref, o_ref):
    x = x_ref[...].astype(f32)
    y = x * lax.rsqrt(jnp.mean(x * x, axis=-1, keepdims=True) + EPS)
    o_ref[...] = (y * g_ref[...]).astype(o_ref.dtype)


def _rmsnorm(x, g, out_dtype, name):
    m, d = x.shape
    tm = _tile(m, (256, 128, 64, 32, 16, 8))
    return pl.pallas_call(
        _rmsnorm_kernel,
        grid=(m // tm,),
        in_specs=[pl.BlockSpec((tm, d), lambda i: (i, 0)), pl.BlockSpec((1, d), lambda i: (0, 0))],
        out_specs=pl.BlockSpec((tm, d), lambda i: (i, 0)),
        out_shape=jax.ShapeDtypeStruct((m, d), out_dtype),
        compiler_params=_params("parallel"),
        name=name,
    )(x, g.reshape(1, d).astype(f32))


CAST_ROWS = 128


def _cast_weight_tile(w_ref, wbf_ref):
    rows_total = w_ref.shape[0]
    step = _tile(rows_total, (CAST_ROWS,))
    for r0 in range(0, rows_total, step):
        wbf_ref[r0:r0 + step, :] = w_ref[r0:r0 + step, :].astype(bf16)


def _mm_wres_kernel(ap_ref, as_ref, w_ref, *rest, epilogue, n_extra, n_m, w_is_nk):
    extras_p, extras_s = rest[:n_extra], rest[n_extra:2 * n_extra]
    op_ref, os_ref, wbf_ref = rest[2 * n_extra:]
    mi = pl.program_id(1)
    mm = _dot_nt if w_is_nk else _dot

    @pl.when(mi == 0)
    def _():
        _cast_weight_tile(w_ref, wbf_ref)

    w = wbf_ref[...]
    op_ref[...] = epilogue(mm(ap_ref[...], w), *[e[...] for e in extras_p]).astype(op_ref.dtype)

    @pl.when(mi == n_m - 1)
    def _():
        os_ref[...] = epilogue(mm(as_ref[...], w), *[e[...] for e in extras_s]).astype(os_ref.dtype)


def _matmul_wres(a_p, a_s, w, n_cols, out_dtypes, name, *, w_is_nk=False, col0=0, epilogue=lambda acc: acc,
                 extras_p=(), extras_s=()):
    mp, k = a_p.shape
    ms = a_s.shape[0]
    tm = _tile(mp, (1024, 512, 256, 128))
    tn = _tile(n_cols, (512, 256, 128))
    n_m, n_n = mp // tm, n_cols // tn
    if w_is_nk:
        assert col0 % SUBLANES == 0
        w_spec = pl.BlockSpec((pl.Element(tn), pl.Element(k)),
                              lambda j, i: (pl.multiple_of(col0 + j * tn, SUBLANES), 0))
        wbf_shape = (tn, k)
    else:
        assert col0 % tn == 0
        w_spec = pl.BlockSpec((k, tn), lambda j, i: (0, col0 // tn + j))
        wbf_shape = (k, tn)
    in_specs = [pl.BlockSpec((tm, k), lambda j, i: (i, 0)), pl.BlockSpec((ms, k), lambda j, i: (0, 0)), w_spec]
    in_specs += [pl.BlockSpec((tm, tn), lambda j, i: (i, j)) for _ in extras_p]
    in_specs += [pl.BlockSpec((ms, tn), lambda j, i: (0, j)) for _ in extras_s]
    return pl.pallas_call(
        functools.partial(_mm_wres_kernel, epilogue=epilogue, n_extra=len(extras_p), n_m=n_m, w_is_nk=w_is_nk),
        grid=(n_n, n_m),
        in_specs=in_specs,
        out_specs=[pl.BlockSpec((tm, tn), lambda j, i: (i, j)), pl.BlockSpec((ms, tn), lambda j, i: (0, j))],
        out_shape=[jax.ShapeDtypeStruct((mp, n_cols), out_dtypes[0]),
                   jax.ShapeDtypeStruct((ms, n_cols), out_dtypes[1])],
        scratch_shapes=[pltpu.VMEM(wbf_shape, bf16)],
        compiler_params=_params("parallel", "arbitrary"),
        name=name,
    )(a_p, a_s, w, *extras_p, *extras_s)


def _merge_kernel(ap_ref, as_ref, bp_ref, bs_ref, wa_ref, wb_ref, gap_ref, gbp_ref, gas_ref, gbs_ref,
                  op_ref, os_ref, wabf_ref, wbbf_ref, *, n_m):
    mi = pl.program_id(1)

    @pl.when(mi == 0)
    def _():
        _cast_weight_tile(wa_ref, wabf_ref)
        _cast_weight_tile(wb_ref, wbbf_ref)

    def mix(a_ref, b_ref, ga_ref, gb_ref, o_ref):
        pa = _dot(a_ref[...], wabf_ref[...])
        pb = _dot(b_ref[...], wbbf_ref[...])
        o_ref[...] = (ga_ref[...].astype(f32) * pa + gb_ref[...].astype(f32) * pb).astype(o_ref.dtype)

    mix(ap_ref, bp_ref, gap_ref, gbp_ref, op_ref)

    @pl.when(mi == n_m - 1)
    def _():
        mix(as_ref, bs_ref, gas_ref, gbs_ref, os_ref)


def _merge(nsa_p, nsa_s, sg_p, sg_s, w_pa, w_pb, gates_p, gates_s, name):
    mp, ka = nsa_p.shape
    ms = nsa_s.shape[0]
    kb = sg_p.shape[1]
    d = w_pa.shape[1]
    tm = _tile(mp, (1024, 512, 256, 128))
    tn = _tile(d, (512, 256, 128))
    n_m, n_n = mp // tm, d // tn
    row_p = lambda w: pl.BlockSpec((tm, w), lambda j, i: (i, 0))
    row_s = lambda w: pl.BlockSpec((ms, w), lambda j, i: (0, 0))
    return pl.pallas_call(
        functools.partial(_merge_kernel, n_m=n_m),
        grid=(n_n, n_m),
        in_specs=[row_p(ka), row_s(ka), row_p(kb), row_s(kb),
                  pl.BlockSpec((ka, tn), lambda j, i: (0, j)), pl.BlockSpec((kb, tn), lambda j, i: (0, j)),
                  pl.BlockSpec((tm, tn), lambda j, i: (i, j)), pl.BlockSpec((tm, tn), lambda j, i: (i, j + n_n)),
                  pl.BlockSpec((ms, tn), lambda j, i: (0, j)), pl.BlockSpec((ms, tn), lambda j, i: (0, j + n_n))],
        out_specs=[pl.BlockSpec((tm, tn), lambda j, i: (i, j)), pl.BlockSpec((ms, tn), lambda j, i: (0, j))],
        out_shape=[jax.ShapeDtypeStruct((mp, d), bf16), jax.ShapeDtypeStruct((ms, d), bf16)],
        scratch_shapes=[pltpu.VMEM((ka, tn), bf16), pltpu.VMEM((kb, tn), bf16)],
        compiler_params=_params("parallel", "arbitrary"),
        name=name,
    )(nsa_p, nsa_s, sg_p, sg_s, w_pa, w_pb, gates_p, gates_p, gates_s, gates_s)


def _mm_resid_kernel(a_ref, w_ref, r_ref, o_ref):
    o_ref[...] = r_ref[...] + _dot(a_ref[...], w_ref[...])


def _matmul_residual(a, w, resid, name):
    m, k = a.shape
    n = w.shape[1]
    tm = _tile(m, (512, 256, 128, 64, 32, 16))
    tn = _tile(n, (256, 128))
    return pl.pallas_call(
        _mm_resid_kernel,
        grid=(m // tm, n // tn),
        in_specs=[pl.BlockSpec((tm, k), lambda i, j: (i, 0)), pl.BlockSpec((k, tn), lambda i, j: (0, j)),
                  pl.BlockSpec((tm, tn), lambda i, j: (i, j))],
        out_specs=pl.BlockSpec((tm, tn), lambda i, j: (i, j)),
        out_shape=jax.ShapeDtypeStruct((m, n), f32),
        compiler_params=_params("parallel", "arbitrary"),
        name=name,
    )(a, w, resid)


def _prep_kernel(kc, vc, ks, vs, kw, vw, c_ref, sa_ref, sb_ref, okc, ovc, oks, ovs, okw, ovw,
                 bks, bvs, bkw, bvw, *, tr):
    c, sa, sb = c_ref[...], sa_ref[...], sb_ref[...]
    values = (kc[...], vc[...], _rope(ks[...], c, sa, sb), vs[...], _rope(kw[...], c, sa, sb), vw[...])
    for val, o_ref in zip(values, (okc, ovc, oks, ovs, okw, ovw)):
        for g in range(N_KV_HEADS):
            o_ref[pl.ds(g, tr, stride=N_KV_HEADS), :] = val[:, g * HEAD_DIM:(g + 1) * HEAD_DIM]
    for val, o_ref in zip(values[2:], (bks, bvs, bkw, bvw)):
        o_ref[...] = val.astype(bf16)


def _prep_kv(z1, tables, rows_per_seq, name):
    m = z1.shape[0]
    tr = _tile(rows_per_seq, (256, 128, 64, 32, 16, 8))
    n_tab = tables[0].shape[0] // tr
    col = lambda cb: pl.BlockSpec((tr, KV_W), lambda i, cb=cb: (i, cb))
    tab = pl.BlockSpec((tr, KV_W), lambda i: (i % n_tab, 0))
    out = pl.BlockSpec((tr * N_KV_HEADS, HEAD_DIM), lambda i: (i, 0))
    out_b = pl.BlockSpec((tr, KV_W), lambda i: (i, 0))
    return pl.pallas_call(
        functools.partial(_prep_kernel, tr=tr),
        grid=(m // tr,),
        in_specs=[col(COL_KC), col(COL_VC), col(COL_KS), col(COL_VS), col(COL_KW), col(COL_VW), tab, tab, tab],
        out_specs=[out] * 6 + [out_b] * 4,
        out_shape=[jax.ShapeDtypeStruct((m * N_KV_HEADS, HEAD_DIM), f32)] * 6
        + [jax.ShapeDtypeStruct((m, KV_W), bf16)] * 4,
        compiler_params=_params("parallel"),
        name=name,
    )(z1, z1, z1, z1, z1, z1, *tables)


def _compress_prompt_kernel(k_ref, v_ref, pwk_ref, bk_ref, pwv_ref, bv_ref, ak_ref, av_ref, *, nsub):
    for x_ref, pw_ref, b_ref, o_ref in ((k_ref, pwk_ref, bk_ref, ak_ref), (v_ref, pwv_ref, bv_ref, av_ref)):
        for g in range(N_KV_HEADS):
            first = None
            second = None
            for j in range(STRIDE):
                rows = x_ref[pl.ds(j * N_KV_HEADS + g, nsub, stride=STRIDE * N_KV_HEADS), :]
                fa = rows * pw_ref[j:j + 1, :]
                sa = rows * pw_ref[STRIDE + j:STRIDE + j + 1, :]
                first = fa if first is None else first + fa
                second = sa if second is None else second + sa
            o_ref[0, :, g * HEAD_DIM:(g + 1) * HEAD_DIM] = _silu(pltpu.roll(first, 1, 0) + second + b_ref[...])


def _compress_prompt(kc, vc, pwk, bk, pwv, bv, batch, seq, name):
    nsub = seq // STRIDE
    xspec = pl.BlockSpec((seq * N_KV_HEADS, HEAD_DIM), lambda b: (b, 0))
    pwspec = pl.BlockSpec((L_CMP, HEAD_DIM), lambda b: (0, 0))
    bspec = pl.BlockSpec((1, HEAD_DIM), lambda b: (0, 0))
    ospec = pl.BlockSpec((1, nsub, KV_W), lambda b: (b, 0, 0))
    return pl.pallas_call(
        functools.partial(_compress_prompt_kernel, nsub=nsub),
        grid=(batch,),
        in_specs=[xspec, xspec, pwspec, bspec, pwspec, bspec],
        out_specs=[ospec, ospec],
        out_shape=[jax.ShapeDtypeStruct((batch, nsub, KV_W), f32)] * 2,
        compiler_params=_params("parallel"),
        name=name,
    )(kc, vc, pwk, bk.reshape(1, HEAD_DIM), pwv, bv.reshape(1, HEAD_DIM))


PAGES_PER_CHUNK = 16


def _compress_paged_kernel(pt_ref, wfk_ref, wsk_ref, bk_ref, wfv_ref, wsv_ref, bv_ref, poolk_ref, poolv_ref,
                           ak_ref, av_ref, bufk, bufv, sems, *, n_pages, page_rows, pps):
    b = pl.program_id(0)
    n_chunks = n_pages // pps
    tiles_per_sub = STRIDE * N_KV_HEADS // SUBLANES
    nsub = page_rows // (STRIDE * N_KV_HEADS)

    def copies(seq, chunk, slot):
        out = []
        for p in range(pps):
            phys = pt_ref[seq * n_pages + chunk * pps + p]
            out.append(pltpu.make_async_copy(poolk_ref.at[phys], bufk.at[slot, p], sems.at[0, slot]))
            out.append(pltpu.make_async_copy(poolv_ref.at[phys], bufv.at[slot, p], sems.at[1, slot]))
        return out

    @pl.when(b == 0)
    def _():
        for cp in copies(b, 0, 0):
            cp.start()

    carry = (jnp.zeros((SUBLANES, HEAD_DIM), f32), jnp.zeros((SUBLANES, HEAD_DIM), f32))
    for chunk in range(n_chunks):
        slot = chunk % 2
        if chunk + 1 < n_chunks:
            for cp in copies(b, chunk + 1, 1 - slot):
                cp.start()
        else:
            @pl.when(b + 1 < pl.num_programs(0))
            def _():
                for cp in copies(b + 1, 0, 0):
                    cp.start()
        for cp in copies(b, chunk, slot):
            cp.wait()

        def page_body(p, prev, chunk=chunk, slot=slot):
            out_row0 = pl.multiple_of((chunk * pps + p) * nsub * SUBLANES, nsub * SUBLANES)
            new_prev = []
            for which, (buf, wf_ref, ws_ref, b_ref, o_ref) in enumerate(
                    ((bufk, wfk_ref, wsk_ref, bk_ref, ak_ref), (bufv, wfv_ref, wsv_ref, bv_ref, av_ref))):
                last = prev[which]
                for n in range(nsub):
                    first = None
                    second = None
                    for t in range(tiles_per_sub):
                        x = buf[slot, p, pl.ds((n * tiles_per_sub + t) * SUBLANES, SUBLANES), :]
                        fa = x * wf_ref[t]
                        sa = x * ws_ref[t]
                        first = fa if first is None else first + fa
                        second = sa if second is None else second + sa
                    first = first + pltpu.roll(first, SUBLANES // 2, 0)
                    second = second + pltpu.roll(second, SUBLANES // 2, 0)
                    o_ref[pl.ds(out_row0 + n * SUBLANES, SUBLANES), :] = _silu(last + second + b_ref[...])
                    last = first
                new_prev.append(last)
            return tuple(new_prev)

        carry = lax.fori_loop(0, pps, page_body, carry)


def _compress_paged(pool_k, pool_v, page_table, pwk, bk, pwv, bv, name):
    n_batch, n_pages = page_table.shape
    n_phys, page = pool_k.shape[:2]
    pps = min(PAGES_PER_CHUNK, n_pages // 2)
    assert N_KV_HEADS * 2 == SUBLANES and n_pages % (2 * pps) == 0 and page % STRIDE == 0
    page_rows = page * N_KV_HEADS
    out_rows = n_pages * (page // STRIDE) * SUBLANES

    def tile_weights(pw):
        return jnp.repeat(pw.reshape(STRIDE // 2, 2, HEAD_DIM), N_KV_HEADS, axis=1)

    full3 = pl.BlockSpec((STRIDE // 2, SUBLANES, HEAD_DIM), lambda b, pt: (0, 0, 0))
    brow = pl.BlockSpec((SUBLANES, HEAD_DIM), lambda b, pt: (0, 0))
    o_spec = pl.BlockSpec((out_rows, HEAD_DIM), lambda b, pt: (b, 0))
    out_shape = jax.ShapeDtypeStruct((n_batch * out_rows, HEAD_DIM), f32)
    buf = pltpu.VMEM((2, pps, page_rows, HEAD_DIM), f32)
    return pl.pallas_call(
        functools.partial(_compress_paged_kernel, n_pages=n_pages, page_rows=page_rows, pps=pps),
        grid_spec=pltpu.PrefetchScalarGridSpec(
            num_scalar_prefetch=1,
            grid=(n_batch,),
            in_specs=[full3, full3, brow, full3, full3, brow,
                      pl.BlockSpec(memory_space=pl.ANY), pl.BlockSpec(memory_space=pl.ANY)],
            out_specs=[o_spec, o_spec],
            scratch_shapes=[buf, buf, pltpu.SemaphoreType.DMA((2, 2))],
        ),
        out_shape=[out_shape, out_shape],
        compiler_params=_params("arbitrary"),
        name=name,
    )(page_table.reshape(-1),
      tile_weights(pwk[:STRIDE]), tile_weights(pwk[STRIDE:]), jnp.tile(bk.reshape(1, HEAD_DIM), (SUBLANES, 1)),
      tile_weights(pwv[:STRIDE]), tile_weights(pwv[STRIDE:]), jnp.tile(bv.reshape(1, HEAD_DIM), (SUBLANES, 1)),
      pool_k.reshape(n_phys, page_rows, HEAD_DIM), pool_v.reshape(n_phys, page_rows, HEAD_DIM))


def _block_overlap(blk, rown):
    c_start = (rown - 1) * STRIDE
    return (rown >= 1) & (c_start < blk * L_SEL + L_SEL) & (c_start + L_CMP > blk * L_SEL)


def _stable_rank(score, score_ref, blk, nblk):
    score_ref[...] = score

    def count_better(i, cnt):
        other = score_ref[pl.ds(i, 1), :]
        better = (other > score) | ((other == score) & (i < blk))
        return cnt + better.astype(f32)

    return lax.fori_loop(0, nblk, count_better, jnp.zeros(score.shape, f32))


def _cmp_select_prompt_kernel(q_ref, ak_ref, av_ref, wk_ref, wv_ref, ocmp_ref, sel_ref, score_ref, *,
                              tq, nrows, nblk):
    pos0 = pl.program_id(2) * tq
    pos_col = pos0 + lax.broadcasted_iota(jnp.int32, (tq, 1), 0)
    pos_row = pos0 + lax.broadcasted_iota(jnp.int32, (1, tq), 1)

    ck = _split2(_dot_3pass(ak_ref[0], wk_ref[...]))
    cv = _dot(av_ref[0].astype(bf16), wv_ref[...].astype(bf16)).astype(bf16)

    rown = lax.broadcasted_iota(jnp.int32, (1, nrows), 1)
    visible = (rown >= 1) & (rown * STRIDE + (L_CMP - STRIDE - 1) <= pos_col)

    pg = jnp.zeros((tq, nrows), f32)
    for r in range(GROUP):
        sl = slice(r * HEAD_DIM, (r + 1) * HEAD_DIM)
        s = _dot_3pass(q_ref[:, sl], ck, mm=_dot_nt) * SCALE
        s = jnp.where(visible, s, NEG_INF)
        e = jnp.where(visible, jnp.exp(s - jnp.max(s, axis=-1, keepdims=True)), 0.0)
        den = jnp.sum(e, axis=-1, keepdims=True)
        p = jnp.where(den > 0.0, e / jnp.where(den > 0.0, den, 1.0), 0.0)
        ocmp_ref[:, sl] = _dot(p.astype(bf16), cv)
        pg = pg + p

    blk = lax.broadcasted_iota(jnp.int32, (nblk, 1), 0)
    imp_t = _dot_mask(_block_overlap(blk, rown).astype(bf16), pg, mm=_dot_nt)
    qb = pos_row // L_SEL
    causal = blk <= qb
    forced = (blk == 0) | (blk == qb) | (blk == qb - 1)
    score = jnp.where(causal, imp_t + FORCE_BONUS * forced.astype(f32), -jnp.inf)
    rank = _stable_rank(score, score_ref, blk, nblk)
    chosen = jnp.where(causal & (rank < float(N_SEL)), 1.0, 0.0).astype(bf16)
    eye = (lax.broadcasted_iota(jnp.int32, (tq, tq), 0)
           == lax.broadcasted_iota(jnp.int32, (tq, tq), 1)).astype(bf16)
    sel_ref[0, 0] = _dot_nt(eye, chosen)


def _cmp_select_prompt(q, ak, av, wk, wv, *, batch, seq, nblk, name):
    nrows = ak.shape[1]
    tq = _tile(seq, (512, 256, 128))
    n_q = seq // tq
    q_spec = pl.BlockSpec((tq, GROUP_W), lambda b, g, i: (b * n_q + i, g))
    a_spec = pl.BlockSpec((1, nrows, HEAD_DIM), lambda b, g, i: (b, 0, g))
    w_spec = pl.BlockSpec((HEAD_DIM, HEAD_DIM), lambda b, g, i: (0, 0))
    return pl.pallas_call(
        functools.partial(_cmp_select_prompt_kernel, tq=tq, nrows=nrows, nblk=nblk),
        grid=(batch, N_KV_HEADS, n_q),
        in_specs=[q_spec, a_spec, a_spec, w_spec, w_spec],
        out_specs=[pl.BlockSpec((tq, GROUP_W), lambda b, g, i: (b * n_q + i, g)),
                   pl.BlockSpec((1, 1, tq, nblk), lambda b, g, i: (b, g, i, 0))],
        out_shape=[jax.ShapeDtypeStruct((batch * seq, Q_W), f32),
                   jax.ShapeDtypeStruct((batch, N_KV_HEADS, seq, nblk), f32)],
        scratch_shapes=[pltpu.VMEM((nblk, tq), f32)],
        compiler_params=_params("parallel", "parallel", "arbitrary"),
        name=name,
    )(q, ak, av, wk, wv)


def _cmp_sample_kernel(q_ref, ak_ref, av_ref, wk_ref, wv_ref, ocmp_ref, imp_ref, *, nrows, nblk, pos):
    rown = lax.broadcasted_iota(jnp.int32, (nrows, 1), 0)
    visible = (rown >= 1) & (rown * STRIDE + (L_CMP - STRIDE - 1) <= pos)
    overlap = _block_overlap(lax.broadcasted_iota(jnp.int32, (nblk, 1), 0),
                             lax.broadcasted_iota(jnp.int32, (1, nrows), 1)).astype(bf16)
    lane = lax.broadcasted_iota(jnp.int32, (1, SUBLANES), 1)
    is_head = lane < GROUP
    imp_all = jnp.zeros((nblk, SUBLANES), f32)
    for g in range(N_KV_HEADS):
        a_k = ak_ref[pl.ds(g, nrows, stride=SUBLANES), :]
        a_v = av_ref[pl.ds(g, nrows, stride=SUBLANES), :]
        ck = _dot_3pass(a_k, wk_ref[...])
        cv = _dot(a_v.astype(bf16), wv_ref[...].astype(bf16))
        s = _dot_3pass(ck, q_ref[0, g], mm=_dot_nt) * SCALE
        s = jnp.where(visible, s, NEG_INF)
        e = jnp.where(visible, jnp.exp(s - jnp.max(s, axis=0, keepdims=True)), 0.0)
        den = jnp.sum(e, axis=0, keepdims=True)
        p = jnp.where(is_head & (den > 0.0), e / jnp.where(den > 0.0, den, 1.0), 0.0)
        for r in range(GROUP):
            ocmp_ref[0, g, r:r + 1, :] = jnp.sum(p[:, r:r + 1] * cv, axis=0, keepdims=True)
        ocmp_ref[0, g, GROUP:, :] = jnp.zeros((SUBLANES - GROUP, HEAD_DIM), f32)
        imp = jnp.sum(_dot_mask(overlap, p), axis=1, keepdims=True)
        imp_all = jnp.where(lane == g, imp, imp_all)
    imp_ref[0] = imp_all


def _cmp_sample(q, ak, av, wk, wv, *, n_batch, nrows, nblk, pos, name):
    q_spec = pl.BlockSpec((1, N_KV_HEADS, SUBLANES, HEAD_DIM), lambda b: (b, 0, 0, 0))
    a_spec = pl.BlockSpec((nrows * SUBLANES, HEAD_DIM), lambda b: (b, 0))
    w_spec = pl.BlockSpec((HEAD_DIM, HEAD_DIM), lambda b: (0, 0))
    return pl.pallas_call(
        functools.partial(_cmp_sample_kernel, nrows=nrows, nblk=nblk, pos=pos),
        grid=(n_batch,),
        in_specs=[q_spec, a_spec, a_spec, w_spec, w_spec],
        out_specs=[q_spec, pl.BlockSpec((1, nblk, SUBLANES), lambda b: (b, 0, 0))],
        out_shape=[jax.ShapeDtypeStruct((n_batch, N_KV_HEADS, SUBLANES, HEAD_DIM), f32),
                   jax.ShapeDtypeStruct((n_batch, nblk, SUBLANES), f32)],
        compiler_params=_params("parallel"),
        name=name,
    )(q, ak, av, wk, wv)


def _rank_sample_kernel(imp_ref, idx_ref, score_ref, *, nblk, qb):
    blk = lax.broadcasted_iota(jnp.int32, (nblk, 1), 0)
    causal = blk <= qb
    forced = (blk == 0) | (blk == qb) | (blk == qb - 1)
    score = jnp.where(causal, imp_ref[...] + FORCE_BONUS * forced.astype(f32), -jnp.inf)
    rank = _stable_rank(score, score_ref, blk, nblk)
    for slot in range(N_SEL):
        hit = causal & (rank == float(slot))
        idx_ref[slot:slot + 1, :] = jnp.sum(jnp.where(hit, blk, 0), axis=0, keepdims=True)
        idx_ref[N_SEL + slot:N_SEL + slot + 1, :] = jnp.max(hit.astype(jnp.int32), axis=0, keepdims=True)


def _rank_sample(imp_t, *, nblk, qb, name):
    lanes = imp_t.shape[1]
    return pl.pallas_call(
        functools.partial(_rank_sample_kernel, nblk=nblk, qb=qb),
        grid=(1,),
        in_specs=[pl.BlockSpec((nblk, lanes), lambda i: (0, 0))],
        out_specs=pl.BlockSpec((2 * N_SEL, lanes), lambda i: (0, 0)),
        out_shape=jax.ShapeDtypeStruct((2 * N_SEL, lanes), jnp.int32),
        scratch_shapes=[pltpu.VMEM((nblk, lanes), f32)],
        compiler_params=_params("arbitrary"),
        name=name,
    )(imp_t)


def _softmax_pv(s, v, exp=jnp.exp):
    e = exp(s - jnp.max(s, axis=-1, keepdims=True))
    den = jnp.sum(e, axis=-1, keepdims=True)
    return _dot(e.astype(bf16), v) / den


LOG2_E = 1.4426950408889634


def _masked_attention(q, k, v, mask, tq):
    heads = q.shape[0] // tq
    keys = k.shape[0]
    s = _dot_nt(q, k)
    s = jnp.where(mask[None], s.reshape(heads, tq, keys), NEG_INF).reshape(heads * tq, keys)
    return _softmax_pv(s, v, exp=jnp.exp2)


HEADS_PER_CHAIN = 2


def _nsa_prompt_kernel(q_ref, c_ref, sa_ref, sb_ref, ks_ref, vs_ref, kw_ref, vw_ref, sel_ref, ocmp_ref,
                       gate_ref, o_ref, *, tq, seq, nblk, span, n_q):
    qi = pl.program_id(2)
    t0 = qi * tq
    qr = (_rope(q_ref[...], c_ref[...], sa_ref[...], sb_ref[...]) * (SCALE * LOG2_E)).astype(bf16)
    n_chain = GROUP // HEADS_PER_CHAIN
    chains = [jnp.concatenate([qr[:, (c * HEADS_PER_CHAIN + h) * HEAD_DIM:(c * HEADS_PER_CHAIN + h + 1) * HEAD_DIM]
                               for h in range(HEADS_PER_CHAIN)], axis=0) for c in range(n_chain)]
    tpos = t0 + lax.broadcasted_iota(jnp.int32, (tq, 1), 0)

    start = pl.multiple_of(jnp.maximum(t0 - WINDOW, 0), tq)
    wpos = start + lax.broadcasted_iota(jnp.int32, (1, span), 1)
    wmask = (wpos <= tpos) & (wpos > tpos - WINDOW)
    kw = kw_ref[pl.ds(start, span), :].astype(bf16)
    vw = vw_ref[pl.ds(start, span), :].astype(bf16)
    o_win = [_masked_attention(qc, kw, vw, wmask, tq) for qc in chains]

    gates = _sigmoid(gate_ref[...])
    sel = sel_ref[0, 0].astype(bf16)

    tiles_per_class = 2
    for cls in range(-(-n_q // tiles_per_class)):
        ext = min(seq, (cls + 1) * tiles_per_class * tq)

        @pl.when(qi // tiles_per_class == cls)
        def _(ext=ext):
            kpos = lax.broadcasted_iota(jnp.int32, (1, ext), 1)
            expand = (kpos // L_SEL == lax.broadcasted_iota(jnp.int32, (nblk, 1), 0)).astype(bf16)
            mask = (_dot(sel, expand) > 0.5) & (kpos <= tpos)
            k = ks_ref[0:ext, :].astype(bf16)
            v = vs_ref[0:ext, :].astype(bf16)
            for c, qc in enumerate(chains):
                o_slc = _masked_attention(qc, k, v, mask, tq)
                for h in range(HEADS_PER_CHAIN):
                    r = c * HEADS_PER_CHAIN + h
                    sl = slice(r * HEAD_DIM, (r + 1) * HEAD_DIM)
                    rows = slice(h * tq, (h + 1) * tq)
                    o = (gates[:, 3 * r:3 * r + 1] * ocmp_ref[:, sl] + gates[:, 3 * r + 1:3 * r + 2] * o_slc[rows]
                         + gates[:, 3 * r + 2:3 * r + 3] * o_win[c][rows])
                    o_ref[:, sl] = o.astype(o_ref.dtype)


def _nsa_prompt(z1, tables, ks, vs, kw, vw, sel, ocmp, gates, *, batch, seq, nblk, name):
    tq = _tile(seq, (256, 128))
    n_q = seq // tq
    span = min(WINDOW + tq, seq)
    row_blk = lambda w: pl.BlockSpec((tq, w), lambda b, g, i: (b * n_q + i, g))
    tab = pl.BlockSpec((tq, GROUP_W), lambda b, g, i: (i, 0))
    kv = pl.BlockSpec((seq, HEAD_DIM), lambda b, g, i: (b, g))
    return pl.pallas_call(
        functools.partial(_nsa_prompt_kernel, tq=tq, seq=seq, nblk=nblk, span=span, n_q=n_q),
        grid=(batch, N_KV_HEADS, n_q),
        in_specs=[row_blk(GROUP_W), tab, tab, tab, kv, kv, kv, kv,
                  pl.BlockSpec((1, 1, tq, nblk), lambda b, g, i: (b, g, i, 0)),
                  row_blk(GROUP_W),
                  pl.BlockSpec((None, tq, 3 * GROUP), lambda b, g, i: (g, b * n_q + i, 0))],
        out_specs=row_blk(GROUP_W),
        out_shape=jax.ShapeDtypeStruct((batch * seq, Q_W), bf16),
        compiler_params=_params("parallel", "parallel", "arbitrary"),
        name=name,
    )(z1, *tables, ks, vs, kw, vw, sel, ocmp, gates)


def _nsa_sample_kernel(idx_ref, pt_ref, q_ref, c_ref, sa_ref, sb_ref, ksn_ref, vsn_ref, kwin_ref, vwin_ref,
                       ocmp_ref, gate_ref, poolk_ref, poolv_ref, o_ref, kbuf, vbuf, sems, *,
                       n_pages, page, pos, window):
    b = pl.program_id(0)
    per_page = page // L_SEL
    past_blocks = n_pages * per_page
    n_keys = N_SEL * L_SEL

    def block_copies(g, slot):
        blk = idx_ref[(b * N_KV_HEADS + g) * 2 * N_SEL + slot]
        jp = jnp.minimum(blk, past_blocks - 1)
        phys = pt_ref[b * n_pages + jp // per_page]
        off = pl.multiple_of((jp % per_page) * L_SEL, L_SEL)
        dst = pl.ds(slot * L_SEL, L_SEL)
        return (pltpu.make_async_copy(poolk_ref.at[phys, pl.ds(off, L_SEL), g], kbuf.at[g, dst], sems.at[0]),
                pltpu.make_async_copy(poolv_ref.at[phys, pl.ds(off, L_SEL), g], vbuf.at[g, dst], sems.at[1]))

    for g in range(N_KV_HEADS):
        for slot in range(N_SEL):
            for cp in block_copies(g, slot):
                cp.start()
    for g in range(N_KV_HEADS):
        for slot in range(N_SEL):
            for cp in block_copies(g, slot):
                cp.wait()

    lane = lax.broadcasted_iota(jnp.int32, (1, n_keys), 1)
    lane_slot = lane // L_SEL
    for g in range(N_KV_HEADS):
        qr = _rope(q_ref[0, g], c_ref[...], sa_ref[...], sb_ref[...]).astype(bf16)
        blk_vec = jnp.zeros((1, n_keys), jnp.int32)
        ok_vec = jnp.zeros((1, n_keys), jnp.int32)
        base = (b * N_KV_HEADS + g) * 2 * N_SEL
        for slot in range(N_SEL):
            blk_vec = jnp.where(lane_slot == slot, idx_ref[base + slot], blk_vec)
            ok_vec = jnp.where(lane_slot == slot, idx_ref[base + N_SEL + slot], ok_vec)
        in_past = blk_vec < past_blocks
        kpos = blk_vec * L_SEL + lane % L_SEL
        mask = (ok_vec > 0) & (kpos <= pos)
        k_new = ksn_ref[0, g:g + 1, :].astype(bf16).astype(f32)
        v_new = vsn_ref[0, g:g + 1, :].astype(bf16).astype(f32)
        s_new = jnp.sum(qr.astype(f32) * k_new, axis=-1, keepdims=True)
        s = jnp.where(in_past, _dot_nt(qr, kbuf[g].astype(bf16)), s_new) * SCALE
        s = jnp.where(mask, s, NEG_INF)
        e = jnp.exp(s - jnp.max(s, axis=-1, keepdims=True))
        den = jnp.sum(e, axis=-1, keepdims=True)
        e_new = jnp.sum(jnp.where(in_past, 0.0, e), axis=-1, keepdims=True)
        e_past = jnp.where(in_past, e, 0.0).astype(bf16)
        o_slc = (_dot(e_past, vbuf[g].astype(bf16)) + e_new.astype(bf16).astype(f32) * v_new) / den
        win_rows = pl.ds(g, window, stride=N_KV_HEADS)
        o_win = _softmax_pv(_dot_nt(qr, kwin_ref[win_rows, :].astype(bf16)) * SCALE,
                            vwin_ref[win_rows, :].astype(bf16))
        gt = _sigmoid(gate_ref[0, g])
        o_ref[0, g] = gt[:, 0:1] * ocmp_ref[0, g] + gt[:, 1:2] * o_slc + gt[:, 2:3] * o_win


def _nsa_sample(idx, page_table, q, tables, ks_new, vs_new, kwin, vwin, ocmp, gates, pool_k, pool_v, *,
                pos, name):
    n_batch, n_pages = page_table.shape
    page = pool_k.shape[1]
    window = kwin.shape[0] // (n_batch * N_KV_HEADS)
    tab = pl.BlockSpec((SUBLANES, HEAD_DIM), lambda b, *_: (0, 0))
    heads = pl.BlockSpec((1, N_KV_HEADS, SUBLANES, HEAD_DIM), lambda b, *_: (b, 0, 0, 0))
    new_row = pl.BlockSpec((1, N_KV_HEADS, HEAD_DIM), lambda b, *_: (b, 0, 0))
    win = pl.BlockSpec((window * N_KV_HEADS, HEAD_DIM), lambda b, *_: (b, 0))
    return pl.pallas_call(
        functools.partial(_nsa_sample_kernel, n_pages=n_pages, page=page, pos=pos, window=window),
        grid_spec=pltpu.PrefetchScalarGridSpec(
            num_scalar_prefetch=2,
            grid=(n_batch,),
            in_specs=[heads, tab, tab, tab, new_row, new_row, win, win, heads, heads,
                      pl.BlockSpec(memory_space=pl.ANY), pl.BlockSpec(memory_space=pl.ANY)],
            out_specs=heads,
            scratch_shapes=[pltpu.VMEM((N_KV_HEADS, N_SEL * L_SEL, HEAD_DIM), f32),
                            pltpu.VMEM((N_KV_HEADS, N_SEL * L_SEL, HEAD_DIM), f32),
                            pltpu.SemaphoreType.DMA((2,))],
        ),
        out_shape=jax.ShapeDtypeStruct((n_batch, N_KV_HEADS, SUBLANES, HEAD_DIM), f32),
        compiler_params=_params("arbitrary"),
        name=name,
    )(idx, page_table.reshape(-1), q, *tables, ks_new, vs_new, kwin, vwin, ocmp, gates, pool_k, pool_v)


def _window_update_kernel(kin_ref, vin_ref, knew_ref, vnew_ref, kout_ref, vout_ref):
    rows = kin_ref.shape[1]
    keep = rows - N_KV_HEADS
    for src, new, dst in ((kin_ref, knew_ref, kout_ref), (vin_ref, vnew_ref, vout_ref)):
        dst[0, 0:keep, :] = src[0, N_KV_HEADS:rows, :]
        dst[0, keep:rows, :] = new[0]


def _window_update(win_k, win_v, k_new, v_new, name):
    n_batch, rows, d = win_k.shape
    win = pl.BlockSpec((1, rows, d), lambda b: (b, 0, 0))
    new = pl.BlockSpec((1, N_KV_HEADS, d), lambda b: (b, 0, 0))
    shape = jax.ShapeDtypeStruct(win_k.shape, win_k.dtype)
    return pl.pallas_call(
        _window_update_kernel,
        grid=(n_batch,),
        in_specs=[win, win, new, new],
        out_specs=[win, win],
        out_shape=[shape, shape],
        compiler_params=_params("parallel"),
        name=name,
    )(win_k, win_v, k_new, v_new)


def _sgu_prompt_kernel(u_ref, v_ref, g_ref, w_ref, bt_ref, o_ref):
    v = v_ref[...].astype(f32)
    vn = (v * lax.rsqrt(jnp.mean(v * v, axis=-1, keepdims=True) + EPS) * g_ref[...]).astype(bf16)
    tril = (lax.broadcasted_iota(jnp.int32, (SG_CHUNK, SG_CHUNK), 0)
            >= lax.broadcasted_iota(jnp.int32, (SG_CHUNK, SG_CHUNK), 1))
    bt = bt_ref[...]
    group_dim = v.shape[-1] // SG_GROUPS
    for gi in range(SG_GROUPS):
        sl = slice(gi * group_dim, (gi + 1) * group_dim)
        w = jnp.where(tril, w_ref[gi], 0.0).astype(bf16)
        mixed = _dot(w, vn[:, sl]) + bt[:, gi:gi + 1]
        o_ref[:, sl] = (u_ref[:, sl].astype(f32) * mixed).astype(o_ref.dtype)


def _sgu_prompt(uv, norm_g, w_s, b_s, name):
    m, two_w = uv.shape
    width = two_w // 2
    return pl.pallas_call(
        _sgu_prompt_kernel,
        grid=(m // SG_CHUNK,),
        in_specs=[pl.BlockSpec((SG_CHUNK, width), lambda i: (i, 0)),
                  pl.BlockSpec((SG_CHUNK, width), lambda i: (i, 1)),
                  pl.BlockSpec((1, width), lambda i: (0, 0)),
                  pl.BlockSpec((SG_GROUPS, SG_CHUNK, SG_CHUNK), lambda i: (0, 0, 0)),
                  pl.BlockSpec((SG_CHUNK, SG_GROUPS), lambda i: (0, 0))],
        out_specs=pl.BlockSpec((SG_CHUNK, width), lambda i: (i, 0)),
        out_shape=jax.ShapeDtypeStruct((m, width), bf16),
        compiler_params=_params("parallel"),
        name=name,
    )(uv, uv, norm_g.reshape(1, width), w_s, b_s.T)


def _sgu_sample_kernel(u_ref, v_ref, g_ref, w0_ref, b0_ref, o_ref, vn_ref):
    v = v_ref[...]
    vn = v * lax.rsqrt(jnp.mean(v * v, axis=-1, keepdims=True) + EPS) * g_ref[...]
    vn_ref[...] = vn
    o_ref[...] = (u_ref[...] * (vn * w0_ref[...] + b0_ref[...])).astype(o_ref.dtype)


def _sgu_sample(uv, norm_g, w_s, b_s, name):
    m, two_w = uv.shape
    width = two_w // 2
    group_dim = width // SG_GROUPS
    w0 = jnp.repeat(w_s[:, 0, 0], group_dim).reshape(1, width)
    b0 = jnp.repeat(b_s[:, 0], group_dim).reshape(1, width)
    row = pl.BlockSpec((1, width), lambda i: (0, 0))
    return pl.pallas_call(
        _sgu_sample_kernel,
        grid=(1,),
        in_specs=[pl.BlockSpec((m, width), lambda i: (0, 0)), pl.BlockSpec((m, width), lambda i: (0, 1)),
                  row, row, row],
        out_specs=[pl.BlockSpec((m, width), lambda i: (0, 0))] * 2,
        out_shape=[jax.ShapeDtypeStruct((m, width), bf16), jax.ShapeDtypeStruct((m, width), f32)],
        compiler_params=_params("arbitrary"),
        name=name,
    )(uv, uv, norm_g.reshape(1, width), w0, b0)


CONV_HALO = SUBLANES


def _ffn_up_kernel(xp_ref, xs_ref, wa_ref, wb_ref, cwa_ref, cwb_ref, cba_ref, cbb_ref,
                   h0a_ref, h0b_ref, h1a_ref, h1b_ref,
                   act_ref, ta_ref, tb_ref, acts_ref, upa_ref, upb_ref,
                   wabf_ref, wbbf_ref, bufa, bufb, *, tm, tiles_per_seq, n_m):
    mi = pl.program_id(1)

    @pl.when(mi == 0)
    def _():
        _cast_weight_tile(wa_ref, wabf_ref)
        _cast_weight_tile(wb_ref, wbbf_ref)

    @pl.when(mi % tiles_per_seq == 0)
    def _():
        bufa[...] = jnp.zeros_like(bufa)
        bufb[...] = jnp.zeros_like(bufb)

    x = xp_ref[...]
    halves = []
    for wbf_ref, cw_ref, cb_ref, halo_ref, tail_ref in ((wabf_ref, cwa_ref, cba_ref, bufa, ta_ref),
                                                        (wbbf_ref, cwb_ref, cbb_ref, bufb, tb_ref)):
        up = _dot(x, wbf_ref[...])
        w0, w1, w2, cb = cw_ref[0:1, :], cw_ref[1:2, :], cw_ref[2:3, :], cb_ref[...]
        t = up * w1 + pltpu.roll(up * w0, 1, 0)
        conv = cb + up * w2 + pltpu.roll(t, 1, 0)
        ext = jnp.concatenate([halo_ref[...], up[0:CONV_HALO, :]], axis=0)
        lo = CONV_HALO
        head = (cb + ext[lo:2 * lo, :] * w2 + ext[lo - 1:2 * lo - 1, :] * w1 + ext[lo - 2:2 * lo - 2, :] * w0)
        halves.append(jnp.concatenate([head, conv[CONV_HALO:, :]], axis=0))
        tail_ref[0] = up[tm - (CONV_W - 1):, :]
        halo_ref[...] = up[tm - CONV_HALO:, :]
    act_ref[...] = (_silu(halves[0]) * halves[1]).astype(act_ref.dtype)

    @pl.when(mi == n_m - 1)
    def _():
        xs = xs_ref[...]
        halves = []
        for wbf_ref, cw_ref, cb_ref, h0_ref, h1_ref, up_ref in (
                (wabf_ref, cwa_ref, cba_ref, h0a_ref, h1a_ref, upa_ref),
                (wbbf_ref, cwb_ref, cbb_ref, h0b_ref, h1b_ref, upb_ref)):
            up = _dot(xs, wbf_ref[...])
            up_ref[...] = up
            conv = cb_ref[...] + h0_ref[...] * cw_ref[0:1, :]
            conv = conv + h1_ref[...] * cw_ref[1:2, :]
            halves.append(conv + up * cw_ref[2:3, :])
        acts_ref[...] = (_silu(halves[0]) * halves[1]).astype(acts_ref.dtype)


def _ffn_up(hn_p, hn_s, w_up, conv_w, conv_b, hist0, hist1, batch, seq, name):
    mp, d = hn_p.shape
    ms = hn_s.shape[0]
    d_ff = w_up.shape[1] // 2
    tm = _tile(seq, (1024, 512, 256, 128))
    tn = _tile(d_ff, (256, 128))
    nb = d_ff // tn
    n_m = mp // tm
    tiles_per_seq = seq // tm
    col = lambda shape, off: pl.BlockSpec(shape, lambda j, i: (0, j + off))
    tail = pl.BlockSpec((1, CONV_W - 1, tn), lambda j, i: (i // tiles_per_seq, 0, j))
    tail_shape = jax.ShapeDtypeStruct((batch, CONV_W - 1, d_ff), f32)
    up_shape = jax.ShapeDtypeStruct((ms, d_ff), f32)
    buf = pltpu.VMEM((CONV_HALO, tn), f32)
    return pl.pallas_call(
        functools.partial(_ffn_up_kernel, tm=tm, tiles_per_seq=tiles_per_seq, n_m=n_m),
        grid=(nb, n_m),
        in_specs=[pl.BlockSpec((tm, d), lambda j, i: (i, 0)), pl.BlockSpec((ms, d), lambda j, i: (0, 0)),
                  col((d, tn), 0), col((d, tn), nb),
                  col((CONV_W, tn), 0), col((CONV_W, tn), nb), col((1, tn), 0), col((1, tn), nb),
                  col((ms, tn), 0), col((ms, tn), nb), col((ms, tn), 0), col((ms, tn), nb)],
        out_specs=[pl.BlockSpec((tm, tn), lambda j, i: (i, j)), tail, tail,
                   col((ms, tn), 0), col((ms, tn), 0), col((ms, tn), 0)],
        out_shape=[jax.ShapeDtypeStruct((mp, d_ff), bf16), tail_shape, tail_shape,
                   jax.ShapeDtypeStruct((ms, d_ff), bf16), up_shape, up_shape],
        scratch_shapes=[pltpu.VMEM((d, tn), bf16), pltpu.VMEM((d, tn), bf16), buf, buf],
        compiler_params=_params("parallel", "arbitrary"),
        name=name,
    )(hn_p, hn_s, w_up, w_up, conv_w, conv_w, conv_b, conv_b, hist0, hist0, hist1, hist1)


def kernel(x_prompt, x_sample, cache_k_cmp, cache_v_cmp, cache_k_slc, cache_v_slc, cache_k_win, cache_v_win,
           state_ffn_conv, page_table, norm1_g, w_in, cmp_pool_k, cmp_bias_k, cmp_w_k, cmp_pool_v, cmp_bias_v,
           cmp_w_v, sg_norm_g, sg_w, sg_b, w_proj_a, w_proj_b, w_out, norm2_g, w_up, conv_w, conv_b, w_down,
           norm_f_g):
    batch, seq, d_model = x_prompt.shape
    n_dec, dec_seq, _ = x_sample.shape
    depth = w_in.shape[0]
    page = cache_k_cmp.shape[2]
    past = page_table.shape[1] * page
    assert dec_seq == 1 and cache_k_win.shape[2] == WINDOW and past >= WINDOW
    assert seq % SG_CHUNK == 0 and seq % L_SEL == 0 and past % L_SEL == 0
    nblk_p = max(seq // L_SEL, N_SEL)
    nblk_s = -(-max(-(-(past + 1) // L_SEL), N_SEL) // SUBLANES) * SUBLANES

    tables_p = _rope_tables(jnp.arange(seq), GROUP)
    tables_s_kv = _rope_tables(jnp.full((n_dec,), past), GROUP)
    tables_s_q = _rope_tables(jnp.full((SUBLANES,), past), 1)

    hp = x_prompt.reshape(batch * seq, d_model)
    hs = x_sample.reshape(n_dec, d_model)
    new_p = [[] for _ in range(7)]
    new_s = [[] for _ in range(8)]
    for l in range(depth):
        pwk, bk, pwv, bv = cmp_pool_k[l], cmp_bias_k[l], cmp_pool_v[l], cmp_bias_v[l]
        cw = conv_w[l]
        cb = conv_b[l].reshape(1, -1)

        xn_p = _rmsnorm(hp, norm1_g[l], bf16, "norm1_p")
        xn_s = _rmsnorm(hs, norm1_g[l], bf16, "norm1_s")
        w_in_nk = jnp.swapaxes(w_in[l], 0, 1)
        z1_p, z1_s = _matmul_wres(xn_p, xn_s, w_in_nk, Z1_W, (f32, f32), "in_proj_qkv", w_is_nk=True)
        uv_p, uv_s = _matmul_wres(xn_p, xn_s, w_in_nk, d_model, (bf16, f32), "in_proj_uv", w_is_nk=True,
                                  col0=QKV_W + NSA_GATE_W, epilogue=_gelu_tanh)
        gm_p, gm_s = _matmul_wres(xn_p, xn_s, w_in_nk, 2 * d_model, (bf16, bf16), "in_proj_gate", w_is_nk=True,
                                  col0=QKV_W + NSA_GATE_W + d_model, epilogue=_sigmoid)

        kc_p, vc_p, ks_p, vs_p, kw_p, vw_p, ksb, vsb, kwb, vwb = _prep_kv(z1_p, tables_p, seq, "prep_p")
        ak, av = _compress_prompt(kc_p, vc_p, pwk, bk, pwv, bv, batch, seq, "compress_p")
        ocmp, sel = _cmp_select_prompt(z1_p, ak, av, cmp_w_k[l], cmp_w_v[l], batch=batch, seq=seq, nblk=nblk_p,
                                       name="cmp_select_p")
        gates = z1_p[:, QKV_W:QKV_W + NSA_GATE_W].reshape(batch * seq, N_KV_HEADS, 3 * GROUP).transpose(1, 0, 2)
        nsa_p = _nsa_prompt(z1_p, tables_p, ksb, vsb, kwb, vwb, sel, ocmp, gates, batch=batch, seq=seq,
                            nblk=nblk_p, name="nsa_p")
        sg_p = _sgu_prompt(uv_p, sg_norm_g[l], sg_w[l], sg_b[l], "sgu_p")

        kc_s, vc_s, ks_s, vs_s, kw_s, vw_s = _prep_kv(z1_s, tables_s_kv, n_dec, "prep_s")[:6]
        ak, av = _compress_paged(cache_k_cmp[l], cache_v_cmp[l], page_table, pwk, bk, pwv, bv, "compress_s")
        heads_as_rows = lambda t, w: jnp.pad(t.reshape(n_dec, N_KV_HEADS, GROUP, w),
                                             ((0, 0), (0, 0), (0, SUBLANES - GROUP), (0, HEAD_DIM - w)))
        q_s = heads_as_rows(z1_s[:, :Q_W], HEAD_DIM)
        ocmp, imp = _cmp_sample(q_s, ak, av, cmp_w_k[l], cmp_w_v[l], n_batch=n_dec, nrows=past // STRIDE,
                                nblk=nblk_s, pos=past, name="cmp_s")
        imp_t = imp.transpose(1, 0, 2).reshape(nblk_s, n_dec * SUBLANES)
        idx = _rank_sample(imp_t, nblk=nblk_s, qb=past // L_SEL, name="rank_s")
        idx = idx.T.reshape(n_dec, SUBLANES, 2 * N_SEL)[:, :N_KV_HEADS].reshape(-1)
        new_rows = lambda t: t.reshape(n_dec, N_KV_HEADS, HEAD_DIM)
        cache_rows = lambda t: t.reshape(n_dec, WINDOW * N_KV_HEADS, HEAD_DIM)
        win_k, win_v = _window_update(cache_rows(cache_k_win[l]), cache_rows(cache_v_win[l]), new_rows(kw_s),
                                      new_rows(vw_s), "window_s")
        gates = heads_as_rows(z1_s[:, QKV_W:QKV_W + NSA_GATE_W], 3)
        nsa_s = _nsa_sample(idx, page_table, q_s, tables_s_q, new_rows(ks_s), new_rows(vs_s),
                            win_k.reshape(-1, HEAD_DIM), win_v.reshape(-1, HEAD_DIM), ocmp, gates,
                            cache_k_slc[l], cache_v_slc[l], pos=past, name="nsa_s")
        nsa_s = nsa_s[:, :, :GROUP].reshape(n_dec, Q_W).astype(bf16)
        sg_s, sg_v = _sgu_sample(uv_s, sg_norm_g[l], sg_w[l], sg_b[l], "sgu_s")

        m_p, m_s = _merge(nsa_p, nsa_s, sg_p, sg_s, w_proj_a[l], w_proj_b[l], gm_p, gm_s, "merge")
        h_p, h_s = _matmul_wres(m_p, m_s, w_out[l], d_model, (f32, f32), "out_proj",
                                epilogue=lambda acc, x: x + acc, extras_p=(hp,), extras_s=(hs,))
        hn_p = _rmsnorm(h_p, norm2_g[l], bf16, "norm2_p")
        hn_s = _rmsnorm(h_s, norm2_g[l], bf16, "norm2_s")
        hist = state_ffn_conv[l]
        act_p, tail_a, tail_b, act_s, up_a, up_b = _ffn_up(hn_p, hn_s, w_up[l], cw, cb, hist[:, 0], hist[:, 1],
                                                           batch, seq, "ffn_up")
        wdown = w_down[l].astype(bf16)
        hp = _matmul_residual(act_p, wdown, h_p, "ffn_down_p")
        hs = _matmul_residual(act_s, wdown, h_s, "ffn_down_s")

        kv5 = lambda t: t.reshape(batch, seq, N_KV_HEADS, HEAD_DIM)
        keep_p = min(WINDOW, seq)
        for lst, val in zip(new_p, (kv5(kc_p), kv5(vc_p), kv5(ks_p), kv5(vs_p), kv5(kw_p)[:, seq - keep_p:],
                                    kv5(vw_p)[:, seq - keep_p:], jnp.concatenate([tail_a, tail_b], axis=-1))):
            lst.append(val)
        kv5 = lambda t: t.reshape(n_dec, 1, N_KV_HEADS, HEAD_DIM)
        conv_s = jnp.stack([hist[:, 1], jnp.concatenate([up_a, up_b], axis=-1)], axis=1)
        win5 = lambda t: t.reshape(n_dec, WINDOW, N_KV_HEADS, HEAD_DIM)
        for lst, val in zip(new_s, (kv5(kc_s), kv5(vc_s), kv5(ks_s), kv5(vs_s), win5(win_k), win5(win_v),
                                    sg_v.reshape(n_dec, 1, -1), conv_s)):
            lst.append(val)

    y_prompt = _rmsnorm(hp, norm_f_g, f32, "norm_f_p").reshape(batch, seq, d_model)
    y_sample = _rmsnorm(hs, norm_f_g, f32, "norm_f_s").reshape(n_dec, 1, d_model)
    return (y_prompt, y_sample, *(jnp.stack(v) for v in new_p), *(jnp.stack(v) for v in new_s))
```

```python
import functools

import jax
import jax.numpy as jnp
from jax import lax
from jax.experimental import pallas as pl
from jax.experimental.pallas import tpu as pltpu

f32 = jnp.float32
bf16 = jnp.bfloat16

N_HEADS = 16
N_KV_HEADS = 4
GROUP = N_HEADS // N_KV_HEADS
HEAD_DIM = 128
ROT_DIM = HEAD_DIM // 4
ROT_HALF = ROT_DIM // 2
ROPE_THETA = 500000.0
L_CMP = 32
STRIDE = 16
L_SEL = 64
N_SEL = 16
WINDOW = 512
FORCE_BONUS = 1000.0
SCALE = HEAD_DIM ** -0.5
NEG_INF = -1e30
SG_GROUPS = 16
SG_CHUNK = 128
CONV_W = 3
EPS = 1e-6

SUBLANES = 8
LANES = 128
GROUP_W = GROUP * HEAD_DIM
Q_W = N_HEADS * HEAD_DIM
KV_W = N_KV_HEADS * HEAD_DIM
NSA_GATE_W = 3 * N_HEADS
QKV_W = Q_W + 6 * KV_W
Z1_W = QKV_W + GROUP_W
COL_KC, COL_VC, COL_KS, COL_VS, COL_KW, COL_VW, COL_GATE = (Q_W // GROUP_W + i for i in range(7))

VMEM_LIMIT_BYTES = 56 * 1024 * 1024


def _params(*semantics):
    return pltpu.CompilerParams(dimension_semantics=semantics, vmem_limit_bytes=VMEM_LIMIT_BYTES)


def _tile(n, candidates):
    for c in candidates:
        if n % c == 0:
            return c
    return n


def _sigmoid(x):
    return 0.5 * jnp.tanh(0.5 * x) + 0.5


def _silu(x):
    return x * _sigmoid(x)


def _gelu_tanh(x):
    return x * (0.5 * (1.0 + jnp.tanh(0.7978845608028654 * (x + 0.044715 * (x * x * x)))))


def _dot(a, b, precision=None):
    return jnp.dot(a, b, preferred_element_type=f32, precision=precision)


def _dot_nt(a, b, precision=None):
    return lax.dot_general(a, b, (((1,), (1,)), ((), ())), precision=precision, preferred_element_type=f32)


def _split2(x):
    hi = x.astype(bf16)
    return hi, (x - hi.astype(f32)).astype(bf16)


def _dot_3pass(a, b, mm=_dot):
    a_hi, a_lo = a if isinstance(a, tuple) else _split2(a)
    b_hi, b_lo = b if isinstance(b, tuple) else _split2(b)
    return mm(a_hi, b_hi) + (mm(a_hi, b_lo) + mm(a_lo, b_hi))


def _dot_mask(m, x, mm=_dot):
    hi = x.astype(bf16)
    mid, lo = _split2(x - hi.astype(f32))
    return mm(m, hi) + (mm(m, mid) + mm(m, lo))


def _rope(x, c, sa, sb):
    lanes = x.shape[-1]
    return x * c + pltpu.roll(x, lanes - ROT_HALF, 1) * sa + pltpu.roll(x, ROT_HALF, 1) * sb


def _rope_tables(pos, reps):
    n = pos.shape[0]
    inv_freq = jnp.power(jnp.float32(ROPE_THETA), -jnp.arange(ROT_HALF, dtype=f32) / ROT_HALF)
    ang = pos.astype(f32)[:, None] * inv_freq[None, :]
    cos, sin = jnp.cos(ang), jnp.sin(ang)
    zeros = lambda w: jnp.zeros((n, w), f32)
    c = jnp.concatenate([cos, cos, jnp.ones((n, HEAD_DIM - ROT_DIM), f32)], axis=1)
    sa = jnp.concatenate([-sin, zeros(HEAD_DIM - ROT_HALF)], axis=1)
    sb = jnp.concatenate([zeros(ROT_HALF), sin, zeros(HEAD_DIM - ROT_DIM)], axis=1)
    return tuple(jnp.tile(t, (1, reps)) for t in (c, sa, sb))


def _rmsnorm_kernel(x_ref, g_ref, o_ref):
    x = x_ref[...].astype(f32)
    y = x * lax.rsqrt(jnp.mean(x * x, axis=-1, keepdims=True) + EPS)
    o_ref[...] = (y * g_ref[...]).astype(o_ref.dtype)


def _rmsnorm(x, g, out_dtype, name):
    m, d = x.shape
    tm = _tile(m, (256, 128, 64, 32, 16, 8))
    return pl.pallas_call(
        _rmsnorm_kernel,
        grid=(m // tm,),
        in_specs=[pl.BlockSpec((tm, d), lambda i: (i, 0)), pl.BlockSpec((1, d), lambda i: (0, 0))],
        out_specs=pl.BlockSpec((tm, d), lambda i: (i, 0)),
        out_shape=jax.ShapeDtypeStruct((m, d), out_dtype),
        compiler_params=_params("parallel"),
        name=name,
    )(x, g.reshape(1, d).astype(f32))


CAST_ROWS = 128


def _cast_weight_tile(w_ref, wbf_ref):
    rows_total = w_ref.shape[0]
    step = _tile(rows_total, (CAST_ROWS,))
    for r0 in range(0, rows_total, step):
        wbf_ref[r0:r0 + step, :] = w_ref[r0:r0 + step, :].astype(bf16)


def _mm_wres_kernel(ap_ref, as_ref, w_ref, *rest, epilogue, n_extra, n_m, w_is_nk):
    extras_p, extras_s = rest[:n_extra], rest[n_extra:2 * n_extra]
    op_ref, os_ref, wbf_ref = rest[2 * n_extra:]
    mi = pl.program_id(1)
    mm = _dot_nt if w_is_nk else _dot

    @pl.when(mi == 0)
    def _():
        _cast_weight_tile(w_ref, wbf_ref)

    w = wbf_ref[...]
    op_ref[...] = epilogue(mm(ap_ref[...], w), *[e[...] for e in extras_p]).astype(op_ref.dtype)

    @pl.when(mi == n_m - 1)
    def _():
        os_ref[...] = epilogue(mm(as_ref[...], w), *[e[...] for e in extras_s]).astype(os_ref.dtype)


def _matmul_wres(a_p, a_s, w, n_cols, out_dtypes, name, *, w_is_nk=False, col0=0, epilogue=lambda acc: acc,
                 extras_p=(), extras_s=()):
    mp, k = a_p.shape
    ms = a_s.shape[0]
    tm = _tile(mp, (1024, 512, 256, 128))
    tn = _tile(n_cols, (512, 256, 128))
    n_m, n_n = mp // tm, n_cols // tn
    if w_is_nk:
        assert col0 % SUBLANES == 0
        w_spec = pl.BlockSpec((pl.Element(tn), pl.Element(k)),
                              lambda j, i: (pl.multiple_of(col0 + j * tn, SUBLANES), 0))
        wbf_shape = (tn, k)
    else:
        assert col0 % tn == 0
        w_spec = pl.BlockSpec((k, tn), lambda j, i: (0, col0 // tn + j))
        wbf_shape = (k, tn)
    in_specs = [pl.BlockSpec((tm, k), lambda j, i: (i, 0)), pl.BlockSpec((ms, k), lambda j, i: (0, 0)), w_spec]
    in_specs += [pl.BlockSpec((tm, tn), lambda j, i: (i, j)) for _ in extras_p]
    in_specs += [pl.BlockSpec((ms, tn), lambda j, i: (0, j)) for _ in extras_s]
    return pl.pallas_call(
        functools.partial(_mm_wres_kernel, epilogue=epilogue, n_extra=len(extras_p), n_m=n_m, w_is_nk=w_is_nk),
        grid=(n_n, n_m),
        in_specs=in_specs,
        out_specs=[pl.BlockSpec((tm, tn), lambda j, i: (i, j)), pl.BlockSpec((ms, tn), lambda j, i: (0, j))],
        out_shape=[jax.ShapeDtypeStruct((mp, n_cols), out_dtypes[0]),
                   jax.ShapeDtypeStruct((ms, n_cols), out_dtypes[1])],
        scratch_shapes=[pltpu.VMEM(wbf_shape, bf16)],
        compiler_params=_params("parallel", "arbitrary"),
        name=name,
    )(a_p, a_s, w, *extras_p, *extras_s)


def _merge_kernel(ap_ref, as_ref, bp_ref, bs_ref, wa_ref, wb_ref, gap_ref, gbp_ref, gas_ref, gbs_ref,
                  op_ref, os_ref, wabf_ref, wbbf_ref, *, n_m):
    mi = pl.program_id(1)

    @pl.when(mi == 0)
    def _():
        _cast_weight_tile(wa_ref, wabf_ref)
        _cast_weight_tile(wb_ref, wbbf_ref)

    def mix(a_ref, b_ref, ga_ref, gb_ref, o_ref):
        pa = _dot(a_ref[...], wabf_ref[...])
        pb = _dot(b_ref[...], wbbf_ref[...])
        o_ref[...] = (ga_ref[...].astype(f32) * pa + gb_ref[...].astype(f32) * pb).astype(o_ref.dtype)

    mix(ap_ref, bp_ref, gap_ref, gbp_ref, op_ref)

    @pl.when(mi == n_m - 1)
    def _():
        mix(as_ref, bs_ref, gas_ref, gbs_ref, os_ref)


def _merge(nsa_p, nsa_s, sg_p, sg_s, w_pa, w_pb, gates_p, gates_s, name):
    mp, ka = nsa_p.shape
    ms = nsa_s.shape[0]
    kb = sg_p.shape[1]
    d = w_pa.shape[1]
    tm = _tile(mp, (1024, 512, 256, 128))
    tn = _tile(d, (512, 256, 128))
    n_m, n_n = mp // tm, d // tn
    row_p = lambda w: pl.BlockSpec((tm, w), lambda j, i: (i, 0))
    row_s = lambda w: pl.BlockSpec((ms, w), lambda j, i: (0, 0))
    return pl.pallas_call(
        functools.partial(_merge_kernel, n_m=n_m),
        grid=(n_n, n_m),
        in_specs=[row_p(ka), row_s(ka), row_p(kb), row_s(kb),
                  pl.BlockSpec((ka, tn), lambda j, i: (0, j)), pl.BlockSpec((kb, tn), lambda j, i: (0, j)),
                  pl.BlockSpec((tm, tn), lambda j, i: (i, j)), pl.BlockSpec((tm, tn), lambda j, i: (i, j + n_n)),
                  pl.BlockSpec((ms, tn), lambda j, i: (0, j)), pl.BlockSpec((ms, tn), lambda j, i: (0, j + n_n))],
        out_specs=[pl.BlockSpec((tm, tn), lambda j, i: (i, j)), pl.BlockSpec((ms, tn), lambda j, i: (0, j))],
        out_shape=[jax.ShapeDtypeStruct((mp, d), bf16), jax.ShapeDtypeStruct((ms, d), bf16)],
        scratch_shapes=[pltpu.VMEM((ka, tn), bf16), pltpu.VMEM((kb, tn), bf16)],
        compiler_params=_params("parallel", "arbitrary"),
        name=name,
    )(nsa_p, nsa_s, sg_p, sg_s, w_pa, w_pb, gates_p, gates_p, gates_s, gates_s)


def _mm_resid_kernel(a_ref, w_ref, r_ref, o_ref):
    o_ref[...] = r_ref[...] + _dot(a_ref[...], w_ref[...])


def _matmul_residual(a, w, resid, name):
    m, k = a.shape
    n = w.shape[1]
    tm = _tile(m, (512, 256, 128, 64, 32, 16))
    tn = _tile(n, (256, 128))
    return pl.pallas_call(
        _mm_resid_kernel,
        grid=(m // tm, n // tn),
        in_specs=[pl.BlockSpec((tm, k), lambda i, j: (i, 0)), pl.BlockSpec((k, tn), lambda i, j: (0, j)),
                  pl.BlockSpec((tm, tn), lambda i, j: (i, j))],
        out_specs=pl.BlockSpec((tm, tn), lambda i, j: (i, j)),
        out_shape=jax.ShapeDtypeStruct((m, n), f32),
        compiler_params=_params("parallel", "arbitrary"),
        name=name,
    )(a, w, resid)


def _prep_kernel(kc, vc, ks, vs, kw, vw, c_ref, sa_ref, sb_ref, okc, ovc, oks, ovs, okw, ovw,
                 bks, bvs, bkw, bvw, *, tr):
    c, sa, sb = c_ref[...], sa_ref[...], sb_ref[...]
    values = (kc[...], vc[...], _rope(ks[...], c, sa, sb), vs[...], _rope(kw[...], c, sa, sb), vw[...])
    for val, o_ref in zip(values, (okc, ovc, oks, ovs, okw, ovw)):
        for g in range(N_KV_HEADS):
            o_ref[pl.ds(g, tr, stride=N_KV_HEADS), :] = val[:, g * HEAD_DIM:(g + 1) * HEAD_DIM]
    for val, o_ref in zip(values[2:], (bks, bvs, bkw, bvw)):
        o_ref[...] = val.astype(bf16)


def _prep_kv(z1, tables, rows_per_seq, name):
    m = z1.shape[0]
    tr = _tile(rows_per_seq, (256, 128, 64, 32, 16, 8))
    n_tab = tables[0].shape[0] // tr
    col = lambda cb: pl.BlockSpec((tr, KV_W), lambda i, cb=cb: (i, cb))
    tab = pl.BlockSpec((tr, KV_W), lambda i: (i % n_tab, 0))
    out = pl.BlockSpec((tr * N_KV_HEADS, HEAD_DIM), lambda i: (i, 0))
    out_b = pl.BlockSpec((tr, KV_W), lambda i: (i, 0))
    return pl.pallas_call(
        functools.partial(_prep_kernel, tr=tr),
        grid=(m // tr,),
        in_specs=[col(COL_KC), col(COL_VC), col(COL_KS), col(COL_VS), col(COL_KW), col(COL_VW), tab, tab, tab],
        out_specs=[out] * 6 + [out_b] * 4,
        out_shape=[jax.ShapeDtypeStruct((m * N_KV_HEADS, HEAD_DIM), f32)] * 6
        + [jax.ShapeDtypeStruct((m, KV_W), bf16)] * 4,
        compiler_params=_params("parallel"),
        name=name,
    )(z1, z1, z1, z1, z1, z1, *tables)


def _compress_prompt_kernel(k_ref, v_ref, pwk_ref, bk_ref, pwv_ref, bv_ref, ak_ref, av_ref, *, nsub):
    for x_ref, pw_ref, b_ref, o_ref in ((k_ref, pwk_ref, bk_ref, ak_ref), (v_ref, pwv_ref, bv_ref, av_ref)):
        for g in range(N_KV_HEADS):
            first = None
            second = None
            for j in range(STRIDE):
                rows = x_ref[pl.ds(j * N_KV_HEADS + g, nsub, stride=STRIDE * N_KV_HEADS), :]
                fa = rows * pw_ref[j:j + 1, :]
                sa = rows * pw_ref[STRIDE + j:STRIDE + j + 1, :]
                first = fa if first is None else first + fa
                second = sa if second is None else second + sa
            o_ref[0, :, g * HEAD_DIM:(g + 1) * HEAD_DIM] = _silu(pltpu.roll(first, 1, 0) + second + b_ref[...])


def _compress_prompt(kc, vc, pwk, bk, pwv, bv, batch, seq, name):
    nsub = seq // STRIDE
    xspec = pl.BlockSpec((seq * N_KV_HEADS, HEAD_DIM), lambda b: (b, 0))
    pwspec = pl.BlockSpec((L_CMP, HEAD_DIM), lambda b: (0, 0))
    bspec = pl.BlockSpec((1, HEAD_DIM), lambda b: (0, 0))
    ospec = pl.BlockSpec((1, nsub, KV_W), lambda b: (b, 0, 0))
    return pl.pallas_call(
        functools.partial(_compress_prompt_kernel, nsub=nsub),
        grid=(batch,),
        in_specs=[xspec, xspec, pwspec, bspec, pwspec, bspec],
        out_specs=[ospec, ospec],
        out_shape=[jax.ShapeDtypeStruct((batch, nsub, KV_W), f32)] * 2,
        compiler_params=_params("parallel"),
        name=name,
    )(kc, vc, pwk, bk.reshape(1, HEAD_DIM), pwv, bv.reshape(1, HEAD_DIM))


PAGES_PER_CHUNK = 16


def _compress_paged_kernel(pt_ref, wfk_ref, wsk_ref, bk_ref, wfv_ref, wsv_ref, bv_ref, poolk_ref, poolv_ref,
                           ak_ref, av_ref, bufk, bufv, sems, *, n_pages, page_rows, pps):
    b = pl.program_id(0)
    n_chunks = n_pages // pps
    tiles_per_sub = STRIDE * N_KV_HEADS // SUBLANES
    nsub = page_rows // (STRIDE * N_KV_HEADS)

    def copies(seq, chunk, slot):
        out = []
        for p in range(pps):
            phys = pt_ref[seq * n_pages + chunk * pps + p]
            out.append(pltpu.make_async_copy(poolk_ref.at[phys], bufk.at[slot, p], sems.at[0, slot]))
            out.append(pltpu.make_async_copy(poolv_ref.at[phys], bufv.at[slot, p], sems.at[1, slot]))
        return out

    @pl.when(b == 0)
    def _():
        for cp in copies(b, 0, 0):
            cp.start()

    carry = (jnp.zeros((SUBLANES, HEAD_DIM), f32), jnp.zeros((SUBLANES, HEAD_DIM), f32))
    for chunk in range(n_chunks):
        slot = chunk % 2
        if chunk + 1 < n_chunks:
            for cp in copies(b, chunk + 1, 1 - slot):
                cp.start()
        else:
            @pl.when(b + 1 < pl.num_programs(0))
            def _():
                for cp in copies(b + 1, 0, 0):
                    cp.start()
        for cp in copies(b, chunk, slot):
            cp.wait()

        def page_body(p, prev, chunk=chunk, slot=slot):
            out_row0 = pl.multiple_of((chunk * pps + p) * nsub * SUBLANES, nsub * SUBLANES)
            new_prev = []
            for which, (buf, wf_ref, ws_ref, b_ref, o_ref) in enumerate(
                    ((bufk, wfk_ref, wsk_ref, bk_ref, ak_ref), (bufv, wfv_ref, wsv_ref, bv_ref, av_ref))):
                last = prev[which]
                for n in range(nsub):
                    first = None
                    second = None
                    for t in range(tiles_per_sub):
                        x = buf[slot, p, pl.ds((n * tiles_per_sub + t) * SUBLANES, SUBLANES), :]
                        fa = x * wf_ref[t]
                        sa = x * ws_ref[t]
                        first = fa if first is None else first + fa
                        second = sa if second is None else second + sa
                    first = first + pltpu.roll(first, SUBLANES // 2, 0)
                    second = second + pltpu.roll(second, SUBLANES // 2, 0)
                    o_ref[pl.ds(out_row0 + n * SUBLANES, SUBLANES), :] = _silu(last + second + b_ref[...])
                    last = first
                new_prev.append(last)
            return tuple(new_prev)

        carry = lax.fori_loop(0, pps, page_body, carry)


def _compress_paged(pool_k, pool_v, page_table, pwk, bk, pwv, bv, name):
    n_batch, n_pages = page_table.shape
    n_phys, page = pool_k.shape[:2]
    pps = min(PAGES_PER_CHUNK, n_pages // 2)
    assert N_KV_HEADS * 2 == SUBLANES and n_pages % (2 * pps) == 0 and page % STRIDE == 0
    page_rows = page * N_KV_HEADS
    out_rows = n_pages * (page // STRIDE) * SUBLANES

    def tile_weights(pw):
        return jnp.repeat(pw.reshape(STRIDE // 2, 2, HEAD_DIM), N_KV_HEADS, axis=1)

    full3 = pl.BlockSpec((STRIDE // 2, SUBLANES, HEAD_DIM), lambda b, pt: (0, 0, 0))
    brow = pl.BlockSpec((SUBLANES, HEAD_DIM), lambda b, pt: (0, 0))
    o_spec = pl.BlockSpec((out_rows, HEAD_DIM), lambda b, pt: (b, 0))
    out_shape = jax.ShapeDtypeStruct((n_batch * out_rows, HEAD_DIM), f32)
    buf = pltpu.VMEM((2, pps, page_rows, HEAD_DIM), f32)
    return pl.pallas_call(
        functools.partial(_compress_paged_kernel, n_pages=n_pages, page_rows=page_rows, pps=pps),
        grid_spec=pltpu.PrefetchScalarGridSpec(
            num_scalar_prefetch=1,
            grid=(n_batch,),
            in_specs=[full3, full3, brow, full3, full3, brow,
                      pl.BlockSpec(memory_space=pl.ANY), pl.BlockSpec(memory_space=pl.ANY)],
            out_specs=[o_spec, o_spec],
            scratch_shapes=[buf, buf, pltpu.SemaphoreType.DMA((2, 2))],
        ),
        out_shape=[out_shape, out_shape],
        compiler_params=_params("arbitrary"),
        name=name,
    )(page_table.reshape(-1),
      tile_weights(pwk[:STRIDE]), tile_weights(pwk[STRIDE:]), jnp.tile(bk.reshape(1, HEAD_DIM), (SUBLANES, 1)),
      tile_weights(pwv[:STRIDE]), tile_weights(pwv[STRIDE:]), jnp.tile(bv.reshape(1, HEAD_DIM), (SUBLANES, 1)),
      pool_k.reshape(n_phys, page_rows, HEAD_DIM), pool_v.reshape(n_phys, page_rows, HEAD_DIM))


def _block_overlap(blk, rown):
    c_start = (rown - 1) * STRIDE
    return (rown >= 1) & (c_start < blk * L_SEL + L_SEL) & (c_start + L_CMP > blk * L_SEL)


def _stable_rank(score, score_ref, blk, nblk):
    score_ref[...] = score

    def count_better(i, cnt):
        other = score_ref[pl.ds(i, 1), :]
        better = (other > score) | ((other == score) & (i < blk))
        return cnt + better.astype(f32)

    return lax.fori_loop(0, nblk, count_better, jnp.zeros(score.shape, f32))


def _cmp_select_prompt_kernel(q_ref, ak_ref, av_ref, wk_ref, wv_ref, ocmp_ref, sel_ref, score_ref, *,
                              tq, nrows, nblk):
    pos0 = pl.program_id(2) * tq
    pos_col = pos0 + lax.broadcasted_iota(jnp.int32, (tq, 1), 0)
    pos_row = pos0 + lax.broadcasted_iota(jnp.int32, (1, tq), 1)

    ck = _split2(_dot_3pass(ak_ref[0], wk_ref[...]))
    cv = _dot(av_ref[0].astype(bf16), wv_ref[...].astype(bf16)).astype(bf16)

    rown = lax.broadcasted_iota(jnp.int32, (1, nrows), 1)
    visible = (rown >= 1) & (rown * STRIDE + (L_CMP - STRIDE - 1) <= pos_col)

    pg = jnp.zeros((tq, nrows), f32)
    for r in range(GROUP):
        sl = slice(r * HEAD_DIM, (r + 1) * HEAD_DIM)
        s = _dot_3pass(q_ref[:, sl], ck, mm=_dot_nt) * SCALE
        s = jnp.where(visible, s, NEG_INF)
        e = jnp.where(visible, jnp.exp(s - jnp.max(s, axis=-1, keepdims=True)), 0.0)
        den = jnp.sum(e, axis=-1, keepdims=True)
        p = jnp.where(den > 0.0, e / jnp.where(den > 0.0, den, 1.0), 0.0)
        ocmp_ref[:, sl] = _dot(p.astype(bf16), cv)
        pg = pg + p

    blk = lax.broadcasted_iota(jnp.int32, (nblk, 1), 0)
    imp_t = _dot_mask(_block_overlap(blk, rown).astype(bf16), pg, mm=_dot_nt)
    qb = pos_row // L_SEL
    causal = blk <= qb
    forced = (blk == 0) | (blk == qb) | (blk == qb - 1)
    score = jnp.where(causal, imp_t + FORCE_BONUS * forced.astype(f32), -jnp.inf)
    rank = _stable_rank(score, score_ref, blk, nblk)
    chosen = jnp.where(causal & (rank < float(N_SEL)), 1.0, 0.0).astype(bf16)
    eye = (lax.broadcasted_iota(jnp.int32, (tq, tq), 0)
           == lax.broadcasted_iota(jnp.int32, (tq, tq), 1)).astype(bf16)
    sel_ref[0, 0] = _dot_nt(eye, chosen)


def _cmp_select_prompt(q, ak, av, wk, wv, *, batch, seq, nblk, name):
    nrows = ak.shape[1]
    tq = _tile(seq, (512, 256, 128))
    n_q = seq // tq
    q_spec = pl.BlockSpec((tq, GROUP_W), lambda b, g, i: (b * n_q + i, g))
    a_spec = pl.BlockSpec((1, nrows, HEAD_DIM), lambda b, g, i: (b, 0, g))
    w_spec = pl.BlockSpec((HEAD_DIM, HEAD_DIM), lambda b, g, i: (0, 0))
    return pl.pallas_call(
        functools.partial(_cmp_select_prompt_kernel, tq=tq, nrows=nrows, nblk=nblk),
        grid=(batch, N_KV_HEADS, n_q),
        in_specs=[q_spec, a_spec, a_spec, w_spec, w_spec],
        out_specs=[pl.BlockSpec((tq, GROUP_W), lambda b, g, i: (b * n_q + i, g)),
                   pl.BlockSpec((1, 1, tq, nblk), lambda b, g, i: (b, g, i, 0))],
        out_shape=[jax.ShapeDtypeStruct((batch * seq, Q_W), f32),
                   jax.ShapeDtypeStruct((batch, N_KV_HEADS, seq, nblk), f32)],
        scratch_shapes=[pltpu.VMEM((nblk, tq), f32)],
        compiler_params=_params("parallel", "parallel", "arbitrary"),
        name=name,
    )(q, ak, av, wk, wv)


def _cmp_sample_kernel(q_ref, ak_ref, av_ref, wk_ref, wv_ref, ocmp_ref, imp_ref, *, nrows, nblk, pos):
    rown = lax.broadcasted_iota(jnp.int32, (nrows, 1), 0)
    visible = (rown >= 1) & (rown * STRIDE + (L_CMP - STRIDE - 1) <= pos)
    overlap = _block_overlap(lax.broadcasted_iota(jnp.int32, (nblk, 1), 0),
                             lax.broadcasted_iota(jnp.int32, (1, nrows), 1)).astype(bf16)
    lane = lax.broadcasted_iota(jnp.int32, (1, SUBLANES), 1)
    is_head = lane < GROUP
    imp_all = jnp.zeros((nblk, SUBLANES), f32)
    for g in range(N_KV_HEADS):
        a_k = ak_ref[pl.ds(g, nrows, stride=SUBLANES), :]
        a_v = av_ref[pl.ds(g, nrows, stride=SUBLANES), :]
        ck = _dot_3pass(a_k, wk_ref[...])
        cv = _dot(a_v.astype(bf16), wv_ref[...].astype(bf16))
        s = _dot_3pass(ck, q_ref[0, g], mm=_dot_nt) * SCALE
        s = jnp.where(visible, s, NEG_INF)
        e = jnp.where(visible, jnp.exp(s - jnp.max(s, axis=0, keepdims=True)), 0.0)
        den = jnp.sum(e, axis=0, keepdims=True)
        p = jnp.where(is_head & (den > 0.0), e / jnp.where(den > 0.0, den, 1.0), 0.0)
        for r in range(GROUP):
            ocmp_ref[0, g, r:r + 1, :] = jnp.sum(p[:, r:r + 1] * cv, axis=0, keepdims=True)
        ocmp_ref[0, g, GROUP:, :] = jnp.zeros((SUBLANES - GROUP, HEAD_DIM), f32)
        imp = jnp.sum(_dot_mask(overlap, p), axis=1, keepdims=True)
        imp_all = jnp.where(lane == g, imp, imp_all)
    imp_ref[0] = imp_all


def _cmp_sample(q, ak, av, wk, wv, *, n_batch, nrows, nblk, pos, name):
    q_spec = pl.BlockSpec((1, N_KV_HEADS, SUBLANES, HEAD_DIM), lambda b: (b, 0, 0, 0))
    a_spec = pl.BlockSpec((nrows * SUBLANES, HEAD_DIM), lambda b: (b, 0))
    w_spec = pl.BlockSpec((HEAD_DIM, HEAD_DIM), lambda b: (0, 0))
    return pl.pallas_call(
        functools.partial(_cmp_sample_kernel, nrows=nrows, nblk=nblk, pos=pos),
        grid=(n_batch,),
        in_specs=[q_spec, a_spec, a_spec, w_spec, w_spec],
        out_specs=[q_spec, pl.BlockSpec((1, nblk, SUBLANES), lambda b: (b, 0, 0))],
        out_shape=[jax.ShapeDtypeStruct((n_batch, N_KV_HEADS, SUBLANES, HEAD_DIM), f32),
                   jax.ShapeDtypeStruct((n_batch, nblk, SUBLANES), f32)],
        compiler_params=_params("parallel"),
        name=name,
    )(q, ak, av, wk, wv)


def _rank_sample_kernel(imp_ref, idx_ref, score_ref, *, nblk, qb):
    blk = lax.broadcasted_iota(jnp.int32, (nblk, 1), 0)
    causal = blk <= qb
    forced = (blk == 0) | (blk == qb) | (blk == qb - 1)
    score = jnp.where(causal, imp_ref[...] + FORCE_BONUS * forced.astype(f32), -jnp.inf)
    rank = _stable_rank(score, score_ref, blk, nblk)
    for slot in range(N_SEL):
        hit = causal & (rank == float(slot))
        idx_ref[slot:slot + 1, :] = jnp.sum(jnp.where(hit, blk, 0), axis=0, keepdims=True)
        idx_ref[N_SEL + slot:N_SEL + slot + 1, :] = jnp.max(hit.astype(jnp.int32), axis=0, keepdims=True)


def _rank_sample(imp_t, *, nblk, qb, name):
    lanes = imp_t.shape[1]
    return pl.pallas_call(
        functools.partial(_rank_sample_kernel, nblk=nblk, qb=qb),
        grid=(1,),
        in_specs=[pl.BlockSpec((nblk, lanes), lambda i: (0, 0))],
        out_specs=pl.BlockSpec((2 * N_SEL, lanes), lambda i: (0, 0)),
        out_shape=jax.ShapeDtypeStruct((2 * N_SEL, lanes), jnp.int32),
        scratch_shapes=[pltpu.VMEM((nblk, lanes), f32)],
        compiler_params=_params("arbitrary"),
        name=name,
    )(imp_t)


def _softmax_pv(s, v, exp=jnp.exp):
    e = exp(s - jnp.max(s, axis=-1, keepdims=True))
    den = jnp.sum(e, axis=-1, keepdims=True)
    return _dot(e.astype(bf16), v) / den


LOG2_E = 1.4426950408889634


def _masked_attention(q, k, v, mask, tq):
    heads = q.shape[0] // tq
    keys = k.shape[0]
    s = _dot_nt(q, k)
    s = jnp.where(mask[None], s.reshape(heads, tq, keys), NEG_INF).reshape(heads * tq, keys)
    return _softmax_pv(s, v, exp=jnp.exp2)


HEADS_PER_CHAIN = 1


def _nsa_prompt_kernel(q_ref, c_ref, sa_ref, sb_ref, ks_ref, vs_ref, kw_ref, vw_ref, sel_ref, ocmp_ref,
                       gate_ref, o_ref, *, tq, seq, nblk, span, n_q):
    qi = pl.program_id(2)
    t0 = qi * tq
    qr = (_rope(q_ref[...], c_ref[...], sa_ref[...], sb_ref[...]) * (SCALE * LOG2_E)).astype(bf16)
    n_chain = GROUP // HEADS_PER_CHAIN
    chains = [jnp.concatenate([qr[:, (c * HEADS_PER_CHAIN + h) * HEAD_DIM:(c * HEADS_PER_CHAIN + h + 1) * HEAD_DIM]
                               for h in range(HEADS_PER_CHAIN)], axis=0) for c in range(n_chain)]
    tpos = t0 + lax.broadcasted_iota(jnp.int32, (tq, 1), 0)

    start = pl.multiple_of(jnp.maximum(t0 - WINDOW, 0), tq)
    wpos = start + lax.broadcasted_iota(jnp.int32, (1, span), 1)
    wmask = (wpos <= tpos) & (wpos > tpos - WINDOW)
    kw = kw_ref[pl.ds(start, span), :].astype(bf16)
    vw = vw_ref[pl.ds(start, span), :].astype(bf16)
    o_win = [_masked_attention(qc, kw, vw, wmask, tq) for qc in chains]

    gates = _sigmoid(gate_ref[...])
    sel = sel_ref[0, 0].astype(bf16)

    tiles_per_class = 2
    for cls in range(-(-n_q // tiles_per_class)):
        ext = min(seq, (cls + 1) * tiles_per_class * tq)

        @pl.when(qi // tiles_per_class == cls)
        def _(ext=ext):
            kpos = lax.broadcasted_iota(jnp.int32, (1, ext), 1)
            expand = (kpos // L_SEL == lax.broadcasted_iota(jnp.int32, (nblk, 1), 0)).astype(bf16)
            mask = (_dot(sel, expand) > 0.5) & (kpos <= tpos)
            k = ks_ref[0:ext, :].astype(bf16)
            v = vs_ref[0:ext, :].astype(bf16)
            for c, qc in enumerate(chains):
                o_slc = _masked_attention(qc, k, v, mask, tq)
                for h in range(HEADS_PER_CHAIN):
                    r = c * HEADS_PER_CHAIN + h
                    sl = slice(r * HEAD_DIM, (r + 1) * HEAD_DIM)
                    rows = slice(h * tq, (h + 1) * tq)
                    o = (gates[:, 3 * r:3 * r + 1] * ocmp_ref[:, sl] + gates[:, 3 * r + 1:3 * r + 2] * o_slc[rows]
                         + gates[:, 3 * r + 2:3 * r + 3] * o_win[c][rows])
                    o_ref[:, sl] = o.astype(o_ref.dtype)


def _nsa_prompt(z1, tables, ks, vs, kw, vw, sel, ocmp, gates, *, batch, seq, nblk, name):
    tq = _tile(seq, (256, 128))
    n_q = seq // tq
    span = min(WINDOW + tq, seq)
    row_blk = lambda w: pl.BlockSpec((tq, w), lambda b, g, i: (b * n_q + i, g))
    tab = pl.BlockSpec((tq, GROUP_W), lambda b, g, i: (i, 0))
    kv = pl.BlockSpec((seq, HEAD_DIM), lambda b, g, i: (b, g))
    return pl.pallas_call(
        functools.partial(_nsa_prompt_kernel, tq=tq, seq=seq, nblk=nblk, span=span, n_q=n_q),
        grid=(batch, N_KV_HEADS, n_q),
        in_specs=[row_blk(GROUP_W), tab, tab, tab, kv, kv, kv, kv,
                  pl.BlockSpec((1, 1, tq, nblk), lambda b, g, i: (b, g, i, 0)),
                  row_blk(GROUP_W),
                  pl.BlockSpec((None, tq, 3 * GROUP), lambda b, g, i: (g, b * n_q + i, 0))],
        out_specs=row_blk(GROUP_W),
        out_shape=jax.ShapeDtypeStruct((batch * seq, Q_W), bf16),
        compiler_params=_params("parallel", "parallel", "arbitrary"),
        name=name,
    )(z1, *tables, ks, vs, kw, vw, sel, ocmp, gates)


def _nsa_sample_kernel(idx_ref, pt_ref, q_ref, c_ref, sa_ref, sb_ref, ksn_ref, vsn_ref, kwin_ref, vwin_ref,
                       ocmp_ref, gate_ref, poolk_ref, poolv_ref, o_ref, kbuf, vbuf, sems, *,
                       n_pages, page, pos, window):
    b = pl.program_id(0)
    per_page = page // L_SEL
    past_blocks = n_pages * per_page
    n_keys = N_SEL * L_SEL

    def gather_copies(seq):
        half = seq % 2
        out = []
        for g in range(N_KV_HEADS):
            for slot in range(N_SEL):
                blk = idx_ref[(seq * N_KV_HEADS + g) * 2 * N_SEL + slot]
                jp = jnp.minimum(blk, past_blocks - 1)
                phys = pt_ref[seq * n_pages + jp // per_page]
                off = pl.multiple_of((jp % per_page) * L_SEL, L_SEL)
                dst = pl.ds(slot * L_SEL, L_SEL)
                out.append(pltpu.make_async_copy(poolk_ref.at[phys, pl.ds(off, L_SEL), g], kbuf.at[half, g, dst],
                                                 sems.at[0, half]))
                out.append(pltpu.make_async_copy(poolv_ref.at[phys, pl.ds(off, L_SEL), g], vbuf.at[half, g, dst],
                                                 sems.at[1, half]))
        return out

    @pl.when(b == 0)
    def _():
        for cp in gather_copies(b):
            cp.start()

    @pl.when(b + 1 < pl.num_programs(0))
    def _():
        for cp in gather_copies(b + 1):
            cp.start()

    for cp in gather_copies(b):
        cp.wait()
    half = b % 2

    lane = lax.broadcasted_iota(jnp.int32, (1, n_keys), 1)
    lane_slot = lane // L_SEL
    for g in range(N_KV_HEADS):
        qr = _rope(q_ref[0, g], c_ref[...], sa_ref[...], sb_ref[...]).astype(bf16)
        blk_vec = jnp.zeros((1, n_keys), jnp.int32)
        ok_vec = jnp.zeros((1, n_keys), jnp.int32)
        base = (b * N_KV_HEADS + g) * 2 * N_SEL
        for slot in range(N_SEL):
            blk_vec = jnp.where(lane_slot == slot, idx_ref[base + slot], blk_vec)
            ok_vec = jnp.where(lane_slot == slot, idx_ref[base + N_SEL + slot], ok_vec)
        in_past = blk_vec < past_blocks
        kpos = blk_vec * L_SEL + lane % L_SEL
        mask = (ok_vec > 0) & (kpos <= pos)
        k_new = ksn_ref[0, g:g + 1, :].astype(bf16).astype(f32)
        v_new = vsn_ref[0, g:g + 1, :].astype(bf16).astype(f32)
        s_new = jnp.sum(qr.astype(f32) * k_new, axis=-1, keepdims=True)
        s = jnp.where(in_past, _dot_nt(qr, kbuf[half, g].astype(bf16)), s_new) * SCALE
        s = jnp.where(mask, s, NEG_INF)
        e = jnp.exp(s - jnp.max(s, axis=-1, keepdims=True))
        den = jnp.sum(e, axis=-1, keepdims=True)
        e_new = jnp.sum(jnp.where(in_past, 0.0, e), axis=-1, keepdims=True)
        e_past = jnp.where(in_past, e, 0.0).astype(bf16)
        o_slc = (_dot(e_past, vbuf[half, g].astype(bf16)) + e_new.astype(bf16).astype(f32) * v_new) / den
        win_rows = pl.ds(g, window, stride=N_KV_HEADS)
        o_win = _softmax_pv(_dot_nt(qr, kwin_ref[win_rows, :].astype(bf16)) * SCALE,
                            vwin_ref[win_rows, :].astype(bf16))
        gt = _sigmoid(gate_ref[0, g])
        o_ref[0, g] = gt[:, 0:1] * ocmp_ref[0, g] + gt[:, 1:2] * o_slc + gt[:, 2:3] * o_win


def _nsa_sample(idx, page_table, q, tables, ks_new, vs_new, kwin, vwin, ocmp, gates, pool_k, pool_v, *,
                pos, name):
    n_batch, n_pages = page_table.shape
    page = pool_k.shape[1]
    window = kwin.shape[0] // (n_batch * N_KV_HEADS)
    tab = pl.BlockSpec((SUBLANES, HEAD_DIM), lambda b, *_: (0, 0))
    heads = pl.BlockSpec((1, N_KV_HEADS, SUBLANES, HEAD_DIM), lambda b, *_: (b, 0, 0, 0))
    new_row = pl.BlockSpec((1, N_KV_HEADS, HEAD_DIM), lambda b, *_: (b, 0, 0))
    win = pl.BlockSpec((window * N_KV_HEADS, HEAD_DIM), lambda b, *_: (b, 0))
    return pl.pallas_call(
        functools.partial(_nsa_sample_kernel, n_pages=n_pages, page=page, pos=pos, window=window),
        grid_spec=pltpu.PrefetchScalarGridSpec(
            num_scalar_prefetch=2,
            grid=(n_batch,),
            in_specs=[heads, tab, tab, tab, new_row, new_row, win, win, heads, heads,
                      pl.BlockSpec(memory_space=pl.ANY), pl.BlockSpec(memory_space=pl.ANY)],
            out_specs=heads,
            scratch_shapes=[pltpu.VMEM((2, N_KV_HEADS, N_SEL * L_SEL, HEAD_DIM), f32),
                            pltpu.VMEM((2, N_KV_HEADS, N_SEL * L_SEL, HEAD_DIM), f32),
                            pltpu.SemaphoreType.DMA((2, 2))],
        ),
        out_shape=jax.ShapeDtypeStruct((n_batch, N_KV_HEADS, SUBLANES, HEAD_DIM), f32),
        compiler_params=_params("arbitrary"),
        name=name,
    )(idx, page_table.reshape(-1), q, *tables, ks_new, vs_new, kwin, vwin, ocmp, gates, pool_k, pool_v)


def _window_update_kernel(kin_ref, vin_ref, knew_ref, vnew_ref, kout_ref, vout_ref):
    rows = kin_ref.shape[1]
    keep = rows - N_KV_HEADS
    for src, new, dst in ((kin_ref, knew_ref, kout_ref), (vin_ref, vnew_ref, vout_ref)):
        dst[0, 0:keep, :] = src[0, N_KV_HEADS:rows, :]
        dst[0, keep:rows, :] = new[0]


def _window_update(win_k, win_v, k_new, v_new, name):
    n_batch, rows, d = win_k.shape
    win = pl.BlockSpec((1, rows, d), lambda b: (b, 0, 0))
    new = pl.BlockSpec((1, N_KV_HEADS, d), lambda b: (b, 0, 0))
    shape = jax.ShapeDtypeStruct(win_k.shape, win_k.dtype)
    return pl.pallas_call(
        _window_update_kernel,
        grid=(n_batch,),
        in_specs=[win, win, new, new],
        out_specs=[win, win],
        out_shape=[shape, shape],
        compiler_params=_params("parallel"),
        name=name,
    )(win_k, win_v, k_new, v_new)


def _sgu_prompt_kernel(u_ref, v_ref, g_ref, w_ref, bt_ref, o_ref):
    v = v_ref[...].astype(f32)
    vn = (v * lax.rsqrt(jnp.mean(v * v, axis=-1, keepdims=True) + EPS) * g_ref[...]).astype(bf16)
    tril = (lax.broadcasted_iota(jnp.int32, (SG_CHUNK, SG_CHUNK), 0)
            >= lax.broadcasted_iota(jnp.int32, (SG_CHUNK, SG_CHUNK), 1))
    bt = bt_ref[...]
    group_dim = v.shape[-1] // SG_GROUPS
    for gi in range(SG_GROUPS):
        sl = slice(gi * group_dim, (gi + 1) * group_dim)
        w = jnp.where(tril, w_ref[gi], 0.0).astype(bf16)
        mixed = _dot(w, vn[:, sl]) + bt[:, gi:gi + 1]
        o_ref[:, sl] = (u_ref[:, sl].astype(f32) * mixed).astype(o_ref.dtype)


def _sgu_prompt(uv, norm_g, w_s, b_s, name):
    m, two_w = uv.shape
    width = two_w // 2
    return pl.pallas_call(
        _sgu_prompt_kernel,
        grid=(m // SG_CHUNK,),
        in_specs=[pl.BlockSpec((SG_CHUNK, width), lambda i: (i, 0)),
                  pl.BlockSpec((SG_CHUNK, width), lambda i: (i, 1)),
                  pl.BlockSpec((1, width), lambda i: (0, 0)),
                  pl.BlockSpec((SG_GROUPS, SG_CHUNK, SG_CHUNK), lambda i: (0, 0, 0)),
                  pl.BlockSpec((SG_CHUNK, SG_GROUPS), lambda i: (0, 0))],
        out_specs=pl.BlockSpec((SG_CHUNK, width), lambda i: (i, 0)),
        out_shape=jax.ShapeDtypeStruct((m, width), bf16),
        compiler_params=_params("parallel"),
        name=name,
    )(uv, uv, norm_g.reshape(1, width), w_s, b_s.T)


def _sgu_sample_kernel(u_ref, v_ref, g_ref, w0_ref, b0_ref, o_ref, vn_ref):
    v = v_ref[...]
    vn = v * lax.rsqrt(jnp.mean(v * v, axis=-1, keepdims=True) + EPS) * g_ref[...]
    vn_ref[...] = vn
    o_ref[...] = (u_ref[...] * (vn * w0_ref[...] + b0_ref[...])).astype(o_ref.dtype)


def _sgu_sample(uv, norm_g, w_s, b_s, name):
    m, two_w = uv.shape
    width = two_w // 2
    group_dim = width // SG_GROUPS
    w0 = jnp.repeat(w_s[:, 0, 0], group_dim).reshape(1, width)
    b0 = jnp.repeat(b_s[:, 0], group_dim).reshape(1, width)
    row = pl.BlockSpec((1, width), lambda i: (0, 0))
    return pl.pallas_call(
        _sgu_sample_kernel,
        grid=(1,),
        in_specs=[pl.BlockSpec((m, width), lambda i: (0, 0)), pl.BlockSpec((m, width), lambda i: (0, 1)),
                  row, row, row],
        out_specs=[pl.BlockSpec((m, width), lambda i: (0, 0))] * 2,
        out_shape=[jax.ShapeDtypeStruct((m, width), bf16), jax.ShapeDtypeStruct((m, width), f32)],
        compiler_params=_params("arbitrary"),
        name=name,
    )(uv, uv, norm_g.reshape(1, width), w0, b0)


CONV_HALO = SUBLANES


def _ffn_up_kernel(xp_ref, xs_ref, wa_ref, wb_ref, cwa_ref, cwb_ref, cba_ref, cbb_ref,
                   h0a_ref, h0b_ref, h1a_ref, h1b_ref,
                   act_ref, ta_ref, tb_ref, acts_ref, upa_ref, upb_ref,
                   wabf_ref, wbbf_ref, bufa, bufb, *, tm, tiles_per_seq, n_m):
    mi = pl.program_id(1)

    @pl.when(mi == 0)
    def _():
        _cast_weight_tile(wa_ref, wabf_ref)
        _cast_weight_tile(wb_ref, wbbf_ref)

    @pl.when(mi % tiles_per_seq == 0)
    def _():
        bufa[...] = jnp.zeros_like(bufa)
        bufb[...] = jnp.zeros_like(bufb)

    x = xp_ref[...]
    halves = []
    for wbf_ref, cw_ref, cb_ref, halo_ref, tail_ref in ((wabf_ref, cwa_ref, cba_ref, bufa, ta_ref),
                                                        (wbbf_ref, cwb_ref, cbb_ref, bufb, tb_ref)):
        up = _dot(x, wbf_ref[...])
        w0, w1, w2, cb = cw_ref[0:1, :], cw_ref[1:2, :], cw_ref[2:3, :], cb_ref[...]
        t = up * w1 + pltpu.roll(up * w0, 1, 0)
        conv = cb + up * w2 + pltpu.roll(t, 1, 0)
        ext = jnp.concatenate([halo_ref[...], up[0:CONV_HALO, :]], axis=0)
        lo = CONV_HALO
        head = (cb + ext[lo:2 * lo, :] * w2 + ext[lo - 1:2 * lo - 1, :] * w1 + ext[lo - 2:2 * lo - 2, :] * w0)
        halves.append(jnp.concatenate([head, conv[CONV_HALO:, :]], axis=0))
        tail_ref[0] = up[tm - (CONV_W - 1):, :]
        halo_ref[...] = up[tm - CONV_HALO:, :]
    act_ref[...] = (_silu(halves[0]) * halves[1]).astype(act_ref.dtype)

    @pl.when(mi == n_m - 1)
    def _():
        xs = xs_ref[...]
        halves = []
        for wbf_ref, cw_ref, cb_ref, h0_ref, h1_ref, up_ref in (
                (wabf_ref, cwa_ref, cba_ref, h0a_ref, h1a_ref, upa_ref),
                (wbbf_ref, cwb_ref, cbb_ref, h0b_ref, h1b_ref, upb_ref)):
            up = _dot(xs, wbf_ref[...])
            up_ref[...] = up
            conv = cb_ref[...] + h0_ref[...] * cw_ref[0:1, :]
            conv = conv + h1_ref[...] * cw_ref[1:2, :]
            halves.append(conv + up * cw_ref[2:3, :])
        acts_ref[...] = (_silu(halves[0]) * halves[1]).astype(acts_ref.dtype)


def _ffn_up(hn_p, hn_s, w_up, conv_w, conv_b, hist0, hist1, batch, seq, name):
    mp, d = hn_p.shape
    ms = hn_s.shape[0]
    d_ff = w_up.shape[1] // 2
    tm = _tile(seq, (1024, 512, 256, 128))
    tn = _tile(d_ff, (256, 128))
    nb = d_ff // tn
    n_m = mp // tm
    tiles_per_seq = seq // tm
    col = lambda shape, off: pl.BlockSpec(shape, lambda j, i: (0, j + off))
    tail = pl.BlockSpec((1, CONV_W - 1, tn), lambda j, i: (i // tiles_per_seq, 0, j))
    tail_shape = jax.ShapeDtypeStruct((batch, CONV_W - 1, d_ff), f32)
    up_shape = jax.ShapeDtypeStruct((ms, d_ff), f32)
    buf = pltpu.VMEM((CONV_HALO, tn), f32)
    return pl.pallas_call(
        functools.partial(_ffn_up_kernel, tm=tm, tiles_per_seq=tiles_per_seq, n_m=n_m),
        grid=(nb, n_m),
        in_specs=[pl.BlockSpec((tm, d), lambda j, i: (i, 0)), pl.BlockSpec((ms, d), lambda j, i: (0, 0)),
                  col((d, tn), 0), col((d, tn), nb),
                  col((CONV_W, tn), 0), col((CONV_W, tn), nb), col((1, tn), 0), col((1, tn), nb),
                  col((ms, tn), 0), col((ms, tn), nb), col((ms, tn), 0), col((ms, tn), nb)],
        out_specs=[pl.BlockSpec((tm, tn), lambda j, i: (i, j)), tail, tail,
                   col((ms, tn), 0), col((ms, tn), 0), col((ms, tn), 0)],
        out_shape=[jax.ShapeDtypeStruct((mp, d_ff), bf16), tail_shape, tail_shape,
                   jax.ShapeDtypeStruct((ms, d_ff), bf16), up_shape, up_shape],
        scratch_shapes=[pltpu.VMEM((d, tn), bf16), pltpu.VMEM((d, tn), bf16), buf, buf],
        compiler_params=_params("parallel", "arbitrary"),
        name=name,
    )(hn_p, hn_s, w_up, w_up, conv_w, conv_w, conv_b, conv_b, hist0, hist0, hist1, hist1)


def kernel(x_prompt, x_sample, cache_k_cmp, cache_v_cmp, cache_k_slc, cache_v_slc, cache_k_win, cache_v_win,
           state_ffn_conv, page_table, norm1_g, w_in, cmp_pool_k, cmp_bias_k, cmp_w_k, cmp_pool_v, cmp_bias_v,
           cmp_w_v, sg_norm_g, sg_w, sg_b, w_proj_a, w_proj_b, w_out, norm2_g, w_up, conv_w, conv_b, w_down,
           norm_f_g):
    batch, seq, d_model = x_prompt.shape
    n_dec, dec_seq, _ = x_sample.shape
    depth = w_in.shape[0]
    page = cache_k_cmp.shape[2]
    past = page_table.shape[1] * page
    assert dec_seq == 1 and cache_k_win.shape[2] == WINDOW and past >= WINDOW
    assert seq % SG_CHUNK == 0 and seq % L_SEL == 0 and past % L_SEL == 0
    nblk_p = max(seq // L_SEL, N_SEL)
    nblk_s = -(-max(-(-(past + 1) // L_SEL), N_SEL) // SUBLANES) * SUBLANES

    tables_p = _rope_tables(jnp.arange(seq), GROUP)
    tables_s_kv = _rope_tables(jnp.full((n_dec,), past), GROUP)
    tables_s_q = _rope_tables(jnp.full((SUBLANES,), past), 1)

    hp = x_prompt.reshape(batch * seq, d_model)
    hs = x_sample.reshape(n_dec, d_model)
    new_p = [[] for _ in range(7)]
    new_s = [[] for _ in range(8)]
    for l in range(depth):
        pwk, bk, pwv, bv = cmp_pool_k[l], cmp_bias_k[l], cmp_pool_v[l], cmp_bias_v[l]
        cw = conv_w[l]
        cb = conv_b[l].reshape(1, -1)

        xn_p = _rmsnorm(hp, norm1_g[l], bf16, "norm1_p")
        xn_s = _rmsnorm(hs, norm1_g[l], bf16, "norm1_s")
        w_in_nk = jnp.swapaxes(w_in[l], 0, 1)
        z1_p, z1_s = _matmul_wres(xn_p, xn_s, w_in_nk, Z1_W, (f32, f32), "in_proj_qkv", w_is_nk=True)
        uv_p, uv_s = _matmul_wres(xn_p, xn_s, w_in_nk, d_model, (bf16, f32), "in_proj_uv", w_is_nk=True,
                                  col0=QKV_W + NSA_GATE_W, epilogue=_gelu_tanh)
        gm_p, gm_s = _matmul_wres(xn_p, xn_s, w_in_nk, 2 * d_model, (bf16, bf16), "in_proj_gate", w_is_nk=True,
                                  col0=QKV_W + NSA_GATE_W + d_model, epilogue=_sigmoid)

        kc_p, vc_p, ks_p, vs_p, kw_p, vw_p, ksb, vsb, kwb, vwb = _prep_kv(z1_p, tables_p, seq, "prep_p")
        ak, av = _compress_prompt(kc_p, vc_p, pwk, bk, pwv, bv, batch, seq, "compress_p")
        ocmp, sel = _cmp_select_prompt(z1_p, ak, av, cmp_w_k[l], cmp_w_v[l], batch=batch, seq=seq, nblk=nblk_p,
                                       name="cmp_select_p")
        gates = z1_p[:, QKV_W:QKV_W + NSA_GATE_W].reshape(batch * seq, N_KV_HEADS, 3 * GROUP).transpose(1, 0, 2)
        nsa_p = _nsa_prompt(z1_p, tables_p, ksb, vsb, kwb, vwb, sel, ocmp, gates, batch=batch, seq=seq,
                            nblk=nblk_p, name="nsa_p")
        sg_p = _sgu_prompt(uv_p, sg_norm_g[l], sg_w[l], sg_b[l], "sgu_p")

        kc_s, vc_s, ks_s, vs_s, kw_s, vw_s = _prep_kv(z1_s, tables_s_kv, n_dec, "prep_s")[:6]
        ak, av = _compress_paged(cache_k_cmp[l], cache_v_cmp[l], page_table, pwk, bk, pwv, bv, "compress_s")
        heads_as_rows = lambda t, w: jnp.pad(t.reshape(n_dec, N_KV_HEADS, GROUP, w),
                                             ((0, 0), (0, 0), (0, SUBLANES - GROUP), (0, HEAD_DIM - w)))
        q_s = heads_as_rows(z1_s[:, :Q_W], HEAD_DIM)
        ocmp, imp = _cmp_sample(q_s, ak, av, cmp_w_k[l], cmp_w_v[l], n_batch=n_dec, nrows=past // STRIDE,
                                nblk=nblk_s, pos=past, name="cmp_s")
        imp_t = imp.transpose(1, 0, 2).reshape(nblk_s, n_dec * SUBLANES)
        idx = _rank_sample(imp_t, nblk=nblk_s, qb=past // L_SEL, name="rank_s")
        idx = idx.T.reshape(n_dec, SUBLANES, 2 * N_SEL)[:, :N_KV_HEADS].reshape(-1)
        new_rows = lambda t: t.reshape(n_dec, N_KV_HEADS, HEAD_DIM)
        cache_rows = lambda t: t.reshape(n_dec, WINDOW * N_KV_HEADS, HEAD_DIM)
        win_k, win_v = _window_update(cache_rows(cache_k_win[l]), cache_rows(cache_v_win[l]), new_rows(kw_s),
                                      new_rows(vw_s), "window_s")
        gates = heads_as_rows(z1_s[:, QKV_W:QKV_W + NSA_GATE_W], 3)
        nsa_s = _nsa_sample(idx, page_table, q_s, tables_s_q, new_rows(ks_s), new_rows(vs_s),
                            win_k.reshape(-1, HEAD_DIM), win_v.reshape(-1, HEAD_DIM), ocmp, gates,
                            cache_k_slc[l], cache_v_slc[l], pos=past, name="nsa_s")
        nsa_s = nsa_s[:, :, :GROUP].reshape(n_dec, Q_W).astype(bf16)
        sg_s, sg_v = _sgu_sample(uv_s, sg_norm_g[l], sg_w[l], sg_b[l], "sgu_s")

        m_p, m_s = _merge(nsa_p, nsa_s, sg_p, sg_s, w_proj_a[l], w_proj_b[l], gm_p, gm_s, "merge")
        h_p, h_s = _matmul_wres(m_p, m_s, w_out[l], d_model, (f32, f32), "out_proj",
                                epilogue=lambda acc, x: x + acc, extras_p=(hp,), extras_s=(hs,))
        hn_p = _rmsnorm(h_p, norm2_g[l], bf16, "norm2_p")
        hn_s = _rmsnorm(h_s, norm2_g[l], bf16, "norm2_s")
        hist = state_ffn_conv[l]
        act_p, tail_a, tail_b, act_s, up_a, up_b = _ffn_up(hn_p, hn_s, w_up[l], cw, cb, hist[:, 0], hist[:, 1],
                                                           batch, seq, "ffn_up")
        wdown = w_down[l].astype(bf16)
        hp = _matmul_residual(act_p, wdown, h_p, "ffn_down_p")
        hs = _matmul_residual(act_s, wdown, h_s, "ffn_down_s")

        kv5 = lambda t: t.reshape(batch, seq, N_KV_HEADS, HEAD_DIM)
        keep_p = min(WINDOW, seq)
        for lst, val in zip(new_p, (kv5(kc_p), kv5(vc_p), kv5(ks_p), kv5(vs_p), kv5(kw_p)[:, seq - keep_p:],
                                    kv5(vw_p)[:, seq - keep_p:], jnp.concatenate([tail_a, tail_b], axis=-1))):
            lst.append(val)
        kv5 = lambda t: t.reshape(n_dec, 1, N_KV_HEADS, HEAD_DIM)
        conv_s = jnp.stack([hist[:, 1], jnp.concatenate([up_a, up_b], axis=-1)], axis=1)
        win5 = lambda t: t.reshape(n_dec, WINDOW, N_KV_HEADS, HEAD_DIM)
        for lst, val in zip(new_s, (kv5(kc_s), kv5(vc_s), kv5(ks_s), kv5(vs_s), win5(win_k), win5(win_v),
                                    sg_v.reshape(n_dec, 1, -1), conv_s)):
            lst.append(val)

    y_prompt = _rmsnorm(hp, norm_f_g, f32, "norm_f_p").reshape(batch, seq, d_model)
    y_sample = _rmsnorm(hs, norm_f_g, f32, "norm_f_s").reshape(n_dec, 1, d_model)
    return (y_prompt, y_sample, *(jnp.stack(v) for v in new_p), *(jnp.stack(v) for v in new_s))
```

```python
import functools

import jax
import jax.numpy as jnp
from jax import lax
from jax.experimental import pallas as pl
from jax.experimental.pallas import tpu as pltpu

f32 = jnp.float32
bf16 = jnp.bfloat16

N_HEADS = 16
N_KV_HEADS = 4
GROUP = N_HEADS // N_KV_HEADS
HEAD_DIM = 128
ROT_DIM = HEAD_DIM // 4
ROT_HALF = ROT_DIM // 2
ROPE_THETA = 500000.0
L_CMP = 32
STRIDE = 16
L_SEL = 64
N_SEL = 16
WINDOW = 512
FORCE_BONUS = 1000.0
SCALE = HEAD_DIM ** -0.5
NEG_INF = -1e30
SG_GROUPS = 16
SG_CHUNK = 128
CONV_W = 3
EPS = 1e-6

SUBLANES = 8
LANES = 128
GROUP_W = GROUP * HEAD_DIM
Q_W = N_HEADS * HEAD_DIM
KV_W = N_KV_HEADS * HEAD_DIM
NSA_GATE_W = 3 * N_HEADS
QKV_W = Q_W + 6 * KV_W
Z1_W = QKV_W + GROUP_W
COL_KC, COL_VC, COL_KS, COL_VS, COL_KW, COL_VW, COL_GATE = (Q_W // GROUP_W + i for i in range(7))

VMEM_LIMIT_BYTES = 56 * 1024 * 1024


def _params(*semantics):
    return pltpu.CompilerParams(dimension_semantics=semantics, vmem_limit_bytes=VMEM_LIMIT_BYTES)


def _tile(n, candidates):
    for c in candidates:
        if n % c == 0:
            return c
    return n


def _sigmoid(x):
    return 0.5 * jnp.tanh(0.5 * x) + 0.5


def _silu(x):
    return x * _sigmoid(x)


def _gelu_tanh(x):
    return x * (0.5 * (1.0 + jnp.tanh(0.7978845608028654 * (x + 0.044715 * (x * x * x)))))


def _dot(a, b, precision=None):
    return jnp.dot(a, b, preferred_element_type=f32, precision=precision)


def _dot_nt(a, b, precision=None):
    return lax.dot_general(a, b, (((1,), (1,)), ((), ())), precision=precision, preferred_element_type=f32)


def _split2(x):
    hi = x.astype(bf16)
    return hi, (x - hi.astype(f32)).astype(bf16)


def _dot_3pass(a, b, mm=_dot):
    a_hi, a_lo = a if isinstance(a, tuple) else _split2(a)
    b_hi, b_lo = b if isinstance(b, tuple) else _split2(b)
    return mm(a_hi, b_hi) + (mm(a_hi, b_lo) + mm(a_lo, b_hi))


def _dot_mask(m, x, mm=_dot):
    hi = x.astype(bf16)
    mid, lo = _split2(x - hi.astype(f32))
    return mm(m, hi) + (mm(m, mid) + mm(m, lo))


def _rope(x, c, sa, sb):
    lanes = x.shape[-1]
    return x * c + pltpu.roll(x, lanes - ROT_HALF, 1) * sa + pltpu.roll(x, ROT_HALF, 1) * sb


def _rope_tables(pos, reps):
    n = pos.shape[0]
    inv_freq = jnp.power(jnp.float32(ROPE_THETA), -jnp.arange(ROT_HALF, dtype=f32) / ROT_HALF)
    ang = pos.astype(f32)[:, None] * inv_freq[None, :]
    cos, sin = jnp.cos(ang), jnp.sin(ang)
    zeros = lambda w: jnp.zeros((n, w), f32)
    c = jnp.concatenate([cos, cos, jnp.ones((n, HEAD_DIM - ROT_DIM), f32)], axis=1)
    sa = jnp.concatenate([-sin, zeros(HEAD_DIM - ROT_HALF)], axis=1)
    sb = jnp.concatenate([zeros(ROT_HALF), sin, zeros(HEAD_DIM - ROT_DIM)], axis=1)
    return tuple(jnp.tile(t, (1, reps)) for t in (c, sa, sb))


def _rmsnorm_kernel(x_ref, g_ref, o_ref):
    x = x_ref[...].astype(f32)
    y = x * lax.rsqrt(jnp.mean(x * x, axis=-1, keepdims=True) + EPS)
    o_ref[...] = (y * g_ref[...]).astype(o_ref.dtype)


def _rmsnorm(x, g, out_dtype, name):
    m, d = x.shape
    tm = _tile(m, (256, 128, 64, 32, 16, 8))
    return pl.pallas_call(
        _rmsnorm_kernel,
        grid=(m // tm,),
        in_specs=[pl.BlockSpec((tm, d), lambda i: (i, 0)), pl.BlockSpec((1, d), lambda i: (0, 0))],
        out_specs=pl.BlockSpec((tm, d), lambda i: (i, 0)),
        out_shape=jax.ShapeDtypeStruct((m, d), out_dtype),
        compiler_params=_params("parallel"),
        name=name,
    )(x, g.reshape(1, d).astype(f32))


CAST_ROWS = 128


def _cast_weight_tile(w_ref, wbf_ref):
    rows_total = w_ref.shape[0]
    step = _tile(rows_total, (CAST_ROWS,))
    for r0 in range(0, rows_total, step):
        wbf_ref[r0:r0 + step, :] = w_ref[r0:r0 + step, :].astype(bf16)


def _mm_wres_kernel(ap_ref, as_ref, w_ref, *rest, epilogue, n_extra, n_m, w_is_nk):
    extras_p, extras_s = rest[:n_extra], rest[n_extra:2 * n_extra]
    op_ref, os_ref, wbf_ref = rest[2 * n_extra:]
    mi = pl.program_id(1)
    mm = _dot_nt if w_is_nk else _dot

    @pl.when(mi == 0)
    def _():
        _cast_weight_tile(w_ref, wbf_ref)

    w = wbf_ref[...]
    op_ref[...] = epilogue(mm(ap_ref[...], w), *[e[...] for e in extras_p]).astype(op_ref.dtype)

    @pl.when(mi == n_m - 1)
    def _():
        os_ref[...] = epilogue(mm(as_ref[...], w), *[e[...] for e in extras_s]).astype(os_ref.dtype)


def _matmul_wres(a_p, a_s, w, n_cols, out_dtypes, name, *, w_is_nk=False, col0=0, epilogue=lambda acc: acc,
                 extras_p=(), extras_s=()):
    mp, k = a_p.shape
    ms = a_s.shape[0]
    tm = _tile(mp, (1024, 512, 256, 128))
    tn = _tile(n_cols, (512, 256, 128))
    n_m, n_n = mp // tm, n_cols // tn
    if w_is_nk:
        assert col0 % SUBLANES == 0
        w_spec = pl.BlockSpec((pl.Element(tn), pl.Element(k)),
                              lambda j, i: (pl.multiple_of(col0 + j * tn, SUBLANES), 0))
        wbf_shape = (tn, k)
    else:
        assert col0 % tn == 0
        w_spec = pl.BlockSpec((k, tn), lambda j, i: (0, col0 // tn + j))
        wbf_shape = (k, tn)
    in_specs = [pl.BlockSpec((tm, k), lambda j, i: (i, 0)), pl.BlockSpec((ms, k), lambda j, i: (0, 0)), w_spec]
    in_specs += [pl.BlockSpec((tm, tn), lambda j, i: (i, j)) for _ in extras_p]
    in_specs += [pl.BlockSpec((ms, tn), lambda j, i: (0, j)) for _ in extras_s]
    return pl.pallas_call(
        functools.partial(_mm_wres_kernel, epilogue=epilogue, n_extra=len(extras_p), n_m=n_m, w_is_nk=w_is_nk),
        grid=(n_n, n_m),
        in_specs=in_specs,
        out_specs=[pl.BlockSpec((tm, tn), lambda j, i: (i, j)), pl.BlockSpec((ms, tn), lambda j, i: (0, j))],
        out_shape=[jax.ShapeDtypeStruct((mp, n_cols), out_dtypes[0]),
                   jax.ShapeDtypeStruct((ms, n_cols), out_dtypes[1])],
        scratch_shapes=[pltpu.VMEM(wbf_shape, bf16)],
        compiler_params=_params("parallel", "arbitrary"),
        name=name,
    )(a_p, a_s, w, *extras_p, *extras_s)


def _merge_kernel(ap_ref, as_ref, bp_ref, bs_ref, wa_ref, wb_ref, gap_ref, gbp_ref, gas_ref, gbs_ref,
                  op_ref, os_ref, wabf_ref, wbbf_ref, *, n_m):
    mi = pl.program_id(1)

    @pl.when(mi == 0)
    def _():
        _cast_weight_tile(wa_ref, wabf_ref)
        _cast_weight_tile(wb_ref, wbbf_ref)

    def mix(a_ref, b_ref, ga_ref, gb_ref, o_ref):
        pa = _dot(a_ref[...], wabf_ref[...])
        pb = _dot(b_ref[...], wbbf_ref[...])
        o_ref[...] = (ga_ref[...].astype(f32) * pa + gb_ref[...].astype(f32) * pb).astype(o_ref.dtype)

    mix(ap_ref, bp_ref, gap_ref, gbp_ref, op_ref)

    @pl.when(mi == n_m - 1)
    def _():
        mix(as_ref, bs_ref, gas_ref, gbs_ref, os_ref)


def _merge(nsa_p, nsa_s, sg_p, sg_s, w_pa, w_pb, gates_p, gates_s, name):
    mp, ka = nsa_p.shape
    ms = nsa_s.shape[0]
    kb = sg_p.shape[1]
    d = w_pa.shape[1]
    tm = _tile(mp, (1024, 512, 256, 128))
    tn = _tile(d, (512, 256, 128))
    n_m, n_n = mp // tm, d // tn
    row_p = lambda w: pl.BlockSpec((tm, w), lambda j, i: (i, 0))
    row_s = lambda w: pl.BlockSpec((ms, w), lambda j, i: (0, 0))
    return pl.pallas_call(
        functools.partial(_merge_kernel, n_m=n_m),
        grid=(n_n, n_m),
        in_specs=[row_p(ka), row_s(ka), row_p(kb), row_s(kb),
                  pl.BlockSpec((ka, tn), lambda j, i: (0, j)), pl.BlockSpec((kb, tn), lambda j, i: (0, j)),
                  pl.BlockSpec((tm, tn), lambda j, i: (i, j)), pl.BlockSpec((tm, tn), lambda j, i: (i, j + n_n)),
                  pl.BlockSpec((ms, tn), lambda j, i: (0, j)), pl.BlockSpec((ms, tn), lambda j, i: (0, j + n_n))],
        out_specs=[pl.BlockSpec((tm, tn), lambda j, i: (i, j)), pl.BlockSpec((ms, tn), lambda j, i: (0, j))],
        out_shape=[jax.ShapeDtypeStruct((mp, d), bf16), jax.ShapeDtypeStruct((ms, d), bf16)],
        scratch_shapes=[pltpu.VMEM((ka, tn), bf16), pltpu.VMEM((kb, tn), bf16)],
        compiler_params=_params("parallel", "arbitrary"),
        name=name,
    )(nsa_p, nsa_s, sg_p, sg_s, w_pa, w_pb, gates_p, gates_p, gates_s, gates_s)


def _mm_resid_kernel(a_ref, w_ref, r_ref, o_ref):
    o_ref[...] = r_ref[...] + _dot(a_ref[...], w_ref[...])


def _matmul_residual(a, w, resid, name):
    m, k = a.shape
    n = w.shape[1]
    tm = _tile(m, (512, 256, 128, 64, 32, 16))
    tn = _tile(n, (256, 128))
    return pl.pallas_call(
        _mm_resid_kernel,
        grid=(m // tm, n // tn),
        in_specs=[pl.BlockSpec((tm, k), lambda i, j: (i, 0)), pl.BlockSpec((k, tn), lambda i, j: (0, j)),
                  pl.BlockSpec((tm, tn), lambda i, j: (i, j))],
        out_specs=pl.BlockSpec((tm, tn), lambda i, j: (i, j)),
        out_shape=jax.ShapeDtypeStruct((m, n), f32),
        compiler_params=_params("parallel", "arbitrary"),
        name=name,
    )(a, w, resid)


def _prep_kernel(kc, vc, ks, vs, kw, vw, c_ref, sa_ref, sb_ref, okc, ovc, oks, ovs, okw, ovw,
                 bks, bvs, bkw, bvw, *, tr):
    c, sa, sb = c_ref[...], sa_ref[...], sb_ref[...]
    values = (kc[...], vc[...], _rope(ks[...], c, sa, sb), vs[...], _rope(kw[...], c, sa, sb), vw[...])
    for val, o_ref in zip(values, (okc, ovc, oks, ovs, okw, ovw)):
        for g in range(N_KV_HEADS):
            o_ref[pl.ds(g, tr, stride=N_KV_HEADS), :] = val[:, g * HEAD_DIM:(g + 1) * HEAD_DIM]
    for val, o_ref in zip(values[2:], (bks, bvs, bkw, bvw)):
        o_ref[...] = val.astype(bf16)


def _prep_kv(z1, tables, rows_per_seq, name):
    m = z1.shape[0]
    tr = _tile(rows_per_seq, (256, 128, 64, 32, 16, 8))
    n_tab = tables[0].shape[0] // tr
    col = lambda cb: pl.BlockSpec((tr, KV_W), lambda i, cb=cb: (i, cb))
    tab = pl.BlockSpec((tr, KV_W), lambda i: (i % n_tab, 0))
    out = pl.BlockSpec((tr * N_KV_HEADS, HEAD_DIM), lambda i: (i, 0))
    out_b = pl.BlockSpec((tr, KV_W), lambda i: (i, 0))
    return pl.pallas_call(
        functools.partial(_prep_kernel, tr=tr),
        grid=(m // tr,),
        in_specs=[col(COL_KC), col(COL_VC), col(COL_KS), col(COL_VS), col(COL_KW), col(COL_VW), tab, tab, tab],
        out_specs=[out] * 6 + [out_b] * 4,
        out_shape=[jax.ShapeDtypeStruct((m * N_KV_HEADS, HEAD_DIM), f32)] * 6
        + [jax.ShapeDtypeStruct((m, KV_W), bf16)] * 4,
        compiler_params=_params("parallel"),
        name=name,
    )(z1, z1, z1, z1, z1, z1, *tables)


def _compress_prompt_kernel(k_ref, v_ref, pwk_ref, bk_ref, pwv_ref, bv_ref, ak_ref, av_ref, *, nsub):
    for x_ref, pw_ref, b_ref, o_ref in ((k_ref, pwk_ref, bk_ref, ak_ref), (v_ref, pwv_ref, bv_ref, av_ref)):
        for g in range(N_KV_HEADS):
            first = None
            second = None
            for j in range(STRIDE):
                rows = x_ref[pl.ds(j * N_KV_HEADS + g, nsub, stride=STRIDE * N_KV_HEADS), :]
                fa = rows * pw_ref[j:j + 1, :]
                sa = rows * pw_ref[STRIDE + j:STRIDE + j + 1, :]
                first = fa if first is None else first + fa
                second = sa if second is None else second + sa
            o_ref[0, :, g * HEAD_DIM:(g + 1) * HEAD_DIM] = _silu(pltpu.roll(first, 1, 0) + second + b_ref[...])


def _compress_prompt(kc, vc, pwk, bk, pwv, bv, batch, seq, name):
    nsub = seq // STRIDE
    xspec = pl.BlockSpec((seq * N_KV_HEADS, HEAD_DIM), lambda b: (b, 0))
    pwspec = pl.BlockSpec((L_CMP, HEAD_DIM), lambda b: (0, 0))
    bspec = pl.BlockSpec((1, HEAD_DIM), lambda b: (0, 0))
    ospec = pl.BlockSpec((1, nsub, KV_W), lambda b: (b, 0, 0))
    return pl.pallas_call(
        functools.partial(_compress_prompt_kernel, nsub=nsub),
        grid=(batch,),
        in_specs=[xspec, xspec, pwspec, bspec, pwspec, bspec],
        out_specs=[ospec, ospec],
        out_shape=[jax.ShapeDtypeStruct((batch, nsub, KV_W), f32)] * 2,
        compiler_params=_params("parallel"),
        name=name,
    )(kc, vc, pwk, bk.reshape(1, HEAD_DIM), pwv, bv.reshape(1, HEAD_DIM))


PAGES_PER_CHUNK = 16


def _compress_paged_kernel(pt_ref, wfk_ref, wsk_ref, bk_ref, wfv_ref, wsv_ref, bv_ref, poolk_ref, poolv_ref,
                           ak_ref, av_ref, bufk, bufv, sems, *, n_pages, page_rows, pps):
    b = pl.program_id(0)
    n_chunks = n_pages // pps
    tiles_per_sub = STRIDE * N_KV_HEADS // SUBLANES
    nsub = page_rows // (STRIDE * N_KV_HEADS)

    def copies(seq, chunk, slot):
        out = []
        for p in range(pps):
            phys = pt_ref[seq * n_pages + chunk * pps + p]
            out.append(pltpu.make_async_copy(poolk_ref.at[phys], bufk.at[slot, p], sems.at[0, slot]))
            out.append(pltpu.make_async_copy(poolv_ref.at[phys], bufv.at[slot, p], sems.at[1, slot]))
        return out

    @pl.when(b == 0)
    def _():
        for cp in copies(b, 0, 0):
            cp.start()

    carry = (jnp.zeros((SUBLANES, HEAD_DIM), f32), jnp.zeros((SUBLANES, HEAD_DIM), f32))
    for chunk in range(n_chunks):
        slot = chunk % 2
        if chunk + 1 < n_chunks:
            for cp in copies(b, chunk + 1, 1 - slot):
                cp.start()
        else:
            @pl.when(b + 1 < pl.num_programs(0))
            def _():
                for cp in copies(b + 1, 0, 0):
                    cp.start()
        for cp in copies(b, chunk, slot):
            cp.wait()

        def page_body(p, prev, chunk=chunk, slot=slot):
            out_row0 = pl.multiple_of((chunk * pps + p) * nsub * SUBLANES, nsub * SUBLANES)
            new_prev = []
            for which, (buf, wf_ref, ws_ref, b_ref, o_ref) in enumerate(
                    ((bufk, wfk_ref, wsk_ref, bk_ref, ak_ref), (bufv, wfv_ref, wsv_ref, bv_ref, av_ref))):
                last = prev[which]
                for n in range(nsub):
                    first = None
                    second = None
                    for t in range(tiles_per_sub):
                        x = buf[slot, p, pl.ds((n * tiles_per_sub + t) * SUBLANES, SUBLANES), :]
                        fa = x * wf_ref[t]
                        sa = x * ws_ref[t]
                        first = fa if first is None else first + fa
                        second = sa if second is None else second + sa
                    first = first + pltpu.roll(first, SUBLANES // 2, 0)
                    second = second + pltpu.roll(second, SUBLANES // 2, 0)
                    o_ref[pl.ds(out_row0 + n * SUBLANES, SUBLANES), :] = _silu(last + second + b_ref[...])
                    last = first
                new_prev.append(last)
            return tuple(new_prev)

        carry = lax.fori_loop(0, pps, page_body, carry)


def _compress_paged(pool_k, pool_v, page_table, pwk, bk, pwv, bv, name):
    n_batch, n_pages = page_table.shape
    n_phys, page = pool_k.shape[:2]
    pps = min(PAGES_PER_CHUNK, n_pages // 2)
    assert N_KV_HEADS * 2 == SUBLANES and n_pages % (2 * pps) == 0 and page % STRIDE == 0
    page_rows = page * N_KV_HEADS
    out_rows = n_pages * (page // STRIDE) * SUBLANES

    def tile_weights(pw):
        return jnp.repeat(pw.reshape(STRIDE // 2, 2, HEAD_DIM), N_KV_HEADS, axis=1)

    full3 = pl.BlockSpec((STRIDE // 2, SUBLANES, HEAD_DIM), lambda b, pt: (0, 0, 0))
    brow = pl.BlockSpec((SUBLANES, HEAD_DIM), lambda b, pt: (0, 0))
    o_spec = pl.BlockSpec((out_rows, HEAD_DIM), lambda b, pt: (b, 0))
    out_shape = jax.ShapeDtypeStruct((n_batch * out_rows, HEAD_DIM), f32)
    buf = pltpu.VMEM((2, pps, page_rows, HEAD_DIM), f32)
    return pl.pallas_call(
        functools.partial(_compress_paged_kernel, n_pages=n_pages, page_rows=page_rows, pps=pps),
        grid_spec=pltpu.PrefetchScalarGridSpec(
            num_scalar_prefetch=1,
            grid=(n_batch,),
            in_specs=[full3, full3, brow, full3, full3, brow,
                      pl.BlockSpec(memory_space=pl.ANY), pl.BlockSpec(memory_space=pl.ANY)],
            out_specs=[o_spec, o_spec],
            scratch_shapes=[buf, buf, pltpu.SemaphoreType.DMA((2, 2))],
        ),
        out_shape=[out_shape, out_shape],
        compiler_params=_params("arbitrary"),
        name=name,
    )(page_table.reshape(-1),
      tile_weights(pwk[:STRIDE]), tile_weights(pwk[STRIDE:]), jnp.tile(bk.reshape(1, HEAD_DIM), (SUBLANES, 1)),
      tile_weights(pwv[:STRIDE]), tile_weights(pwv[STRIDE:]), jnp.tile(bv.reshape(1, HEAD_DIM), (SUBLANES, 1)),
      pool_k.reshape(n_phys, page_rows, HEAD_DIM), pool_v.reshape(n_phys, page_rows, HEAD_DIM))


def _block_overlap(blk, rown):
    c_start = (rown - 1) * STRIDE
    return (rown >= 1) & (c_start < blk * L_SEL + L_SEL) & (c_start + L_CMP > blk * L_SEL)


def _stable_rank(score, score_ref, blk, nblk):
    score_ref[...] = score

    def count_better(i, cnt):
        other = score_ref[pl.ds(i, 1), :]
        better = (other > score) | ((other == score) & (i < blk))
        return cnt + better.astype(f32)

    return lax.fori_loop(0, nblk, count_better, jnp.zeros(score.shape, f32))


def _cmp_select_prompt_kernel(q_ref, ak_ref, av_ref, wk_ref, wv_ref, ocmp_ref, sel_ref, score_ref, *,
                              tq, nrows, nblk):
    pos0 = pl.program_id(2) * tq
    pos_col = pos0 + lax.broadcasted_iota(jnp.int32, (tq, 1), 0)
    pos_row = pos0 + lax.broadcasted_iota(jnp.int32, (1, tq), 1)

    ck = _split2(_dot_3pass(ak_ref[0], wk_ref[...]))
    cv = _dot(av_ref[0].astype(bf16), wv_ref[...].astype(bf16)).astype(bf16)

    rown = lax.broadcasted_iota(jnp.int32, (1, nrows), 1)
    visible = (rown >= 1) & (rown * STRIDE + (L_CMP - STRIDE - 1) <= pos_col)

    pg = jnp.zeros((tq, nrows), f32)
    for r in range(GROUP):
        sl = slice(r * HEAD_DIM, (r + 1) * HEAD_DIM)
        s = _dot_3pass(q_ref[:, sl], ck, mm=_dot_nt) * SCALE
        s = jnp.where(visible, s, NEG_INF)
        e = jnp.where(visible, jnp.exp(s - jnp.max(s, axis=-1, keepdims=True)), 0.0)
        den = jnp.sum(e, axis=-1, keepdims=True)
        p = jnp.where(den > 0.0, e / jnp.where(den > 0.0, den, 1.0), 0.0)
        ocmp_ref[:, sl] = _dot(p.astype(bf16), cv)
        pg = pg + p

    blk = lax.broadcasted_iota(jnp.int32, (nblk, 1), 0)
    imp_t = _dot_mask(_block_overlap(blk, rown).astype(bf16), pg, mm=_dot_nt)
    qb = pos_row // L_SEL
    causal = blk <= qb
    forced = (blk == 0) | (blk == qb) | (blk == qb - 1)
    score = jnp.where(causal, imp_t + FORCE_BONUS * forced.astype(f32), -jnp.inf)
    rank = _stable_rank(score, score_ref, blk, nblk)
    chosen = jnp.where(causal & (rank < float(N_SEL)), 1.0, 0.0).astype(bf16)
    eye = (lax.broadcasted_iota(jnp.int32, (tq, tq), 0)
           == lax.broadcasted_iota(jnp.int32, (tq, tq), 1)).astype(bf16)
    sel_ref[0, 0] = _dot_nt(eye, chosen)


def _cmp_select_prompt(q, ak, av, wk, wv, *, batch, seq, nblk, name):
    nrows = ak.shape[1]
    tq = _tile(seq, (1024, 512, 256, 128))
    n_q = seq // tq
    q_spec = pl.BlockSpec((tq, GROUP_W), lambda b, g, i: (b * n_q + i, g))
    a_spec = pl.BlockSpec((1, nrows, HEAD_DIM), lambda b, g, i: (b, 0, g))
    w_spec = pl.BlockSpec((HEAD_DIM, HEAD_DIM), lambda b, g, i: (0, 0))
    return pl.pallas_call(
        functools.partial(_cmp_select_prompt_kernel, tq=tq, nrows=nrows, nblk=nblk),
        grid=(batch, N_KV_HEADS, n_q),
        in_specs=[q_spec, a_spec, a_spec, w_spec, w_spec],
        out_specs=[pl.BlockSpec((tq, GROUP_W), lambda b, g, i: (b * n_q + i, g)),
                   pl.BlockSpec((1, 1, tq, nblk), lambda b, g, i: (b, g, i, 0))],
        out_shape=[jax.ShapeDtypeStruct((batch * seq, Q_W), f32),
                   jax.ShapeDtypeStruct((batch, N_KV_HEADS, seq, nblk), f32)],
        scratch_shapes=[pltpu.VMEM((nblk, tq), f32)],
        compiler_params=_params("parallel", "parallel", "arbitrary"),
        name=name,
    )(q, ak, av, wk, wv)


def _cmp_sample_kernel(q_ref, ak_ref, av_ref, wk_ref, wv_ref, ocmp_ref, imp_ref, *, nrows, nblk, pos):
    rown = lax.broadcasted_iota(jnp.int32, (nrows, 1), 0)
    visible = (rown >= 1) & (rown * STRIDE + (L_CMP - STRIDE - 1) <= pos)
    overlap = _block_overlap(lax.broadcasted_iota(jnp.int32, (nblk, 1), 0),
                             lax.broadcasted_iota(jnp.int32, (1, nrows), 1)).astype(bf16)
    lane = lax.broadcasted_iota(jnp.int32, (1, SUBLANES), 1)
    is_head = lane < GROUP
    imp_all = jnp.zeros((nblk, SUBLANES), f32)
    for g in range(N_KV_HEADS):
        a_k = ak_ref[pl.ds(g, nrows, stride=SUBLANES), :]
        a_v = av_ref[pl.ds(g, nrows, stride=SUBLANES), :]
        ck = _dot_3pass(a_k, wk_ref[...])
        cv = _dot(a_v.astype(bf16), wv_ref[...].astype(bf16))
        s = _dot_3pass(ck, q_ref[0, g], mm=_dot_nt) * SCALE
        s = jnp.where(visible, s, NEG_INF)
        e = jnp.where(visible, jnp.exp(s - jnp.max(s, axis=0, keepdims=True)), 0.0)
        den = jnp.sum(e, axis=0, keepdims=True)
        p = jnp.where(is_head & (den > 0.0), e / jnp.where(den > 0.0, den, 1.0), 0.0)
        for r in range(GROUP):
            ocmp_ref[0, g, r:r + 1, :] = jnp.sum(p[:, r:r + 1] * cv, axis=0, keepdims=True)
        ocmp_ref[0, g, GROUP:, :] = jnp.zeros((SUBLANES - GROUP, HEAD_DIM), f32)
        imp = jnp.sum(_dot_mask(overlap, p), axis=1, keepdims=True)
        imp_all = jnp.where(lane == g, imp, imp_all)
    imp_ref[0] = imp_all


def _cmp_sample(q, ak, av, wk, wv, *, n_batch, nrows, nblk, pos, name):
    q_spec = pl.BlockSpec((1, N_KV_HEADS, SUBLANES, HEAD_DIM), lambda b: (b, 0, 0, 0))
    a_spec = pl.BlockSpec((nrows * SUBLANES, HEAD_DIM), lambda b: (b, 0))
    w_spec = pl.BlockSpec((HEAD_DIM, HEAD_DIM), lambda b: (0, 0))
    return pl.pallas_call(
        functools.partial(_cmp_sample_kernel, nrows=nrows, nblk=nblk, pos=pos),
        grid=(n_batch,),
        in_specs=[q_spec, a_spec, a_spec, w_spec, w_spec],
        out_specs=[q_spec, pl.BlockSpec((1, nblk, SUBLANES), lambda b: (b, 0, 0))],
        out_shape=[jax.ShapeDtypeStruct((n_batch, N_KV_HEADS, SUBLANES, HEAD_DIM), f32),
                   jax.ShapeDtypeStruct((n_batch, nblk, SUBLANES), f32)],
        compiler_params=_params("parallel"),
        name=name,
    )(q, ak, av, wk, wv)


def _rank_sample_kernel(imp_ref, idx_ref, score_ref, *, nblk, qb):
    blk = lax.broadcasted_iota(jnp.int32, (nblk, 1), 0)
    causal = blk <= qb
    forced = (blk == 0) | (blk == qb) | (blk == qb - 1)
    score = jnp.where(causal, imp_ref[...] + FORCE_BONUS * forced.astype(f32), -jnp.inf)
    rank = _stable_rank(score, score_ref, blk, nblk)
    for slot in range(N_SEL):
        hit = causal & (rank == float(slot))
        idx_ref[slot:slot + 1, :] = jnp.sum(jnp.where(hit, blk, 0), axis=0, keepdims=True)
        idx_ref[N_SEL + slot:N_SEL + slot + 1, :] = jnp.max(hit.astype(jnp.int32), axis=0, keepdims=True)


def _rank_sample(imp_t, *, nblk, qb, name):
    lanes = imp_t.shape[1]
    return pl.pallas_call(
        functools.partial(_rank_sample_kernel, nblk=nblk, qb=qb),
        grid=(1,),
        in_specs=[pl.BlockSpec((nblk, lanes), lambda i: (0, 0))],
        out_specs=pl.BlockSpec((2 * N_SEL, lanes), lambda i: (0, 0)),
        out_shape=jax.ShapeDtypeStruct((2 * N_SEL, lanes), jnp.int32),
        scratch_shapes=[pltpu.VMEM((nblk, lanes), f32)],
        compiler_params=_params("arbitrary"),
        name=name,
    )(imp_t)


def _softmax_pv(s, v, exp=jnp.exp):
    e = exp(s - jnp.max(s, axis=-1, keepdims=True))
    den = jnp.sum(e, axis=-1, keepdims=True)
    return _dot(e.astype(bf16), v) / den


LOG2_E = 1.4426950408889634


def _masked_attention(q, k, v, mask, tq):
    heads = q.shape[0] // tq
    keys = k.shape[0]
    s = _dot_nt(q, k)
    s = jnp.where(mask[None], s.reshape(heads, tq, keys), NEG_INF).reshape(heads * tq, keys)
    return _softmax_pv(s, v, exp=jnp.exp2)


HEADS_PER_CHAIN = 2


def _nsa_prompt_kernel(q_ref, c_ref, sa_ref, sb_ref, ks_ref, vs_ref, kw_ref, vw_ref, sel_ref, ocmp_ref,
                       gate_ref, o_ref, *, tq, seq, nblk, span, n_q):
    qi = pl.program_id(2)
    t0 = qi * tq
    qr = (_rope(q_ref[...], c_ref[...], sa_ref[...], sb_ref[...]) * (SCALE * LOG2_E)).astype(bf16)
    n_chain = GROUP // HEADS_PER_CHAIN
    chains = [jnp.concatenate([qr[:, (c * HEADS_PER_CHAIN + h) * HEAD_DIM:(c * HEADS_PER_CHAIN + h + 1) * HEAD_DIM]
                               for h in range(HEADS_PER_CHAIN)], axis=0) for c in range(n_chain)]
    tpos = t0 + lax.broadcasted_iota(jnp.int32, (tq, 1), 0)

    start = pl.multiple_of(jnp.maximum(t0 - WINDOW, 0), tq)
    wpos = start + lax.broadcasted_iota(jnp.int32, (1, span), 1)
    wmask = (wpos <= tpos) & (wpos > tpos - WINDOW)
    kw = kw_ref[pl.ds(start, span), :].astype(bf16)
    vw = vw_ref[pl.ds(start, span), :].astype(bf16)
    o_win = [_masked_attention(qc, kw, vw, wmask, tq) for qc in chains]

    gates = _sigmoid(gate_ref[...])
    sel = sel_ref[0, 0].astype(bf16)

    tiles_per_class = 2
    for cls in range(-(-n_q // tiles_per_class)):
        ext = min(seq, (cls + 1) * tiles_per_class * tq)

        @pl.when(qi // tiles_per_class == cls)
        def _(ext=ext):
            kpos = lax.broadcasted_iota(jnp.int32, (1, ext), 1)
            expand = (kpos // L_SEL == lax.broadcasted_iota(jnp.int32, (nblk, 1), 0)).astype(bf16)
            mask = (_dot(sel, expand) > 0.5) & (kpos <= tpos)
            k = ks_ref[0:ext, :].astype(bf16)
            v = vs_ref[0:ext, :].astype(bf16)
            for c, qc in enumerate(chains):
                o_slc = _masked_attention(qc, k, v, mask, tq)
                for h in range(HEADS_PER_CHAIN):
                    r = c * HEADS_PER_CHAIN + h
                    sl = slice(r * HEAD_DIM, (r + 1) * HEAD_DIM)
                    rows = slice(h * tq, (h + 1) * tq)
                    o = (gates[:, 3 * r:3 * r + 1] * ocmp_ref[:, sl] + gates[:, 3 * r + 1:3 * r + 2] * o_slc[rows]
                         + gates[:, 3 * r + 2:3 * r + 3] * o_win[c][rows])
                    o_ref[:, sl] = o.astype(o_ref.dtype)


def _nsa_prompt(z1, tables, ks, vs, kw, vw, sel, ocmp, gates, *, batch, seq, nblk, name):
    tq = _tile(seq, (256, 128))
    n_q = seq // tq
    span = min(WINDOW + tq, seq)
    row_blk = lambda w: pl.BlockSpec((tq, w), lambda b, g, i: (b * n_q + i, g))
    tab = pl.BlockSpec((tq, GROUP_W), lambda b, g, i: (i, 0))
    kv = pl.BlockSpec((seq, HEAD_DIM), lambda b, g, i: (b, g))
    return pl.pallas_call(
        functools.partial(_nsa_prompt_kernel, tq=tq, seq=seq, nblk=nblk, span=span, n_q=n_q),
        grid=(batch, N_KV_HEADS, n_q),
        in_specs=[row_blk(GROUP_W), tab, tab, tab, kv, kv, kv, kv,
                  pl.BlockSpec((1, 1, tq, nblk), lambda b, g, i: (b, g, i, 0)),
                  row_blk(GROUP_W),
                  pl.BlockSpec((None, tq, 3 * GROUP), lambda b, g, i: (g, b * n_q + i, 0))],
        out_specs=row_blk(GROUP_W),
        out_shape=jax.ShapeDtypeStruct((batch * seq, Q_W), bf16),
        compiler_params=_params("parallel", "parallel", "arbitrary"),
        name=name,
    )(z1, *tables, ks, vs, kw, vw, sel, ocmp, gates)


def _nsa_sample_kernel(idx_ref, pt_ref, q_ref, c_ref, sa_ref, sb_ref, ksn_ref, vsn_ref, kwin_ref, vwin_ref,
                       ocmp_ref, gate_ref, poolk_ref, poolv_ref, o_ref, kbuf, vbuf, sems, *,
                       n_pages, page, pos, window):
    b = pl.program_id(0)
    per_page = page // L_SEL
    past_blocks = n_pages * per_page
    n_keys = N_SEL * L_SEL

    def gather_copies(seq):
        half = seq % 2
        out = []
        for g in range(N_KV_HEADS):
            for slot in range(N_SEL):
                blk = idx_ref[(seq * N_KV_HEADS + g) * 2 * N_SEL + slot]
                jp = jnp.minimum(blk, past_blocks - 1)
                phys = pt_ref[seq * n_pages + jp // per_page]
                off = pl.multiple_of((jp % per_page) * L_SEL, L_SEL)
                dst = pl.ds(slot * L_SEL, L_SEL)
                out.append(pltpu.make_async_copy(poolk_ref.at[phys, pl.ds(off, L_SEL), g], kbuf.at[half, g, dst],
                                                 sems.at[0, half]))
                out.append(pltpu.make_async_copy(poolv_ref.at[phys, pl.ds(off, L_SEL), g], vbuf.at[half, g, dst],
                                                 sems.at[1, half]))
        return out

    @pl.when(b == 0)
    def _():
        for cp in gather_copies(b):
            cp.start()

    @pl.when(b + 1 < pl.num_programs(0))
    def _():
        for cp in gather_copies(b + 1):
            cp.start()

    for cp in gather_copies(b):
        cp.wait()
    half = b % 2

    lane = lax.broadcasted_iota(jnp.int32, (1, n_keys), 1)
    lane_slot = lane // L_SEL
    for g in range(N_KV_HEADS):
        qr = _rope(q_ref[0, g], c_ref[...], sa_ref[...], sb_ref[...]).astype(bf16)
        blk_vec = jnp.zeros((1, n_keys), jnp.int32)
        ok_vec = jnp.zeros((1, n_keys), jnp.int32)
        base = (b * N_KV_HEADS + g) * 2 * N_SEL
        for slot in range(N_SEL):
            blk_vec = jnp.where(lane_slot == slot, idx_ref[base + slot], blk_vec)
            ok_vec = jnp.where(lane_slot == slot, idx_ref[base + N_SEL + slot], ok_vec)
        in_past = blk_vec < past_blocks
        kpos = blk_vec * L_SEL + lane % L_SEL
        mask = (ok_vec > 0) & (kpos <= pos)
        k_new = ksn_ref[0, g:g + 1, :].astype(bf16).astype(f32)
        v_new = vsn_ref[0, g:g + 1, :].astype(bf16).astype(f32)
        s_new = jnp.sum(qr.astype(f32) * k_new, axis=-1, keepdims=True)
        s = jnp.where(in_past, _dot_nt(qr, kbuf[half, g].astype(bf16)), s_new) * SCALE
        s = jnp.where(mask, s, NEG_INF)
        e = jnp.exp(s - jnp.max(s, axis=-1, keepdims=True))
        den = jnp.sum(e, axis=-1, keepdims=True)
        e_new = jnp.sum(jnp.where(in_past, 0.0, e), axis=-1, keepdims=True)
        e_past = jnp.where(in_past, e, 0.0).astype(bf16)
        o_slc = (_dot(e_past, vbuf[half, g].astype(bf16)) + e_new.astype(bf16).astype(f32) * v_new) / den
        win_rows = pl.ds(g, window, stride=N_KV_HEADS)
        o_win = _softmax_pv(_dot_nt(qr, kwin_ref[win_rows, :].astype(bf16)) * SCALE,
                            vwin_ref[win_rows, :].astype(bf16))
        gt = _sigmoid(gate_ref[0, g])
        o_ref[0, g] = gt[:, 0:1] * ocmp_ref[0, g] + gt[:, 1:2] * o_slc + gt[:, 2:3] * o_win


def _nsa_sample(idx, page_table, q, tables, ks_new, vs_new, kwin, vwin, ocmp, gates, pool_k, pool_v, *,
                pos, name):
    n_batch, n_pages = page_table.shape
    page = pool_k.shape[1]
    window = kwin.shape[0] // (n_batch * N_KV_HEADS)
    tab = pl.BlockSpec((SUBLANES, HEAD_DIM), lambda b, *_: (0, 0))
    heads = pl.BlockSpec((1, N_KV_HEADS, SUBLANES, HEAD_DIM), lambda b, *_: (b, 0, 0, 0))
    new_row = pl.BlockSpec((1, N_KV_HEADS, HEAD_DIM), lambda b, *_: (b, 0, 0))
    win = pl.BlockSpec((window * N_KV_HEADS, HEAD_DIM), lambda b, *_: (b, 0))
    return pl.pallas_call(
        functools.partial(_nsa_sample_kernel, n_pages=n_pages, page=page, pos=pos, window=window),
        grid_spec=pltpu.PrefetchScalarGridSpec(
            num_scalar_prefetch=2,
            grid=(n_batch,),
            in_specs=[heads, tab, tab, tab, new_row, new_row, win, win, heads, heads,
                      pl.BlockSpec(memory_space=pl.ANY), pl.BlockSpec(memory_space=pl.ANY)],
            out_specs=heads,
            scratch_shapes=[pltpu.VMEM((2, N_KV_HEADS, N_SEL * L_SEL, HEAD_DIM), f32),
                            pltpu.VMEM((2, N_KV_HEADS, N_SEL * L_SEL, HEAD_DIM), f32),
                            pltpu.SemaphoreType.DMA((2, 2))],
        ),
        out_shape=jax.ShapeDtypeStruct((n_batch, N_KV_HEADS, SUBLANES, HEAD_DIM), f32),
        compiler_params=_params("arbitrary"),
        name=name,
    )(idx, page_table.reshape(-1), q, *tables, ks_new, vs_new, kwin, vwin, ocmp, gates, pool_k, pool_v)


def _window_update_kernel(kin_ref, vin_ref, knew_ref, vnew_ref, kout_ref, vout_ref):
    rows = kin_ref.shape[1]
    keep = rows - N_KV_HEADS
    for src, new, dst in ((kin_ref, knew_ref, kout_ref), (vin_ref, vnew_ref, vout_ref)):
        dst[0, 0:keep, :] = src[0, N_KV_HEADS:rows, :]
        dst[0, keep:rows, :] = new[0]


def _window_update(win_k, win_v, k_new, v_new, name):
    n_batch, rows, d = win_k.shape
    win = pl.BlockSpec((1, rows, d), lambda b: (b, 0, 0))
    new = pl.BlockSpec((1, N_KV_HEADS, d), lambda b: (b, 0, 0))
    shape = jax.ShapeDtypeStruct(win_k.shape, win_k.dtype)
    return pl.pallas_call(
        _window_update_kernel,
        grid=(n_batch,),
        in_specs=[win, win, new, new],
        out_specs=[win, win],
        out_shape=[shape, shape],
        compiler_params=_params("parallel"),
        name=name,
    )(win_k, win_v, k_new, v_new)


def _sgu_prompt_kernel(u_ref, v_ref, g_ref, w_ref, bt_ref, o_ref):
    v = v_ref[...].astype(f32)
    vn = (v * lax.rsqrt(jnp.mean(v * v, axis=-1, keepdims=True) + EPS) * g_ref[...]).astype(bf16)
    tril = (lax.broadcasted_iota(jnp.int32, (SG_CHUNK, SG_CHUNK), 0)
            >= lax.broadcasted_iota(jnp.int32, (SG_CHUNK, SG_CHUNK), 1))
    bt = bt_ref[...]
    group_dim = v.shape[-1] // SG_GROUPS
    for gi in range(SG_GROUPS):
        sl = slice(gi * group_dim, (gi + 1) * group_dim)
        w = jnp.where(tril, w_ref[gi], 0.0).astype(bf16)
        mixed = _dot(w, vn[:, sl]) + bt[:, gi:gi + 1]
        o_ref[:, sl] = (u_ref[:, sl].astype(f32) * mixed).astype(o_ref.dtype)


def _sgu_prompt(uv, norm_g, w_s, b_s, name):
    m, two_w = uv.shape
    width = two_w // 2
    return pl.pallas_call(
        _sgu_prompt_kernel,
        grid=(m // SG_CHUNK,),
        in_specs=[pl.BlockSpec((SG_CHUNK, width), lambda i: (i, 0)),
                  pl.BlockSpec((SG_CHUNK, width), lambda i: (i, 1)),
                  pl.BlockSpec((1, width), lambda i: (0, 0)),
                  pl.BlockSpec((SG_GROUPS, SG_CHUNK, SG_CHUNK), lambda i: (0, 0, 0)),
                  pl.BlockSpec((SG_CHUNK, SG_GROUPS), lambda i: (0, 0))],
        out_specs=pl.BlockSpec((SG_CHUNK, width), lambda i: (i, 0)),
        out_shape=jax.ShapeDtypeStruct((m, width), bf16),
        compiler_params=_params("parallel"),
        name=name,
    )(uv, uv, norm_g.reshape(1, width), w_s, b_s.T)


def _sgu_sample_kernel(u_ref, v_ref, g_ref, w0_ref, b0_ref, o_ref, vn_ref):
    v = v_ref[...]
    vn = v * lax.rsqrt(jnp.mean(v * v, axis=-1, keepdims=True) + EPS) * g_ref[...]
    vn_ref[...] = vn
    o_ref[...] = (u_ref[...] * (vn * w0_ref[...] + b0_ref[...])).astype(o_ref.dtype)


def _sgu_sample(uv, norm_g, w_s, b_s, name):
    m, two_w = uv.shape
    width = two_w // 2
    group_dim = width // SG_GROUPS
    w0 = jnp.repeat(w_s[:, 0, 0], group_dim).reshape(1, width)
    b0 = jnp.repeat(b_s[:, 0], group_dim).reshape(1, width)
    row = pl.BlockSpec((1, width), lambda i: (0, 0))
    return pl.pallas_call(
        _sgu_sample_kernel,
        grid=(1,),
        in_specs=[pl.BlockSpec((m, width), lambda i: (0, 0)), pl.BlockSpec((m, width), lambda i: (0, 1)),
                  row, row, row],
        out_specs=[pl.BlockSpec((m, width), lambda i: (0, 0))] * 2,
        out_shape=[jax.ShapeDtypeStruct((m, width), bf16), jax.ShapeDtypeStruct((m, width), f32)],
        compiler_params=_params("arbitrary"),
        name=name,
    )(uv, uv, norm_g.reshape(1, width), w0, b0)


CONV_HALO = SUBLANES


def _ffn_up_kernel(xp_ref, xs_ref, wa_ref, wb_ref, cwa_ref, cwb_ref, cba_ref, cbb_ref,
                   h0a_ref, h0b_ref, h1a_ref, h1b_ref,
                   act_ref, ta_ref, tb_ref, acts_ref, upa_ref, upb_ref,
                   wabf_ref, wbbf_ref, bufa, bufb, *, tm, tiles_per_seq, n_m):
    mi = pl.program_id(1)

    @pl.when(mi == 0)
    def _():
        _cast_weight_tile(wa_ref, wabf_ref)
        _cast_weight_tile(wb_ref, wbbf_ref)

    @pl.when(mi % tiles_per_seq == 0)
    def _():
        bufa[...] = jnp.zeros_like(bufa)
        bufb[...] = jnp.zeros_like(bufb)

    x = xp_ref[...]
    halves = []
    for wbf_ref, cw_ref, cb_ref, halo_ref, tail_ref in ((wabf_ref, cwa_ref, cba_ref, bufa, ta_ref),
                                                        (wbbf_ref, cwb_ref, cbb_ref, bufb, tb_ref)):
        up = _dot(x, wbf_ref[...])
        w0, w1, w2, cb = cw_ref[0:1, :], cw_ref[1:2, :], cw_ref[2:3, :], cb_ref[...]
        t = up * w1 + pltpu.roll(up * w0, 1, 0)
        conv = cb + up * w2 + pltpu.roll(t, 1, 0)
        ext = jnp.concatenate([halo_ref[...], up[0:CONV_HALO, :]], axis=0)
        lo = CONV_HALO
        head = (cb + ext[lo:2 * lo, :] * w2 + ext[lo - 1:2 * lo - 1, :] * w1 + ext[lo - 2:2 * lo - 2, :] * w0)
        halves.append(jnp.concatenate([head, conv[CONV_HALO:, :]], axis=0))
        tail_ref[0] = up[tm - (CONV_W - 1):, :]
        halo_ref[...] = up[tm - CONV_HALO:, :]
    act_ref[...] = (_silu(halves[0]) * halves[1]).astype(act_ref.dtype)

    @pl.when(mi == n_m - 1)
    def _():
        xs = xs_ref[...]
        halves = []
        for wbf_ref, cw_ref, cb_ref, h0_ref, h1_ref, up_ref in (
                (wabf_ref, cwa_ref, cba_ref, h0a_ref, h1a_ref, upa_ref),
                (wbbf_ref, cwb_ref, cbb_ref, h0b_ref, h1b_ref, upb_ref)):
            up = _dot(xs, wbf_ref[...])
            up_ref[...] = up
            conv = cb_ref[...] + h0_ref[...] * cw_ref[0:1, :]
            conv = conv + h1_ref[...] * cw_ref[1:2, :]
            halves.append(conv + up * cw_ref[2:3, :])
        acts_ref[...] = (_silu(halves[0]) * halves[1]).astype(acts_ref.dtype)


def _ffn_up(hn_p, hn_s, w_up, conv_w, conv_b, hist0, hist1, batch, seq, name):
    mp, d = hn_p.shape
    ms = hn_s.shape[0]
    d_ff = w_up.shape[1] // 2
    tm = _tile(seq, (1024, 512, 256, 128))
    tn = _tile(d_ff, (256, 128))
    nb = d_ff // tn
    n_m = mp // tm
    tiles_per_seq = seq // tm
    col = lambda shape, off: pl.BlockSpec(shape, lambda j, i: (0, j + off))
    tail = pl.BlockSpec((1, CONV_W - 1, tn), lambda j, i: (i // tiles_per_seq, 0, j))
    tail_shape = jax.ShapeDtypeStruct((batch, CONV_W - 1, d_ff), f32)
    up_shape = jax.ShapeDtypeStruct((ms, d_ff), f32)
    buf = pltpu.VMEM((CONV_HALO, tn), f32)
    return pl.pallas_call(
        functools.partial(_ffn_up_kernel, tm=tm, tiles_per_seq=tiles_per_seq, n_m=n_m),
        grid=(nb, n_m),
        in_specs=[pl.BlockSpec((tm, d), lambda j, i: (i, 0)), pl.BlockSpec((ms, d), lambda j, i: (0, 0)),
                  col((d, tn), 0), col((d, tn), nb),
                  col((CONV_W, tn), 0), col((CONV_W, tn), nb), col((1, tn), 0), col((1, tn), nb),
                  col((ms, tn), 0), col((ms, tn), nb), col((ms, tn), 0), col((ms, tn), nb)],
        out_specs=[pl.BlockSpec((tm, tn), lambda j, i: (i, j)), tail, tail,
                   col((ms, tn), 0), col((ms, tn), 0), col((ms, tn), 0)],
        out_shape=[jax.ShapeDtypeStruct((mp, d_ff), bf16), tail_shape, tail_shape,
                   jax.ShapeDtypeStruct((ms, d_ff), bf16), up_shape, up_shape],
        scratch_shapes=[pltpu.VMEM((d, tn), bf16), pltpu.VMEM((d, tn), bf16), buf, buf],
        compiler_params=_params("parallel", "arbitrary"),
        name=name,
    )(hn_p, hn_s, w_up, w_up, conv_w, conv_w, conv_b, conv_b, hist0, hist0, hist1, hist1)


def kernel(x_prompt, x_sample, cache_k_cmp, cache_v_cmp, cache_k_slc, cache_v_slc, cache_k_win, cache_v_win,
           state_ffn_conv, page_table, norm1_g, w_in, cmp_pool_k, cmp_bias_k, cmp_w_k, cmp_pool_v, cmp_bias_v,
           cmp_w_v, sg_norm_g, sg_w, sg_b, w_proj_a, w_proj_b, w_out, norm2_g, w_up, conv_w, conv_b, w_down,
           norm_f_g):
    batch, seq, d_model = x_prompt.shape
    n_dec, dec_seq, _ = x_sample.shape
    depth = w_in.shape[0]
    page = cache_k_cmp.shape[2]
    past = page_table.shape[1] * page
    assert dec_seq == 1 and cache_k_win.shape[2] == WINDOW and past >= WINDOW
    assert seq % SG_CHUNK == 0 and seq % L_SEL == 0 and past % L_SEL == 0
    nblk_p = max(seq // L_SEL, N_SEL)
    nblk_s = -(-max(-(-(past + 1) // L_SEL), N_SEL) // SUBLANES) * SUBLANES

    tables_p = _rope_tables(jnp.arange(seq), GROUP)
    tables_s_kv = _rope_tables(jnp.full((n_dec,), past), GROUP)
    tables_s_q = _rope_tables(jnp.full((SUBLANES,), past), 1)

    hp = x_prompt.reshape(batch * seq, d_model)
    hs = x_sample.reshape(n_dec, d_model)
    new_p = [[] for _ in range(7)]
    new_s = [[] for _ in range(8)]
    for l in range(depth):
        pwk, bk, pwv, bv = cmp_pool_k[l], cmp_bias_k[l], cmp_pool_v[l], cmp_bias_v[l]
        cw = conv_w[l]
        cb = conv_b[l].reshape(1, -1)

        xn_p = _rmsnorm(hp, norm1_g[l], bf16, "norm1_p")
        xn_s = _rmsnorm(hs, norm1_g[l], bf16, "norm1_s")
        w_in_nk = jnp.swapaxes(w_in[l], 0, 1)
        z1_p, z1_s = _matmul_wres(xn_p, xn_s, w_in_nk, Z1_W, (f32, f32), "in_proj_qkv", w_is_nk=True)
        uv_p, uv_s = _matmul_wres(xn_p, xn_s, w_in_nk, d_model, (bf16, f32), "in_proj_uv", w_is_nk=True,
                                  col0=QKV_W + NSA_GATE_W, epilogue=_gelu_tanh)
        gm_p, gm_s = _matmul_wres(xn_p, xn_s, w_in_nk, 2 * d_model, (bf16, bf16), "in_proj_gate", w_is_nk=True,
                                  col0=QKV_W + NSA_GATE_W + d_model, epilogue=_sigmoid)

        kc_p, vc_p, ks_p, vs_p, kw_p, vw_p, ksb, vsb, kwb, vwb = _prep_kv(z1_p, tables_p, seq, "prep_p")
        ak, av = _compress_prompt(kc_p, vc_p, pwk, bk, pwv, bv, batch, seq, "compress_p")
        ocmp, sel = _cmp_select_prompt(z1_p, ak, av, cmp_w_k[l], cmp_w_v[l], batch=batch, seq=seq, nblk=nblk_p,
                                       name="cmp_select_p")
        gates = z1_p[:, QKV_W:QKV_W + NSA_GATE_W].reshape(batch * seq, N_KV_HEADS, 3 * GROUP).transpose(1, 0, 2)
        nsa_p = _nsa_prompt(z1_p, tables_p, ksb, vsb, kwb, vwb, sel, ocmp, gates, batch=batch, seq=seq,
                            nblk=nblk_p, name="nsa_p")
        sg_p = _sgu_prompt(uv_p, sg_norm_g[l], sg_w[l], sg_b[l], "sgu_p")

        kc_s, vc_s, ks_s, vs_s, kw_s, vw_s = _prep_kv(z1_s, tables_s_kv, n_dec, "prep_s")[:6]
        ak, av = _compress_paged(cache_k_cmp[l], cache_v_cmp[l], page_table, pwk, bk, pwv, bv, "compress_s")
        heads_as_rows = lambda t, w: jnp.pad(t.reshape(n_dec, N_KV_HEADS, GROUP, w),
                                             ((0, 0), (0, 0), (0, SUBLANES - GROUP), (0, HEAD_DIM - w)))
        q_s = heads_as_rows(z1_s[:, :Q_W], HEAD_DIM)
        ocmp, imp = _cmp_sample(q_s, ak, av, cmp_w_k[l], cmp_w_v[l], n_batch=n_dec, nrows=past // STRIDE,
                                nblk=nblk_s, pos=past, name="cmp_s")
        imp_t = imp.transpose(1, 0, 2).reshape(nblk_s, n_dec * SUBLANES)
        idx = _rank_sample(imp_t, nblk=nblk_s, qb=past // L_SEL, name="rank_s")
        idx = idx.T.reshape(n_dec, SUBLANES, 2 * N_SEL)[:, :N_KV_HEADS].reshape(-1)
        new_rows = lambda t: t.reshape(n_dec, N_KV_HEADS, HEAD_DIM)
        cache_rows = lambda t: t.reshape(n_dec, WINDOW * N_KV_HEADS, HEAD_DIM)
        win_k, win_v = _window_update(cache_rows(cache_k_win[l]), cache_rows(cache_v_win[l]), new_rows(kw_s),
                                      new_rows(vw_s), "window_s")
        gates = heads_as_rows(z1_s[:, QKV_W:QKV_W + NSA_GATE_W], 3)
        nsa_s = _nsa_sample(idx, page_table, q_s, tables_s_q, new_rows(ks_s), new_rows(vs_s),
                            win_k.reshape(-1, HEAD_DIM), win_v.reshape(-1, HEAD_DIM), ocmp, gates,
                            cache_k_slc[l], cache_v_slc[l], pos=past, name="nsa_s")
        nsa_s = nsa_s[:, :, :GROUP].reshape(n_dec, Q_W).astype(bf16)
        sg_s, sg_v = _sgu_sample(uv_s, sg_norm_g[l], sg_w[l], sg_b[l], "sgu_s")

        m_p, m_s = _merge(nsa_p, nsa_s, sg_p, sg_s, w_proj_a[l], w_proj_b[l], gm_p, gm_s, "merge")
        h_p, h_s = _matmul_wres(m_p, m_s, w_out[l], d_model, (f32, f32), "out_proj",
                                epilogue=lambda acc, x: x + acc, extras_p=(hp,), extras_s=(hs,))
        hn_p = _rmsnorm(h_p, norm2_g[l], bf16, "norm2_p")
        hn_s = _rmsnorm(h_s, norm2_g[l], bf16, "norm2_s")
        hist = state_ffn_conv[l]
        act_p, tail_a, tail_b, act_s, up_a, up_b = _ffn_up(hn_p, hn_s, w_up[l], cw, cb, hist[:, 0], hist[:, 1],
                                                           batch, seq, "ffn_up")
        wdown = w_down[l].astype(bf16)
        hp = _matmul_residual(act_p, wdown, h_p, "ffn_down_p")
        hs = _matmul_residual(act_s, wdown, h_s, "ffn_down_s")

        kv5 = lambda t: t.reshape(batch, seq, N_KV_HEADS, HEAD_DIM)
        keep_p = min(WINDOW, seq)
        for lst, val in zip(new_p, (kv5(kc_p), kv5(vc_p), kv5(ks_p), kv5(vs_p), kv5(kw_p)[:, seq - keep_p:],
                                    kv5(vw_p)[:, seq - keep_p:], jnp.concatenate([tail_a, tail_b], axis=-1))):
            lst.append(val)
        kv5 = lambda t: t.reshape(n_dec, 1, N_KV_HEADS, HEAD_DIM)
        conv_s = jnp.stack([hist[:, 1], jnp.concatenate([up_a, up_b], axis=-1)], axis=1)
        win5 = lambda t: t.reshape(n_dec, WINDOW, N_KV_HEADS, HEAD_DIM)
        for lst, val in zip(new_s, (kv5(kc_s), kv5(vc_s), kv5(ks_s), kv5(vs_s), win5(win_k), win5(win_v),
                                    sg_v.reshape(n_dec, 1, -1), conv_s)):
            lst.append(val)

    y_prompt = _rmsnorm(hp, norm_f_g, f32, "norm_f_p").reshape(batch, seq, d_model)
    y_sample = _rmsnorm(hs, norm_f_g, f32, "norm_f_s").reshape(n_dec, 1, d_model)
    return (y_prompt, y_sample, *(jnp.stack(v) for v in new_p), *(jnp.stack(v) for v in new_s))
```

```python
import functools

import jax
import jax.numpy as jnp
from jax import lax
from jax.experimental import pallas as pl
from jax.experimental.pallas import tpu as pltpu

f32 = jnp.float32
bf16 = jnp.bfloat16

N_HEADS = 16
N_KV_HEADS = 4
GROUP = N_HEADS // N_KV_HEADS
HEAD_DIM = 128
ROT_DIM = HEAD_DIM // 4
ROT_HALF = ROT_DIM // 2
ROPE_THETA = 500000.0
L_CMP = 32
STRIDE = 16
L_SEL = 64
N_SEL = 16
WINDOW = 512
FORCE_BONUS = 1000.0
SCALE = HEAD_DIM ** -0.5
NEG_INF = -1e30
SG_GROUPS = 16
SG_CHUNK = 128
CONV_W = 3
EPS = 1e-6

SUBLANES = 8
LANES = 128
GROUP_W = GROUP * HEAD_DIM
Q_W = N_HEADS * HEAD_DIM
KV_W = N_KV_HEADS * HEAD_DIM
NSA_GATE_W = 3 * N_HEADS
QKV_W = Q_W + 6 * KV_W
Z1_W = QKV_W + GROUP_W
COL_KC, COL_VC, COL_KS, COL_VS, COL_KW, COL_VW, COL_GATE = (Q_W // GROUP_W + i for i in range(7))

VMEM_LIMIT_BYTES = 56 * 1024 * 1024


def _params(*semantics):
    return pltpu.CompilerParams(dimension_semantics=semantics, vmem_limit_bytes=VMEM_LIMIT_BYTES)


def _tile(n, candidates):
    for c in candidates:
        if n % c == 0:
            return c
    return n


def _sigmoid(x):
    return 0.5 * jnp.tanh(0.5 * x) + 0.5


def _silu(x):
    return x * _sigmoid(x)


def _gelu_tanh(x):
    return x * (0.5 * (1.0 + jnp.tanh(0.7978845608028654 * (x + 0.044715 * (x * x * x)))))


def _dot(a, b, precision=None):
    return jnp.dot(a, b, preferred_element_type=f32, precision=precision)


def _dot_nt(a, b, precision=None):
    return lax.dot_general(a, b, (((1,), (1,)), ((), ())), precision=precision, preferred_element_type=f32)


def _split2(x):
    hi = x.astype(bf16)
    return hi, (x - hi.astype(f32)).astype(bf16)


def _dot_3pass(a, b, mm=_dot):
    a_hi, a_lo = a if isinstance(a, tuple) else _split2(a)
    b_hi, b_lo = b if isinstance(b, tuple) else _split2(b)
    return mm(a_hi, b_hi) + (mm(a_hi, b_lo) + mm(a_lo, b_hi))


def _dot_mask(m, x, mm=_dot):
    hi = x.astype(bf16)
    mid, lo = _split2(x - hi.astype(f32))
    return mm(m, hi) + (mm(m, mid) + mm(m, lo))


def _rope(x, c, sa, sb):
    lanes = x.shape[-1]
    return x * c + pltpu.roll(x, lanes - ROT_HALF, 1) * sa + pltpu.roll(x, ROT_HALF, 1) * sb


def _rope_tables(pos, reps):
    n = pos.shape[0]
    inv_freq = jnp.power(jnp.float32(ROPE_THETA), -jnp.arange(ROT_HALF, dtype=f32) / ROT_HALF)
    ang = pos.astype(f32)[:, None] * inv_freq[None, :]
    cos, sin = jnp.cos(ang), jnp.sin(ang)
    zeros = lambda w: jnp.zeros((n, w), f32)
    c = jnp.concatenate([cos, cos, jnp.ones((n, HEAD_DIM - ROT_DIM), f32)], axis=1)
    sa = jnp.concatenate([-sin, zeros(HEAD_DIM - ROT_HALF)], axis=1)
    sb = jnp.concatenate([zeros(ROT_HALF), sin, zeros(HEAD_DIM - ROT_DIM)], axis=1)
    return tuple(jnp.tile(t, (1, reps)) for t in (c, sa, sb))


def _rmsnorm_kernel(x_ref, g_ref, o_ref):
    x = x_ref[...].astype(f32)
    y = x * lax.rsqrt(jnp.mean(x * x, axis=-1, keepdims=True) + EPS)
    o_ref[...] = (y * g_ref[...]).astype(o_ref.dtype)


def _rmsnorm(x, g, out_dtype, name):
    m, d = x.shape
    tm = _tile(m, (256, 128, 64, 32, 16, 8))
    return pl.pallas_call(
        _rmsnorm_kernel,
        grid=(m // tm,),
        in_specs=[pl.BlockSpec((tm, d), lambda i: (i, 0)), pl.BlockSpec((1, d), lambda i: (0, 0))],
        out_specs=pl.BlockSpec((tm, d), lambda i: (i, 0)),
        out_shape=jax.ShapeDtypeStruct((m, d), out_dtype),
        compiler_params=_params("parallel"),
        name=name,
    )(x, g.reshape(1, d).astype(f32))


CAST_ROWS = 128


def _cast_weight_tile(w_ref, wbf_ref):
    rows_total = w_ref.shape[0]
    step = _tile(rows_total, (CAST_ROWS,))
    for r0 in range(0, rows_total, step):
        wbf_ref[r0:r0 + step, :] = w_ref[r0:r0 + step, :].astype(bf16)


def _mm_wres_kernel(ap_ref, as_ref, w_ref, *rest, epilogue, n_extra, n_m, w_is_nk):
    extras_p, extras_s = rest[:n_extra], rest[n_extra:2 * n_extra]
    op_ref, os_ref, wbf_ref = rest[2 * n_extra:]
    mi = pl.program_id(1)
    mm = _dot_nt if w_is_nk else _dot

    @pl.when(mi == 0)
    def _():
        _cast_weight_tile(w_ref, wbf_ref)

    w = wbf_ref[...]
    op_ref[...] = epilogue(mm(ap_ref[...], w), *[e[...] for e in extras_p]).astype(op_ref.dtype)

    @pl.when(mi == n_m - 1)
    def _():
        os_ref[...] = epilogue(mm(as_ref[...], w), *[e[...] for e in extras_s]).astype(os_ref.dtype)


def _matmul_wres(a_p, a_s, w, n_cols, out_dtypes, name, *, w_is_nk=False, col0=0, epilogue=lambda acc: acc,
                 extras_p=(), extras_s=()):
    mp, k = a_p.shape
    ms = a_s.shape[0]
    tm = _tile(mp, (1024, 512, 256, 128))
    tn = _tile(n_cols, (512, 256, 128))
    n_m, n_n = mp // tm, n_cols // tn
    if w_is_nk:
        assert col0 % SUBLANES == 0
        w_spec = pl.BlockSpec((pl.Element(tn), pl.Element(k)),
                              lambda j, i: (pl.multiple_of(col0 + j * tn, SUBLANES), 0))
        wbf_shape = (tn, k)
    else:
        assert col0 % tn == 0
        w_spec = pl.BlockSpec((k, tn), lambda j, i: (0, col0 // tn + j))
        wbf_shape = (k, tn)
    in_specs = [pl.BlockSpec((tm, k), lambda j, i: (i, 0)), pl.BlockSpec((ms, k), lambda j, i: (0, 0)), w_spec]
    in_specs += [pl.BlockSpec((tm, tn), lambda j, i: (i, j)) for _ in extras_p]
    in_specs += [pl.BlockSpec((ms, tn), lambda j, i: (0, j)) for _ in extras_s]
    return pl.pallas_call(
        functools.partial(_mm_wres_kernel, epilogue=epilogue, n_extra=len(extras_p), n_m=n_m, w_is_nk=w_is_nk),
        grid=(n_n, n_m),
        in_specs=in_specs,
        out_specs=[pl.BlockSpec((tm, tn), lambda j, i: (i, j)), pl.BlockSpec((ms, tn), lambda j, i: (0, j))],
        out_shape=[jax.ShapeDtypeStruct((mp, n_cols), out_dtypes[0]),
                   jax.ShapeDtypeStruct((ms, n_cols), out_dtypes[1])],
        scratch_shapes=[pltpu.VMEM(wbf_shape, bf16)],
        compiler_params=_params("parallel", "arbitrary"),
        name=name,
    )(a_p, a_s, w, *extras_p, *extras_s)


def _merge_kernel(ap_ref, as_ref, bp_ref, bs_ref, wa_ref, wb_ref, gap_ref, gbp_ref, gas_ref, gbs_ref,
                  op_ref, os_ref, wabf_ref, wbbf_ref, *, n_m):
    mi = pl.program_id(1)

    @pl.when(mi == 0)
    def _():
        _cast_weight_tile(wa_ref, wabf_ref)
        _cast_weight_tile(wb_ref, wbbf_ref)

    def mix(a_ref, b_ref, ga_ref, gb_ref, o_ref):
        pa = _dot(a_ref[...], wabf_ref[...])
        pb = _dot(b_ref[...], wbbf_ref[...])
        o_ref[...] = (ga_ref[...].astype(f32) * pa + gb_ref[...].astype(f32) * pb).astype(o_ref.dtype)

    mix(ap_ref, bp_ref, gap_ref, gbp_ref, op_ref)

    @pl.when(mi == n_m - 1)
    def _():
        mix(as_ref, bs_ref, gas_ref, gbs_ref, os_ref)


def _merge(nsa_p, nsa_s, sg_p, sg_s, w_pa, w_pb, gates_p, gates_s, name):
    mp, ka = nsa_p.shape
    ms = nsa_s.shape[0]
    kb = sg_p.shape[1]
    d = w_pa.shape[1]
    tm = _tile(mp, (1024, 512, 256, 128))
    tn = _tile(d, (512, 256, 128))
    n_m, n_n = mp // tm, d // tn
    row_p = lambda w: pl.BlockSpec((tm, w), lambda j, i: (i, 0))
    row_s = lambda w: pl.BlockSpec((ms, w), lambda j, i: (0, 0))
    return pl.pallas_call(
        functools.partial(_merge_kernel, n_m=n_m),
        grid=(n_n, n_m),
        in_specs=[row_p(ka), row_s(ka), row_p(kb), row_s(kb),
                  pl.BlockSpec((ka, tn), lambda j, i: (0, j)), pl.BlockSpec((kb, tn), lambda j, i: (0, j)),
                  pl.BlockSpec((tm, tn), lambda j, i: (i, j)), pl.BlockSpec((tm, tn), lambda j, i: (i, j + n_n)),
                  pl.BlockSpec((ms, tn), lambda j, i: (0, j)), pl.BlockSpec((ms, tn), lambda j, i: (0, j + n_n))],
        out_specs=[pl.BlockSpec((tm, tn), lambda j, i: (i, j)), pl.BlockSpec((ms, tn), lambda j, i: (0, j))],
        out_shape=[jax.ShapeDtypeStruct((mp, d), bf16), jax.ShapeDtypeStruct((ms, d), bf16)],
        scratch_shapes=[pltpu.VMEM((ka, tn), bf16), pltpu.VMEM((kb, tn), bf16)],
        compiler_params=_params("parallel", "arbitrary"),
        name=name,
    )(nsa_p, nsa_s, sg_p, sg_s, w_pa, w_pb, gates_p, gates_p, gates_s, gates_s)


def _mm_resid_kernel(a_ref, w_ref, r_ref, o_ref):
    o_ref[...] = r_ref[...] + _dot(a_ref[...], w_ref[...])


def _matmul_residual(a, w, resid, name):
    m, k = a.shape
    n = w.shape[1]
    tm = _tile(m, (512, 256, 128, 64, 32, 16))
    tn = _tile(n, (256, 128))
    return pl.pallas_call(
        _mm_resid_kernel,
        grid=(m // tm, n // tn),
        in_specs=[pl.BlockSpec((tm, k), lambda i, j: (i, 0)), pl.BlockSpec((k, tn), lambda i, j: (0, j)),
                  pl.BlockSpec((tm, tn), lambda i, j: (i, j))],
        out_specs=pl.BlockSpec((tm, tn), lambda i, j: (i, j)),
        out_shape=jax.ShapeDtypeStruct((m, n), f32),
        compiler_params=_params("parallel", "arbitrary"),
        name=name,
    )(a, w, resid)


def _prep_kernel(kc, vc, ks, vs, kw, vw, c_ref, sa_ref, sb_ref, okc, ovc, oks, ovs, okw, ovw,
                 bks, bvs, bkw, bvw, *, tr):
    c, sa, sb = c_ref[...], sa_ref[...], sb_ref[...]
    values = (kc[...], vc[...], _rope(ks[...], c, sa, sb), vs[...], _rope(kw[...], c, sa, sb), vw[...])
    for val, o_ref in zip(values, (okc, ovc, oks, ovs, okw, ovw)):
        for g in range(N_KV_HEADS):
            o_ref[pl.ds(g, tr, stride=N_KV_HEADS), :] = val[:, g * HEAD_DIM:(g + 1) * HEAD_DIM]
    for val, o_ref in zip(values[2:], (bks, bvs, bkw, bvw)):
        o_ref[...] = val.astype(bf16)


def _prep_kv(z1, tables, rows_per_seq, name):
    m = z1.shape[0]
    tr = _tile(rows_per_seq, (256, 128, 64, 32, 16, 8))
    n_tab = tables[0].shape[0] // tr
    col = lambda cb: pl.BlockSpec((tr, KV_W), lambda i, cb=cb: (i, cb))
    tab = pl.BlockSpec((tr, KV_W), lambda i: (i % n_tab, 0))
    out = pl.BlockSpec((tr * N_KV_HEADS, HEAD_DIM), lambda i: (i, 0))
    out_b = pl.BlockSpec((tr, KV_W), lambda i: (i, 0))
    return pl.pallas_call(
        functools.partial(_prep_kernel, tr=tr),
        grid=(m // tr,),
        in_specs=[col(COL_KC), col(COL_VC), col(COL_KS), col(COL_VS), col(COL_KW), col(COL_VW), tab, tab, tab],
        out_specs=[out] * 6 + [out_b] * 4,
        out_shape=[jax.ShapeDtypeStruct((m * N_KV_HEADS, HEAD_DIM), f32)] * 6
        + [jax.ShapeDtypeStruct((m, KV_W), bf16)] * 4,
        compiler_params=_params("parallel"),
        name=name,
    )(z1, z1, z1, z1, z1, z1, *tables)


def _compress_prompt_kernel(k_ref, v_ref, pwk_ref, bk_ref, pwv_ref, bv_ref, ak_ref, av_ref, *, nsub):
    for x_ref, pw_ref, b_ref, o_ref in ((k_ref, pwk_ref, bk_ref, ak_ref), (v_ref, pwv_ref, bv_ref, av_ref)):
        for g in range(N_KV_HEADS):
            first = None
            second = None
            for j in range(STRIDE):
                rows = x_ref[pl.ds(j * N_KV_HEADS + g, nsub, stride=STRIDE * N_KV_HEADS), :]
                fa = rows * pw_ref[j:j + 1, :]
                sa = rows * pw_ref[STRIDE + j:STRIDE + j + 1, :]
                first = fa if first is None else first + fa
                second = sa if second is None else second + sa
            o_ref[0, :, g * HEAD_DIM:(g + 1) * HEAD_DIM] = _silu(pltpu.roll(first, 1, 0) + second + b_ref[...])


def _compress_prompt(kc, vc, pwk, bk, pwv, bv, batch, seq, name):
    nsub = seq // STRIDE
    xspec = pl.BlockSpec((seq * N_KV_HEADS, HEAD_DIM), lambda b: (b, 0))
    pwspec = pl.BlockSpec((L_CMP, HEAD_DIM), lambda b: (0, 0))
    bspec = pl.BlockSpec((1, HEAD_DIM), lambda b: (0, 0))
    ospec = pl.BlockSpec((1, nsub, KV_W), lambda b: (b, 0, 0))
    return pl.pallas_call(
        functools.partial(_compress_prompt_kernel, nsub=nsub),
        grid=(batch,),
        in_specs=[xspec, xspec, pwspec, bspec, pwspec, bspec],
        out_specs=[ospec, ospec],
        out_shape=[jax.ShapeDtypeStruct((batch, nsub, KV_W), f32)] * 2,
        compiler_params=_params("parallel"),
        name=name,
    )(kc, vc, pwk, bk.reshape(1, HEAD_DIM), pwv, bv.reshape(1, HEAD_DIM))


PAGES_PER_CHUNK = 32


def _compress_paged_kernel(pt_ref, wfk_ref, wsk_ref, bk_ref, wfv_ref, wsv_ref, bv_ref, poolk_ref, poolv_ref,
                           ak_ref, av_ref, bufk, bufv, sems, *, n_pages, page_rows, pps):
    b = pl.program_id(0)
    n_chunks = n_pages // pps
    tiles_per_sub = STRIDE * N_KV_HEADS // SUBLANES
    nsub = page_rows // (STRIDE * N_KV_HEADS)

    def copies(seq, chunk, slot):
        out = []
        for p in range(pps):
            phys = pt_ref[seq * n_pages + chunk * pps + p]
            out.append(pltpu.make_async_copy(poolk_ref.at[phys], bufk.at[slot, p], sems.at[0, slot]))
            out.append(pltpu.make_async_copy(poolv_ref.at[phys], bufv.at[slot, p], sems.at[1, slot]))
        return out

    @pl.when(b == 0)
    def _():
        for cp in copies(b, 0, 0):
            cp.start()

    carry = (jnp.zeros((SUBLANES, HEAD_DIM), f32), jnp.zeros((SUBLANES, HEAD_DIM), f32))
    for chunk in range(n_chunks):
        slot = chunk % 2
        if chunk + 1 < n_chunks:
            for cp in copies(b, chunk + 1, 1 - slot):
                cp.start()
        else:
            @pl.when(b + 1 < pl.num_programs(0))
            def _():
                for cp in copies(b + 1, 0, 0):
                    cp.start()
        for cp in copies(b, chunk, slot):
            cp.wait()

        def page_body(p, prev, chunk=chunk, slot=slot):
            out_row0 = pl.multiple_of((chunk * pps + p) * nsub * SUBLANES, nsub * SUBLANES)
            new_prev = []
            for which, (buf, wf_ref, ws_ref, b_ref, o_ref) in enumerate(
                    ((bufk, wfk_ref, wsk_ref, bk_ref, ak_ref), (bufv, wfv_ref, wsv_ref, bv_ref, av_ref))):
                last = prev[which]
                for n in range(nsub):
                    first = None
                    second = None
                    for t in range(tiles_per_sub):
                        x = buf[slot, p, pl.ds((n * tiles_per_sub + t) * SUBLANES, SUBLANES), :]
                        fa = x * wf_ref[t]
                        sa = x * ws_ref[t]
                        first = fa if first is None else first + fa
                        second = sa if second is None else second + sa
                    first = first + pltpu.roll(first, SUBLANES // 2, 0)
                    second = second + pltpu.roll(second, SUBLANES // 2, 0)
                    o_ref[pl.ds(out_row0 + n * SUBLANES, SUBLANES), :] = _silu(last + second + b_ref[...])
                    last = first
                new_prev.append(last)
            return tuple(new_prev)

        carry = lax.fori_loop(0, pps, page_body, carry)


def _compress_paged(pool_k, pool_v, page_table, pwk, bk, pwv, bv, name):
    n_batch, n_pages = page_table.shape
    n_phys, page = pool_k.shape[:2]
    pps = min(PAGES_PER_CHUNK, n_pages // 2)
    assert N_KV_HEADS * 2 == SUBLANES and n_pages % (2 * pps) == 0 and page % STRIDE == 0
    page_rows = page * N_KV_HEADS
    out_rows = n_pages * (page // STRIDE) * SUBLANES

    def tile_weights(pw):
        return jnp.repeat(pw.reshape(STRIDE // 2, 2, HEAD_DIM), N_KV_HEADS, axis=1)

    full3 = pl.BlockSpec((STRIDE // 2, SUBLANES, HEAD_DIM), lambda b, pt: (0, 0, 0))
    brow = pl.BlockSpec((SUBLANES, HEAD_DIM), lambda b, pt: (0, 0))
    o_spec = pl.BlockSpec((out_rows, HEAD_DIM), lambda b, pt: (b, 0))
    out_shape = jax.ShapeDtypeStruct((n_batch * out_rows, HEAD_DIM), f32)
    buf = pltpu.VMEM((2, pps, page_rows, HEAD_DIM), f32)
    return pl.pallas_call(
        functools.partial(_compress_paged_kernel, n_pages=n_pages, page_rows=page_rows, pps=pps),
        grid_spec=pltpu.PrefetchScalarGridSpec(
            num_scalar_prefetch=1,
            grid=(n_batch,),
            in_specs=[full3, full3, brow, full3, full3, brow,
                      pl.BlockSpec(memory_space=pl.ANY), pl.BlockSpec(memory_space=pl.ANY)],
            out_specs=[o_spec, o_spec],
            scratch_shapes=[buf, buf, pltpu.SemaphoreType.DMA((2, 2))],
        ),
        out_shape=[out_shape, out_shape],
        compiler_params=_params("arbitrary"),
        name=name,
    )(page_table.reshape(-1),
      tile_weights(pwk[:STRIDE]), tile_weights(pwk[STRIDE:]), jnp.tile(bk.reshape(1, HEAD_DIM), (SUBLANES, 1)),
      tile_weights(pwv[:STRIDE]), tile_weights(pwv[STRIDE:]), jnp.tile(bv.reshape(1, HEAD_DIM), (SUBLANES, 1)),
      pool_k.reshape(n_phys, page_rows, HEAD_DIM), pool_v.reshape(n_phys, page_rows, HEAD_DIM))


def _block_overlap(blk, rown):
    c_start = (rown - 1) * STRIDE
    return (rown >= 1) & (c_start < blk * L_SEL + L_SEL) & (c_start + L_CMP > blk * L_SEL)


def _stable_rank(score, score_ref, blk, nblk):
    score_ref[...] = score

    def count_better(i, cnt):
        other = score_ref[pl.ds(i, 1), :]
        better = (other > score) | ((other == score) & (i < blk))
        return cnt + better.astype(f32)

    return lax.fori_loop(0, nblk, count_better, jnp.zeros(score.shape, f32))


def _cmp_select_prompt_kernel(q_ref, ak_ref, av_ref, wk_ref, wv_ref, ocmp_ref, sel_ref, score_ref, *,
                              tq, nrows, nblk):
    pos0 = pl.program_id(2) * tq
    pos_col = pos0 + lax.broadcasted_iota(jnp.int32, (tq, 1), 0)
    pos_row = pos0 + lax.broadcasted_iota(jnp.int32, (1, tq), 1)

    ck = _split2(_dot_3pass(ak_ref[0], wk_ref[...]))
    cv = _dot(av_ref[0].astype(bf16), wv_ref[...].astype(bf16)).astype(bf16)

    rown = lax.broadcasted_iota(jnp.int32, (1, nrows), 1)
    visible = (rown >= 1) & (rown * STRIDE + (L_CMP - STRIDE - 1) <= pos_col)

    pg = jnp.zeros((tq, nrows), f32)
    for r in range(GROUP):
        sl = slice(r * HEAD_DIM, (r + 1) * HEAD_DIM)
        s = _dot_3pass(q_ref[:, sl], ck, mm=_dot_nt) * SCALE
        s = jnp.where(visible, s, NEG_INF)
        e = jnp.where(visible, jnp.exp(s - jnp.max(s, axis=-1, keepdims=True)), 0.0)
        den = jnp.sum(e, axis=-1, keepdims=True)
        p = jnp.where(den > 0.0, e / jnp.where(den > 0.0, den, 1.0), 0.0)
        ocmp_ref[:, sl] = _dot(p.astype(bf16), cv)
        pg = pg + p

    blk = lax.broadcasted_iota(jnp.int32, (nblk, 1), 0)
    imp_t = _dot_mask(_block_overlap(blk, rown).astype(bf16), pg, mm=_dot_nt)
    qb = pos_row // L_SEL
    causal = blk <= qb
    forced = (blk == 0) | (blk == qb) | (blk == qb - 1)
    score = jnp.where(causal, imp_t + FORCE_BONUS * forced.astype(f32), -jnp.inf)
    rank = _stable_rank(score, score_ref, blk, nblk)
    chosen = jnp.where(causal & (rank < float(N_SEL)), 1.0, 0.0).astype(bf16)
    eye = (lax.broadcasted_iota(jnp.int32, (tq, tq), 0)
           == lax.broadcasted_iota(jnp.int32, (tq, tq), 1)).astype(bf16)
    sel_ref[0, 0] = _dot_nt(eye, chosen)


def _cmp_select_prompt(q, ak, av, wk, wv, *, batch, seq, nblk, name):
    nrows = ak.shape[1]
    tq = _tile(seq, (1024, 512, 256, 128))
    n_q = seq // tq
    q_spec = pl.BlockSpec((tq, GROUP_W), lambda b, g, i: (b * n_q + i, g))
    a_spec = pl.BlockSpec((1, nrows, HEAD_DIM), lambda b, g, i: (b, 0, g))
    w_spec = pl.BlockSpec((HEAD_DIM, HEAD_DIM), lambda b, g, i: (0, 0))
    return pl.pallas_call(
        functools.partial(_cmp_select_prompt_kernel, tq=tq, nrows=nrows, nblk=nblk),
        grid=(batch, N_KV_HEADS, n_q),
        in_specs=[q_spec, a_spec, a_spec, w_spec, w_spec],
        out_specs=[pl.BlockSpec((tq, GROUP_W), lambda b, g, i: (b * n_q + i, g)),
                   pl.BlockSpec((1, 1, tq, nblk), lambda b, g, i: (b, g, i, 0))],
        out_shape=[jax.ShapeDtypeStruct((batch * seq, Q_W), f32),
                   jax.ShapeDtypeStruct((batch, N_KV_HEADS, seq, nblk), f32)],
        scratch_shapes=[pltpu.VMEM((nblk, tq), f32)],
        compiler_params=_params("parallel", "parallel", "arbitrary"),
        name=name,
    )(q, ak, av, wk, wv)


def _cmp_sample_kernel(q_ref, ak_ref, av_ref, wk_ref, wv_ref, ocmp_ref, imp_ref, *, nrows, nblk, pos):
    rown = lax.broadcasted_iota(jnp.int32, (nrows, 1), 0)
    visible = (rown >= 1) & (rown * STRIDE + (L_CMP - STRIDE - 1) <= pos)
    overlap = _block_overlap(lax.broadcasted_iota(jnp.int32, (nblk, 1), 0),
                             lax.broadcasted_iota(jnp.int32, (1, nrows), 1)).astype(bf16)
    lane = lax.broadcasted_iota(jnp.int32, (1, SUBLANES), 1)
    is_head = lane < GROUP
    imp_all = jnp.zeros((nblk, SUBLANES), f32)
    for g in range(N_KV_HEADS):
        a_k = ak_ref[pl.ds(g, nrows, stride=SUBLANES), :]
        a_v = av_ref[pl.ds(g, nrows, stride=SUBLANES), :]
        ck = _dot_3pass(a_k, wk_ref[...])
        cv = _dot(a_v.astype(bf16), wv_ref[...].astype(bf16))
        s = _dot_3pass(ck, q_ref[0, g], mm=_dot_nt) * SCALE
        s = jnp.where(visible, s, NEG_INF)
        e = jnp.where(visible, jnp.exp(s - jnp.max(s, axis=0, keepdims=True)), 0.0)
        den = jnp.sum(e, axis=0, keepdims=True)
        p = jnp.where(is_head & (den > 0.0), e / jnp.where(den > 0.0, den, 1.0), 0.0)
        for r in range(GROUP):
            ocmp_ref[0, g, r:r + 1, :] = jnp.sum(p[:, r:r + 1] * cv, axis=0, keepdims=True)
        ocmp_ref[0, g, GROUP:, :] = jnp.zeros((SUBLANES - GROUP, HEAD_DIM), f32)
        imp = jnp.sum(_dot_mask(overlap, p), axis=1, keepdims=True)
        imp_all = jnp.where(lane == g, imp, imp_all)
    imp_ref[0] = imp_all


def _cmp_sample(q, ak, av, wk, wv, *, n_batch, nrows, nblk, pos, name):
    q_spec = pl.BlockSpec((1, N_KV_HEADS, SUBLANES, HEAD_DIM), lambda b: (b, 0, 0, 0))
    a_spec = pl.BlockSpec((nrows * SUBLANES, HEAD_DIM), lambda b: (b, 0))
    w_spec = pl.BlockSpec((HEAD_DIM, HEAD_DIM), lambda b: (0, 0))
    return pl.pallas_call(
        functools.partial(_cmp_sample_kernel, nrows=nrows, nblk=nblk, pos=pos),
        grid=(n_batch,),
        in_specs=[q_spec, a_spec, a_spec, w_spec, w_spec],
        out_specs=[q_spec, pl.BlockSpec((1, nblk, SUBLANES), lambda b: (b, 0, 0))],
        out_shape=[jax.ShapeDtypeStruct((n_batch, N_KV_HEADS, SUBLANES, HEAD_DIM), f32),
                   jax.ShapeDtypeStruct((n_batch, nblk, SUBLANES), f32)],
        compiler_params=_params("parallel"),
        name=name,
    )(q, ak, av, wk, wv)


def _rank_sample_kernel(imp_ref, idx_ref, score_ref, *, nblk, qb):
    blk = lax.broadcasted_iota(jnp.int32, (nblk, 1), 0)
    causal = blk <= qb
    forced = (blk == 0) | (blk == qb) | (blk == qb - 1)
    score = jnp.where(causal, imp_ref[...] + FORCE_BONUS * forced.astype(f32), -jnp.inf)
    rank = _stable_rank(score, score_ref, blk, nblk)
    for slot in range(N_SEL):
        hit = causal & (rank == float(slot))
        idx_ref[slot:slot + 1, :] = jnp.sum(jnp.where(hit, blk, 0), axis=0, keepdims=True)
        idx_ref[N_SEL + slot:N_SEL + slot + 1, :] = jnp.max(hit.astype(jnp.int32), axis=0, keepdims=True)


def _rank_sample(imp_t, *, nblk, qb, name):
    lanes = imp_t.shape[1]
    return pl.pallas_call(
        functools.partial(_rank_sample_kernel, nblk=nblk, qb=qb),
        grid=(1,),
        in_specs=[pl.BlockSpec((nblk, lanes), lambda i: (0, 0))],
        out_specs=pl.BlockSpec((2 * N_SEL, lanes), lambda i: (0, 0)),
        out_shape=jax.ShapeDtypeStruct((2 * N_SEL, lanes), jnp.int32),
        scratch_shapes=[pltpu.VMEM((nblk, lanes), f32)],
        compiler_params=_params("arbitrary"),
        name=name,
    )(imp_t)


def _softmax_pv(s, v, exp=jnp.exp):
    e = exp(s - jnp.max(s, axis=-1, keepdims=True))
    den = jnp.sum(e, axis=-1, keepdims=True)
    return _dot(e.astype(bf16), v) / den


LOG2_E = 1.4426950408889634


def _masked_attention(q, k, v, mask, tq):
    heads = q.shape[0] // tq
    keys = k.shape[0]
    s = _dot_nt(q, k)
    s = jnp.where(mask[None], s.reshape(heads, tq, keys), NEG_INF).reshape(heads * tq, keys)
    return _softmax_pv(s, v, exp=jnp.exp2)


HEADS_PER_CHAIN = 2


def _nsa_prompt_kernel(q_ref, c_ref, sa_ref, sb_ref, ks_ref, vs_ref, kw_ref, vw_ref, sel_ref, ocmp_ref,
                       gate_ref, o_ref, *, tq, seq, nblk, span, n_q):
    qi = pl.program_id(2)
    t0 = qi * tq
    qr = (_rope(q_ref[...], c_ref[...], sa_ref[...], sb_ref[...]) * (SCALE * LOG2_E)).astype(bf16)
    n_chain = GROUP // HEADS_PER_CHAIN
    chains = [jnp.concatenate([qr[:, (c * HEADS_PER_CHAIN + h) * HEAD_DIM:(c * HEADS_PER_CHAIN + h + 1) * HEAD_DIM]
                               for h in range(HEADS_PER_CHAIN)], axis=0) for c in range(n_chain)]
    tpos = t0 + lax.broadcasted_iota(jnp.int32, (tq, 1), 0)

    start = pl.multiple_of(jnp.maximum(t0 - WINDOW, 0), tq)
    wpos = start + lax.broadcasted_iota(jnp.int32, (1, span), 1)
    wmask = (wpos <= tpos) & (wpos > tpos - WINDOW)
    kw = kw_ref[pl.ds(start, span), :].astype(bf16)
    vw = vw_ref[pl.ds(start, span), :].astype(bf16)
    o_win = [_masked_attention(qc, kw, vw, wmask, tq) for qc in chains]

    gates = _sigmoid(gate_ref[...])
    sel = sel_ref[0, 0].astype(bf16)

    tiles_per_class = 2
    for cls in range(-(-n_q // tiles_per_class)):
        ext = min(seq, (cls + 1) * tiles_per_class * tq)

        @pl.when(qi // tiles_per_class == cls)
        def _(ext=ext):
            kpos = lax.broadcasted_iota(jnp.int32, (1, ext), 1)
            expand = (kpos // L_SEL == lax.broadcasted_iota(jnp.int32, (nblk, 1), 0)).astype(bf16)
            mask = (_dot(sel, expand) > 0.5) & (kpos <= tpos)
            k = ks_ref[0:ext, :].astype(bf16)
            v = vs_ref[0:ext, :].astype(bf16)
            for c, qc in enumerate(chains):
                o_slc = _masked_attention(qc, k, v, mask, tq)
                for h in range(HEADS_PER_CHAIN):
                    r = c * HEADS_PER_CHAIN + h
                    sl = slice(r * HEAD_DIM, (r + 1) * HEAD_DIM)
                    rows = slice(h * tq, (h + 1) * tq)
                    o = (gates[:, 3 * r:3 * r + 1] * ocmp_ref[:, sl] + gates[:, 3 * r + 1:3 * r + 2] * o_slc[rows]
                         + gates[:, 3 * r + 2:3 * r + 3] * o_win[c][rows])
                    o_ref[:, sl] = o.astype(o_ref.dtype)


def _nsa_prompt(z1, tables, ks, vs, kw, vw, sel, ocmp, gates, *, batch, seq, nblk, name):
    tq = _tile(seq, (256, 128))
    n_q = seq // tq
    span = min(WINDOW + tq, seq)
    row_blk = lambda w: pl.BlockSpec((tq, w), lambda b, g, i: (b * n_q + i, g))
    tab = pl.BlockSpec((tq, GROUP_W), lambda b, g, i: (i, 0))
    kv = pl.BlockSpec((seq, HEAD_DIM), lambda b, g, i: (b, g))
    return pl.pallas_call(
        functools.partial(_nsa_prompt_kernel, tq=tq, seq=seq, nblk=nblk, span=span, n_q=n_q),
        grid=(batch, N_KV_HEADS, n_q),
        in_specs=[row_blk(GROUP_W), tab, tab, tab, kv, kv, kv, kv,
                  pl.BlockSpec((1, 1, tq, nblk), lambda b, g, i: (b, g, i, 0)),
                  row_blk(GROUP_W),
                  pl.BlockSpec((None, tq, 3 * GROUP), lambda b, g, i: (g, b * n_q + i, 0))],
        out_specs=row_blk(GROUP_W),
        out_shape=jax.ShapeDtypeStruct((batch * seq, Q_W), bf16),
        compiler_params=_params("parallel", "parallel", "arbitrary"),
        name=name,
    )(z1, *tables, ks, vs, kw, vw, sel, ocmp, gates)


def _nsa_sample_kernel(idx_ref, pt_ref, q_ref, c_ref, sa_ref, sb_ref, ksn_ref, vsn_ref, kwin_ref, vwin_ref,
                       ocmp_ref, gate_ref, poolk_ref, poolv_ref, o_ref, kbuf, vbuf, sems, *,
                       n_pages, page, pos, window):
    b = pl.program_id(0)
    per_page = page // L_SEL
    past_blocks = n_pages * per_page
    n_keys = N_SEL * L_SEL

    def gather_copies(seq):
        half = seq % 2
        out = []
        for g in range(N_KV_HEADS):
            for slot in range(N_SEL):
                blk = idx_ref[(seq * N_KV_HEADS + g) * 2 * N_SEL + slot]
                jp = jnp.minimum(blk, past_blocks - 1)
                phys = pt_ref[seq * n_pages + jp // per_page]
                off = pl.multiple_of((jp % per_page) * L_SEL, L_SEL)
                dst = pl.ds(slot * L_SEL, L_SEL)
                out.append(pltpu.make_async_copy(poolk_ref.at[phys, pl.ds(off, L_SEL), g], kbuf.at[half, g, dst],
                                                 sems.at[0, half]))
                out.append(pltpu.make_async_copy(poolv_ref.at[phys, pl.ds(off, L_SEL), g], vbuf.at[half, g, dst],
                                                 sems.at[1, half]))
        return out

    @pl.when(b == 0)
    def _():
        for cp in gather_copies(b):
            cp.start()

    @pl.when(b + 1 < pl.num_programs(0))
    def _():
        for cp in gather_copies(b + 1):
            cp.start()

    for cp in gather_copies(b):
        cp.wait()
    half = b % 2

    lane = lax.broadcasted_iota(jnp.int32, (1, n_keys), 1)
    lane_slot = lane // L_SEL
    for g in range(N_KV_HEADS):
        qr = _rope(q_ref[0, g], c_ref[...], sa_ref[...], sb_ref[...]).astype(bf16)
        blk_vec = jnp.zeros((1, n_keys), jnp.int32)
        ok_vec = jnp.zeros((1, n_keys), jnp.int32)
        base = (b * N_KV_HEADS + g) * 2 * N_SEL
        for slot in range(N_SEL):
            blk_vec = jnp.where(lane_slot == slot, idx_ref[base + slot], blk_vec)
            ok_vec = jnp.where(lane_slot == slot, idx_ref[base + N_SEL + slot], ok_vec)
        in_past = blk_vec < past_blocks
        kpos = blk_vec * L_SEL + lane % L_SEL
        mask = (ok_vec > 0) & (kpos <= pos)
        k_new = ksn_ref[0, g:g + 1, :].astype(bf16).astype(f32)
        v_new = vsn_ref[0, g:g + 1, :].astype(bf16).astype(f32)
        s_new = jnp.sum(qr.astype(f32) * k_new, axis=-1, keepdims=True)
        s = jnp.where(in_past, _dot_nt(qr, kbuf[half, g].astype(bf16)), s_new) * SCALE
        s = jnp.where(mask, s, NEG_INF)
        e = jnp.exp(s - jnp.max(s, axis=-1, keepdims=True))
        den = jnp.sum(e, axis=-1, keepdims=True)
        e_new = jnp.sum(jnp.where(in_past, 0.0, e), axis=-1, keepdims=True)
        e_past = jnp.where(in_past, e, 0.0).astype(bf16)
        o_slc = (_dot(e_past, vbuf[half, g].astype(bf16)) + e_new.astype(bf16).astype(f32) * v_new) / den
        win_rows = pl.ds(g, window, stride=N_KV_HEADS)
        o_win = _softmax_pv(_dot_nt(qr, kwin_ref[win_rows, :].astype(bf16)) * SCALE,
                            vwin_ref[win_rows, :].astype(bf16))
        gt = _sigmoid(gate_ref[0, g])
        o_ref[0, g] = gt[:, 0:1] * ocmp_ref[0, g] + gt[:, 1:2] * o_slc + gt[:, 2:3] * o_win


def _nsa_sample(idx, page_table, q, tables, ks_new, vs_new, kwin, vwin, ocmp, gates, pool_k, pool_v, *,
                pos, name):
    n_batch, n_pages = page_table.shape
    page = pool_k.shape[1]
    window = kwin.shape[0] // (n_batch * N_KV_HEADS)
    tab = pl.BlockSpec((SUBLANES, HEAD_DIM), lambda b, *_: (0, 0))
    heads = pl.BlockSpec((1, N_KV_HEADS, SUBLANES, HEAD_DIM), lambda b, *_: (b, 0, 0, 0))
    new_row = pl.BlockSpec((1, N_KV_HEADS, HEAD_DIM), lambda b, *_: (b, 0, 0))
    win = pl.BlockSpec((window * N_KV_HEADS, HEAD_DIM), lambda b, *_: (b, 0))
    return pl.pallas_call(
        functools.partial(_nsa_sample_kernel, n_pages=n_pages, page=page, pos=pos, window=window),
        grid_spec=pltpu.PrefetchScalarGridSpec(
            num_scalar_prefetch=2,
            grid=(n_batch,),
            in_specs=[heads, tab, tab, tab, new_row, new_row, win, win, heads, heads,
                      pl.BlockSpec(memory_space=pl.ANY), pl.BlockSpec(memory_space=pl.ANY)],
            out_specs=heads,
            scratch_shapes=[pltpu.VMEM((2, N_KV_HEADS, N_SEL * L_SEL, HEAD_DIM), f32),
                            pltpu.VMEM((2, N_KV_HEADS, N_SEL * L_SEL, HEAD_DIM), f32),
                            pltpu.SemaphoreType.DMA((2, 2))],
        ),
        out_shape=jax.ShapeDtypeStruct((n_batch, N_KV_HEADS, SUBLANES, HEAD_DIM), f32),
        compiler_params=_params("arbitrary"),
        name=name,
    )(idx, page_table.reshape(-1), q, *tables, ks_new, vs_new, kwin, vwin, ocmp, gates, pool_k, pool_v)


def _window_update_kernel(kin_ref, vin_ref, knew_ref, vnew_ref, kout_ref, vout_ref):
    rows = kin_ref.shape[1]
    keep = rows - N_KV_HEADS
    for src, new, dst in ((kin_ref, knew_ref, kout_ref), (vin_ref, vnew_ref, vout_ref)):
        dst[0, 0:keep, :] = src[0, N_KV_HEADS:rows, :]
        dst[0, keep:rows, :] = new[0]


def _window_update(win_k, win_v, k_new, v_new, name):
    n_batch, rows, d = win_k.shape
    win = pl.BlockSpec((1, rows, d), lambda b: (b, 0, 0))
    new = pl.BlockSpec((1, N_KV_HEADS, d), lambda b: (b, 0, 0))
    shape = jax.ShapeDtypeStruct(win_k.shape, win_k.dtype)
    return pl.pallas_call(
        _window_update_kernel,
        grid=(n_batch,),
        in_specs=[win, win, new, new],
        out_specs=[win, win],
        out_shape=[shape, shape],
        compiler_params=_params("parallel"),
        name=name,
    )(win_k, win_v, k_new, v_new)


def _sgu_prompt_kernel(u_ref, v_ref, g_ref, w_ref, bt_ref, o_ref):
    v = v_ref[...].astype(f32)
    vn = (v * lax.rsqrt(jnp.mean(v * v, axis=-1, keepdims=True) + EPS) * g_ref[...]).astype(bf16)
    tril = (lax.broadcasted_iota(jnp.int32, (SG_CHUNK, SG_CHUNK), 0)
            >= lax.broadcasted_iota(jnp.int32, (SG_CHUNK, SG_CHUNK), 1))
    bt = bt_ref[...]
    group_dim = v.shape[-1] // SG_GROUPS
    for gi in range(SG_GROUPS):
        sl = slice(gi * group_dim, (gi + 1) * group_dim)
        w = jnp.where(tril, w_ref[gi], 0.0).astype(bf16)
        mixed = _dot(w, vn[:, sl]) + bt[:, gi:gi + 1]
        o_ref[:, sl] = (u_ref[:, sl].astype(f32) * mixed).astype(o_ref.dtype)


def _sgu_prompt(uv, norm_g, w_s, b_s, name):
    m, two_w = uv.shape
    width = two_w // 2
    return pl.pallas_call(
        _sgu_prompt_kernel,
        grid=(m // SG_CHUNK,),
        in_specs=[pl.BlockSpec((SG_CHUNK, width), lambda i: (i, 0)),
                  pl.BlockSpec((SG_CHUNK, width), lambda i: (i, 1)),
                  pl.BlockSpec((1, width), lambda i: (0, 0)),
                  pl.BlockSpec((SG_GROUPS, SG_CHUNK, SG_CHUNK), lambda i: (0, 0, 0)),
                  pl.BlockSpec((SG_CHUNK, SG_GROUPS), lambda i: (0, 0))],
        out_specs=pl.BlockSpec((SG_CHUNK, width), lambda i: (i, 0)),
        out_shape=jax.ShapeDtypeStruct((m, width), bf16),
        compiler_params=_params("parallel"),
        name=name,
    )(uv, uv, norm_g.reshape(1, width), w_s, b_s.T)


def _sgu_sample_kernel(u_ref, v_ref, g_ref, w0_ref, b0_ref, o_ref, vn_ref):
    v = v_ref[...]
    vn = v * lax.rsqrt(jnp.mean(v * v, axis=-1, keepdims=True) + EPS) * g_ref[...]
    vn_ref[...] = vn
    o_ref[...] = (u_ref[...] * (vn * w0_ref[...] + b0_ref[...])).astype(o_ref.dtype)


def _sgu_sample(uv, norm_g, w_s, b_s, name):
    m, two_w = uv.shape
    width = two_w // 2
    group_dim = width // SG_GROUPS
    w0 = jnp.repeat(w_s[:, 0, 0], group_dim).reshape(1, width)
    b0 = jnp.repeat(b_s[:, 0], group_dim).reshape(1, width)
    row = pl.BlockSpec((1, width), lambda i: (0, 0))
    return pl.pallas_call(
        _sgu_sample_kernel,
        grid=(1,),
        in_specs=[pl.BlockSpec((m, width), lambda i: (0, 0)), pl.BlockSpec((m, width), lambda i: (0, 1)),
                  row, row, row],
        out_specs=[pl.BlockSpec((m, width), lambda i: (0, 0))] * 2,
        out_shape=[jax.ShapeDtypeStruct((m, width), bf16), jax.ShapeDtypeStruct((m, width), f32)],
        compiler_params=_params("arbitrary"),
        name=name,
    )(uv, uv, norm_g.reshape(1, width), w0, b0)


CONV_HALO = SUBLANES


def _ffn_up_kernel(xp_ref, xs_ref, wa_ref, wb_ref, cwa_ref, cwb_ref, cba_ref, cbb_ref,
                   h0a_ref, h0b_ref, h1a_ref, h1b_ref,
                   act_ref, ta_ref, tb_ref, acts_ref, upa_ref, upb_ref,
                   wabf_ref, wbbf_ref, bufa, bufb, *, tm, tiles_per_seq, n_m):
    mi = pl.program_id(1)

    @pl.when(mi == 0)
    def _():
        _cast_weight_tile(wa_ref, wabf_ref)
        _cast_weight_tile(wb_ref, wbbf_ref)

    @pl.when(mi % tiles_per_seq == 0)
    def _():
        bufa[...] = jnp.zeros_like(bufa)
        bufb[...] = jnp.zeros_like(bufb)

    x = xp_ref[...]
    halves = []
    for wbf_ref, cw_ref, cb_ref, halo_ref, tail_ref in ((wabf_ref, cwa_ref, cba_ref, bufa, ta_ref),
                                                        (wbbf_ref, cwb_ref, cbb_ref, bufb, tb_ref)):
        up = _dot(x, wbf_ref[...])
        w0, w1, w2, cb = cw_ref[0:1, :], cw_ref[1:2, :], cw_ref[2:3, :], cb_ref[...]
        t = up * w1 + pltpu.roll(up * w0, 1, 0)
        conv = cb + up * w2 + pltpu.roll(t, 1, 0)
        ext = jnp.concatenate([halo_ref[...], up[0:CONV_HALO, :]], axis=0)
        lo = CONV_HALO
        head = (cb + ext[lo:2 * lo, :] * w2 + ext[lo - 1:2 * lo - 1, :] * w1 + ext[lo - 2:2 * lo - 2, :] * w0)
        halves.append(jnp.concatenate([head, conv[CONV_HALO:, :]], axis=0))
        tail_ref[0] = up[tm - (CONV_W - 1):, :]
        halo_ref[...] = up[tm - CONV_HALO:, :]
    act_ref[...] = (_silu(halves[0]) * halves[1]).astype(act_ref.dtype)

    @pl.when(mi == n_m - 1)
    def _():
        xs = xs_ref[...]
        halves = []
        for wbf_ref, cw_ref, cb_ref, h0_ref, h1_ref, up_ref in (
                (wabf_ref, cwa_ref, cba_ref, h0a_ref, h1a_ref, upa_ref),
                (wbbf_ref, cwb_ref, cbb_ref, h0b_ref, h1b_ref, upb_ref)):
            up = _dot(xs, wbf_ref[...])
            up_ref[...] = up
            conv = cb_ref[...] + h0_ref[...] * cw_ref[0:1, :]
            conv = conv + h1_ref[...] * cw_ref[1:2, :]
            halves.append(conv + up * cw_ref[2:3, :])
        acts_ref[...] = (_silu(halves[0]) * halves[1]).astype(acts_ref.dtype)


def _ffn_up(hn_p, hn_s, w_up, conv_w, conv_b, hist0, hist1, batch, seq, name):
    mp, d = hn_p.shape
    ms = hn_s.shape[0]
    d_ff = w_up.shape[1] // 2
    tm = _tile(seq, (1024, 512, 256, 128))
    tn = _tile(d_ff, (256, 128))
    nb = d_ff // tn
    n_m = mp // tm
    tiles_per_seq = seq // tm
    col = lambda shape, off: pl.BlockSpec(shape, lambda j, i: (0, j + off))
    tail = pl.BlockSpec((1, CONV_W - 1, tn), lambda j, i: (i // tiles_per_seq, 0, j))
    tail_shape = jax.ShapeDtypeStruct((batch, CONV_W - 1, d_ff), f32)
    up_shape = jax.ShapeDtypeStruct((ms, d_ff), f32)
    buf = pltpu.VMEM((CONV_HALO, tn), f32)
    return pl.pallas_call(
        functools.partial(_ffn_up_kernel, tm=tm, tiles_per_seq=tiles_per_seq, n_m=n_m),
        grid=(nb, n_m),
        in_specs=[pl.BlockSpec((tm, d), lambda j, i: (i, 0)), pl.BlockSpec((ms, d), lambda j, i: (0, 0)),
                  col((d, tn), 0), col((d, tn), nb),
                  col((CONV_W, tn), 0), col((CONV_W, tn), nb), col((1, tn), 0), col((1, tn), nb),
                  col((ms, tn), 0), col((ms, tn), nb), col((ms, tn), 0), col((ms, tn), nb)],
        out_specs=[pl.BlockSpec((tm, tn), lambda j, i: (i, j)), tail, tail,
                   col((ms, tn), 0), col((ms, tn), 0), col((ms, tn), 0)],
        out_shape=[jax.ShapeDtypeStruct((mp, d_ff), bf16), tail_shape, tail_shape,
                   jax.ShapeDtypeStruct((ms, d_ff), bf16), up_shape, up_shape],
        scratch_shapes=[pltpu.VMEM((d, tn), bf16), pltpu.VMEM((d, tn), bf16), buf, buf],
        compiler_params=_params("parallel", "arbitrary"),
        name=name,
    )(hn_p, hn_s, w_up, w_up, conv_w, conv_w, conv_b, conv_b, hist0, hist0, hist1, hist1)


def kernel(x_prompt, x_sample, cache_k_cmp, cache_v_cmp, cache_k_slc, cache_v_slc, cache_k_win, cache_v_win,
           state_ffn_conv, page_table, norm1_g, w_in, cmp_pool_k, cmp_bias_k, cmp_w_k, cmp_pool_v, cmp_bias_v,
           cmp_w_v, sg_norm_g, sg_w, sg_b, w_proj_a, w_proj_b, w_out, norm2_g, w_up, conv_w, conv_b, w_down,
           norm_f_g):
    batch, seq, d_model = x_prompt.shape
    n_dec, dec_seq, _ = x_sample.shape
    depth = w_in.shape[0]
    page = cache_k_cmp.shape[2]
    past = page_table.shape[1] * page
    assert dec_seq == 1 and cache_k_win.shape[2] == WINDOW and past >= WINDOW
    assert seq % SG_CHUNK == 0 and seq % L_SEL == 0 and past % L_SEL == 0
    nblk_p = max(seq // L_SEL, N_SEL)
    nblk_s = -(-max(-(-(past + 1) // L_SEL), N_SEL) // SUBLANES) * SUBLANES

    tables_p = _rope_tables(jnp.arange(seq), GROUP)
    tables_s_kv = _rope_tables(jnp.full((n_dec,), past), GROUP)
    tables_s_q = _rope_tables(jnp.full((SUBLANES,), past), 1)

    hp = x_prompt.reshape(batch * seq, d_model)
    hs = x_sample.reshape(n_dec, d_model)
    new_p = [[] for _ in range(7)]
    new_s = [[] for _ in range(8)]
    for l in range(depth):
        pwk, bk, pwv, bv = cmp_pool_k[l], cmp_bias_k[l], cmp_pool_v[l], cmp_bias_v[l]
        cw = conv_w[l]
        cb = conv_b[l].reshape(1, -1)

        xn_p = _rmsnorm(hp, norm1_g[l], bf16, "norm1_p")
        xn_s = _rmsnorm(hs, norm1_g[l], bf16, "norm1_s")
        w_in_nk = jnp.swapaxes(w_in[l], 0, 1)
        z1_p, z1_s = _matmul_wres(xn_p, xn_s, w_in_nk, Z1_W, (f32, f32), "in_proj_qkv", w_is_nk=True)
        uv_p, uv_s = _matmul_wres(xn_p, xn_s, w_in_nk, d_model, (bf16, f32), "in_proj_uv", w_is_nk=True,
                                  col0=QKV_W + NSA_GATE_W, epilogue=_gelu_tanh)
        gm_p, gm_s = _matmul_wres(xn_p, xn_s, w_in_nk, 2 * d_model, (bf16, bf16), "in_proj_gate", w_is_nk=True,
                                  col0=QKV_W + NSA_GATE_W + d_model, epilogue=_sigmoid)

        kc_p, vc_p, ks_p, vs_p, kw_p, vw_p, ksb, vsb, kwb, vwb = _prep_kv(z1_p, tables_p, seq, "prep_p")
        ak, av = _compress_prompt(kc_p, vc_p, pwk, bk, pwv, bv, batch, seq, "compress_p")
        ocmp, sel = _cmp_select_prompt(z1_p, ak, av, cmp_w_k[l], cmp_w_v[l], batch=batch, seq=seq, nblk=nblk_p,
                                       name="cmp_select_p")
        gates = z1_p[:, QKV_W:QKV_W + NSA_GATE_W].reshape(batch * seq, N_KV_HEADS, 3 * GROUP).transpose(1, 0, 2)
        nsa_p = _nsa_prompt(z1_p, tables_p, ksb, vsb, kwb, vwb, sel, ocmp, gates, batch=batch, seq=seq,
                            nblk=nblk_p, name="nsa_p")
        sg_p = _sgu_prompt(uv_p, sg_norm_g[l], sg_w[l], sg_b[l], "sgu_p")

        kc_s, vc_s, ks_s, vs_s, kw_s, vw_s = _prep_kv(z1_s, tables_s_kv, n_dec, "prep_s")[:6]
        ak, av = _compress_paged(cache_k_cmp[l], cache_v_cmp[l], page_table, pwk, bk, pwv, bv, "compress_s")
        heads_as_rows = lambda t, w: jnp.pad(t.reshape(n_dec, N_KV_HEADS, GROUP, w),
                                             ((0, 0), (0, 0), (0, SUBLANES - GROUP), (0, HEAD_DIM - w)))
        q_s = heads_as_rows(z1_s[:, :Q_W], HEAD_DIM)
        ocmp, imp = _cmp_sample(q_s, ak, av, cmp_w_k[l], cmp_w_v[l], n_batch=n_dec, nrows=past // STRIDE,
                                nblk=nblk_s, pos=past, name="cmp_s")
        imp_t = imp.transpose(1, 0, 2).reshape(nblk_s, n_dec * SUBLANES)
        idx = _rank_sample(imp_t, nblk=nblk_s, qb=past // L_SEL, name="rank_s")
        idx = idx.T.reshape(n_dec, SUBLANES, 2 * N_SEL)[:, :N_KV_HEADS].reshape(-1)
        new_rows = lambda t: t.reshape(n_dec, N_KV_HEADS, HEAD_DIM)
        cache_rows = lambda t: t.reshape(n_dec, WINDOW * N_KV_HEADS, HEAD_DIM)
        win_k, win_v = _window_update(cache_rows(cache_k_win[l]), cache_rows(cache_v_win[l]), new_rows(kw_s),
                                      new_rows(vw_s), "window_s")
        gates = heads_as_rows(z1_s[:, QKV_W:QKV_W + NSA_GATE_W], 3)
        nsa_s = _nsa_sample(idx, page_table, q_s, tables_s_q, new_rows(ks_s), new_rows(vs_s),
                            win_k.reshape(-1, HEAD_DIM), win_v.reshape(-1, HEAD_DIM), ocmp, gates,
                            cache_k_slc[l], cache_v_slc[l], pos=past, name="nsa_s")
        nsa_s = nsa_s[:, :, :GROUP].reshape(n_dec, Q_W).astype(bf16)
        sg_s, sg_v = _sgu_sample(uv_s, sg_norm_g[l], sg_w[l], sg_b[l], "sgu_s")

        m_p, m_s = _merge(nsa_p, nsa_s, sg_p, sg_s, w_proj_a[l], w_proj_b[l], gm_p, gm_s, "merge")
        h_p, h_s = _matmul_wres(m_p, m_s, w_out[l], d_model, (f32, f32), "out_proj",
                                epilogue=lambda acc, x: x + acc, extras_p=(hp,), extras_s=(hs,))
        hn_p = _rmsnorm(h_p, norm2_g[l], bf16, "norm2_p")
        hn_s = _rmsnorm(h_s, norm2_g[l], bf16, "norm2_s")
        hist = state_ffn_conv[l]
        act_p, tail_a, tail_b, act_s, up_a, up_b = _ffn_up(hn_p, hn_s, w_up[l], cw, cb, hist[:, 0], hist[:, 1],
                                                           batch, seq, "ffn_up")
        wdown = w_down[l].astype(bf16)
        hp = _matmul_residual(act_p, wdown, h_p, "ffn_down_p")
        hs = _matmul_residual(act_s, wdown, h_s, "ffn_down_s")

        kv5 = lambda t: t.reshape(batch, seq, N_KV_HEADS, HEAD_DIM)
        keep_p = min(WINDOW, seq)
        for lst, val in zip(new_p, (kv5(kc_p), kv5(vc_p), kv5(ks_p), kv5(vs_p), kv5(kw_p)[:, seq - keep_p:],
                                    kv5(vw_p)[:, seq - keep_p:], jnp.concatenate([tail_a, tail_b], axis=-1))):
            lst.append(val)
        kv5 = lambda t: t.reshape(n_dec, 1, N_KV_HEADS, HEAD_DIM)
        conv_s = jnp.stack([hist[:, 1], jnp.concatenate([up_a, up_b], axis=-1)], axis=1)
        win5 = lambda t: t.reshape(n_dec, WINDOW, N_KV_HEADS, HEAD_DIM)
        for lst, val in zip(new_s, (kv5(kc_s), kv5(vc_s), kv5(ks_s), kv5(vs_s), win5(win_k), win5(win_v),
                                    sg_v.reshape(n_dec, 1, -1), conv_s)):
            lst.append(val)

    y_prompt = _rmsnorm(hp, norm_f_g, f32, "norm_f_p").reshape(batch, seq, d_model)
    y_sample = _rmsnorm(hs, norm_f_g, f32, "norm_f_s").reshape(n_dec, 1, d_model)
    return (y_prompt, y_sample, *(jnp.stack(v) for v in new_p), *(jnp.stack(v) for v in new_s))
```
